```python
import math
import jax
import jax.numpy as jnp
from jax import lax
import numpy as np

D_MODEL = 4096
BATCH = 2
SEQ = 4096
DEPTH = 1

CTX_LEN = 256
GRID_W = 64
N_MOD = 6
EPS = 1e-6

N_Q_HEADS = 32
N_KV_HEADS = 8
HEAD_DIM = 128
GQA_GROUP = N_Q_HEADS // N_KV_HEADS
Q_BLOCK = 128
ROPE_THETA = 10000.0
ROPE_AXIS_FREQS = HEAD_DIM // 4

SSD_INNER = D_MODEL
SSD_HEAD_DIM = 64
SSD_HEADS = SSD_INNER // SSD_HEAD_DIM
SSD_GROUPS = 8
SSD_HEADS_PER_GROUP = SSD_HEADS // SSD_GROUPS
SSD_STATE = 128
CONV_W = 5
CHUNK = 128

ATTN_Q_DIM = N_Q_HEADS * HEAD_DIM
ATTN_KV_DIM = N_KV_HEADS * HEAD_DIM
BC_DIM = SSD_GROUPS * SSD_STATE
XBC_DIM = SSD_INNER + 2 * BC_DIM
CTX_COLS = 2 * ATTN_KV_DIM + XBC_DIM + 2 * SSD_HEADS
IN_COLS = CTX_COLS + ATTN_Q_DIM + 2 * D_MODEL + SSD_INNER

N_GROUPS = 4
EXPERTS_PER_GROUP = 8
N_EXPERTS = N_GROUPS * EXPERTS_PER_GROUP
TOP_K = 2
D_EXPERT = D_MODEL // 4
MOE_BLOCK = 128

kernel_name = 'hybrid_gqa_ssd_hmoe_dit_block'


def rms_norm(x, w):
    xf = x.astype(jnp.float32)
    xf = xf * lax.rsqrt(jnp.mean(xf * xf, axis=-1, keepdims=True) + EPS)
    return xf.astype(x.dtype) * w


def modulate(h, shift, scale):
    return h * (1.0 + scale) + shift


def adaln(cond, p):
    mod = jax.nn.silu(cond) @ p['w_ada'] + p['b_ada']
    return jnp.split(mod, N_MOD, axis=-1)


def rope_tables(seq_len):
    rows = seq_len // GRID_W
    row_pos = jnp.repeat(jnp.arange(rows, dtype=jnp.float32), GRID_W)
    col_pos = (jnp.arange(seq_len) % GRID_W).astype(jnp.float32)
    inv_freq = ROPE_THETA ** (-jnp.arange(ROPE_AXIS_FREQS, dtype=jnp.float32) / ROPE_AXIS_FREQS)
    ang_r = row_pos[:, None] * inv_freq
    ang_c = col_pos[:, None] * inv_freq
    return (jnp.cos(ang_r), jnp.sin(ang_r), jnp.cos(ang_c), jnp.sin(ang_c))


def rotate_pairs(x, cos, sin):
    f = x.shape[-1] // 2
    x1, x2 = x[..., :f], x[..., f:]
    c, s = cos[:, None, :], sin[:, None, :]
    return jnp.concatenate([x1 * c - x2 * s, x2 * c + x1 * s], axis=-1)


def apply_rope_2d(x, rope):
    cos_r, sin_r, cos_c, sin_c = rope
    half = HEAD_DIM // 2
    xr = rotate_pairs(x[..., :half], cos_r, sin_r)
    xc = rotate_pairs(x[..., half:], cos_c, sin_c)
    return jnp.concatenate([xr, xc], axis=-1).astype(x.dtype)


def centred_dwconv(x, w, b):
    pad = CONV_W // 2
    length = x.shape[1]
    xp = jnp.pad(x, ((0, 0), (pad, pad), (0, 0)))
    out = xp[:, 0:length] * w[0]
    for j in range(1, CONV_W):
        out = out + xp[:, j:j + length] * w[j]
    return out + b


def kv_ssd_inputs(part, p):
    b, l = part.shape[:2]
    o1, o2 = ATTN_KV_DIM, 2 * ATTN_KV_DIM
    o3 = o2 + XBC_DIM
    k = rms_norm(part[..., :o1].reshape(b, l, N_KV_HEADS, HEAD_DIM), p['k_norm_w'])
    v = part[..., o1:o2].reshape(b, l, N_KV_HEADS, HEAD_DIM)
    xbc = jax.nn.silu(centred_dwconv(part[..., o2:o3], p['conv_w'], p['conv_b']))
    xs = xbc[..., :SSD_INNER].reshape(b, l, SSD_GROUPS, SSD_HEADS_PER_GROUP, SSD_HEAD_DIM)
    bm = xbc[..., SSD_INNER:SSD_INNER + BC_DIM].reshape(b, l, SSD_GROUPS, SSD_STATE)
    cm = xbc[..., SSD_INNER + BC_DIM:].reshape(b, l, SSD_GROUPS, SSD_STATE)
    dt = part[..., o3:].astype(jnp.float32)
    dtf = jax.nn.softplus(dt[..., :SSD_HEADS] + p['dt_bias_f'].astype(jnp.float32))
    dtb = jax.nn.softplus(dt[..., SSD_HEADS:] + p['dt_bias_b'].astype(jnp.float32))
    dtf = dtf.reshape(b, l, SSD_GROUPS, SSD_HEADS_PER_GROUP)
    dtb = dtb.reshape(b, l, SSD_GROUPS, SSD_HEADS_PER_GROUP)
    return k, v, xs, bm, cm, dtf, dtb


def q_gate_z(proj):
    o1 = CTX_COLS
    o2 = o1 + ATTN_Q_DIM
    o3 = o2 + 2 * D_MODEL
    return proj[..., o1:o2], proj[..., o2:o3], proj[..., o3:]


def block_attention(q, k, v):
    b, s = q.shape[:2]
    n_blk = s // Q_BLOCK
    qb = q.reshape(b, n_blk, Q_BLOCK, N_KV_HEADS, GQA_GROUP, HEAD_DIM).transpose(1, 0, 2, 3, 4, 5)
    scale = HEAD_DIM ** -0.5

    def one_block(qi):
        sc = jnp.einsum('bqkgd,blkd->bkgql', qi, k).astype(jnp.float32) * scale
        pr = jax.nn.softmax(sc, axis=-1).astype(v.dtype)
        return jnp.einsum('bkgql,blkd->bqkgd', pr, v)

    o = lax.map(one_block, qb)
    return o.transpose(1, 0, 2, 3, 4, 5).reshape(b, s, ATTN_Q_DIM)


def ssd_chunked(xs, dt, a, bm, cm, h0):
    b, length = xs.shape[:2]
    nc = length // CHUNK

    def to_chunks(t):
        return jnp.moveaxis(t.reshape((b, nc, CHUNK) + t.shape[2:]), 1, 0)

    lower = jnp.tril(jnp.ones((CHUNK, CHUNK), dtype=bool))[None, :, :, None, None]

    def step(h, inp):
        xc, dtc, bc, cc = inp
        acum = jnp.cumsum(dtc * a, axis=1)
        seg = acum[:, :, None] - acum[:, None, :]
        decay = jnp.exp(jnp.where(lower, seg, -jnp.inf))
        cb = jnp.einsum('bign,bjgn->bijg', cc, bc)
        wmat = cb[..., None] * decay * dtc[:, None]
        y = jnp.einsum('bijgr,bjgrp->bigrp', wmat, xc)
        y = y + jnp.einsum('bign,bgrpn->bigrp', cc, h) * jnp.exp(acum)[..., None]
        to_end = jnp.exp(acum[:, -1:] - acum) * dtc
        h_new = h * jnp.exp(acum[:, -1])[..., None, None] + jnp.einsum('bjgn,bjgr,bjgrp->bgrpn', bc, to_end, xc)
        return h_new, y

    h_fin, ys = lax.scan(step, h0, (to_chunks(xs), to_chunks(dt), to_chunks(bm), to_chunks(cm)))
    y = jnp.moveaxis(ys, 0, 1).reshape(xs.shape).astype(xs.dtype)
    return y, h_fin


def ssd_final_state(xs, dt, a, bm):
    acum = jnp.cumsum(dt * a, axis=1)
    to_end = jnp.exp(acum[:, -1:] - acum) * dt
    return jnp.einsum('blgn,blgr,blgrp->bgrpn', bm, to_end, xs)


def flip_seq(t):
    return jnp.flip(t, axis=1)


def merge_branches(attn, y_ssd, xs, z, gates, p):
    b, l = attn.shape[:2]
    y = (y_ssd + p['d_skip'].reshape(SSD_GROUPS, SSD_HEADS_PER_GROUP, 1) * xs).reshape(b, l, SSD_INNER)
    y = rms_norm(y * jax.nn.silu(z), p['ssd_norm_w'])
    g = jax.nn.sigmoid(gates.astype(jnp.float32)).astype(attn.dtype)
    merged = g[..., :D_MODEL] * (attn @ p['w_attn_proj']) + g[..., D_MODEL:] * (y @ p['w_ssd_proj'])
    return merged @ p['w_out']


def hier_moe(h, p):
    t = h.shape[0]
    g_logits = (h @ p['w_router_group']).astype(jnp.float32) + p['b_router_group'].astype(jnp.float32)
    g_prob, g_idx = lax.top_k(jax.nn.softmax(g_logits, axis=-1), 1)
    e_logits = (h @ p['w_router_expert']).astype(jnp.float32) + p['b_router_expert'].astype(jnp.float32)
    e_logits = e_logits.reshape(t, N_GROUPS, EXPERTS_PER_GROUP)
    e_logits = jnp.take_along_axis(e_logits, g_idx[:, :, None], axis=1)[:, 0]
    e_prob, e_idx = lax.top_k(jax.nn.softmax(e_logits, axis=-1), TOP_K)
    weights = g_prob * e_prob / jnp.sum(e_prob, axis=-1, keepdims=True)
    expert_id = g_idx * EXPERTS_PER_GROUP + e_idx

    n_assign = t * TOP_K
    flat_e = expert_id.reshape(-1)
    flat_t = jnp.repeat(jnp.arange(t, dtype=jnp.int32), TOP_K)
    flat_w = weights.reshape(-1)
    order = jnp.argsort(flat_e)
    se, st, sw = flat_e[order], flat_t[order], flat_w[order]
    counts = jnp.bincount(flat_e, length=N_EXPERTS)
    starts = jnp.cumsum(counts) - counts
    padded = (counts + MOE_BLOCK - 1) // MOE_BLOCK * MOE_BLOCK
    pends = jnp.cumsum(padded)
    pstarts = pends - padded
    dest = pstarts[se] + jnp.arange(n_assign) - starts[se]
    n_rows = (-(-n_assign // MOE_BLOCK) + N_EXPERTS) * MOE_BLOCK
    n_blocks = n_rows // MOE_BLOCK
    row_tok = jnp.zeros((n_rows,), jnp.int32).at[dest].set(st)
    row_w = jnp.zeros((n_rows,), h.dtype).at[dest].set(sw.astype(h.dtype))
    blk_e = jnp.minimum(jnp.searchsorted(pends, jnp.arange(n_blocks) * MOE_BLOCK, side='right'), N_EXPERTS - 1)
    xb = h[row_tok].reshape(n_blocks, MOE_BLOCK, h.shape[1])

    def expert_block(args):
        xi, e = args
        a = xi @ p['w_exp_gate'][e]
        u = xi @ p['w_exp_up'][e]
        return (jax.nn.silu(a) * u) @ p['w_exp_down'][e]

    yb = lax.map(expert_block, (xb, blk_e)).reshape(n_rows, h.shape[1])
    return jnp.zeros_like(h).at[row_tok].add(yb * row_w[:, None])


def trunk_layer(x, ctx, c, c_ctx, p, rope, update_ctx):
    b, s, d = x.shape
    a_f = -jnp.exp(p['a_log_f'].astype(jnp.float32)).reshape(SSD_GROUPS, SSD_HEADS_PER_GROUP)
    a_b = -jnp.exp(p['a_log_b'].astype(jnp.float32)).reshape(SSD_GROUPS, SSD_HEADS_PER_GROUP)
    sh_m, sc_m, gt_m, sh_f, sc_f, gt_f = [m[:, None, :] for m in adaln(c, p)]
    csh_m, csc_m, cgt_m, csh_f, csc_f, cgt_f = adaln(c_ctx, p)

    hc = modulate(rms_norm(ctx, p['norm_mix_w']), csh_m, csc_m)
    w_in_ctx = p['w_in'] if update_ctx else p['w_in'][:, :CTX_COLS]
    proj_c = hc @ w_in_ctx
    kc, vc, xs_c, b_c, c_c, dtf_c, dtb_c = kv_ssd_inputs(proj_c[..., :CTX_COLS], p)
    if update_ctx:
        q_c, gates_c, z_c = q_gate_z(proj_c)
        q_c = rms_norm(q_c.reshape(b, CTX_LEN, N_Q_HEADS, HEAD_DIM), p['q_norm_w'])
        attn_c = block_attention(q_c, kc, vc)
        h0 = jnp.zeros((b, SSD_GROUPS, SSD_HEADS_PER_GROUP, SSD_HEAD_DIM, SSD_STATE), jnp.float32)
        yf_c, hf_c = ssd_chunked(xs_c, dtf_c, a_f, b_c, c_c, h0)
        yb_c, hb_c = ssd_chunked(flip_seq(xs_c), flip_seq(dtb_c), a_b, flip_seq(b_c), flip_seq(c_c), h0)
        ctx_next = ctx + cgt_m * merge_branches(attn_c, yf_c + flip_seq(yb_c), xs_c, z_c, gates_c, p)
        hc2 = modulate(rms_norm(ctx_next, p['norm_ffn_w']), csh_f, csc_f)
        ctx_next = ctx_next + cgt_f * hier_moe(hc2.reshape(-1, d), p).reshape(ctx.shape)
    else:
        hf_c = ssd_final_state(xs_c, dtf_c, a_f, b_c)
        hb_c = ssd_final_state(flip_seq(xs_c), flip_seq(dtb_c), a_b, flip_seq(b_c))
        ctx_next = ctx

    hx = modulate(rms_norm(x, p['norm_mix_w']), sh_m, sc_m)
    proj = hx @ p['w_in']
    kx, vx, xs, bm, cm, dtf, dtb = kv_ssd_inputs(proj[..., :CTX_COLS], p)
    q, gates, z = q_gate_z(proj)
    q = apply_rope_2d(rms_norm(q.reshape(b, s, N_Q_HEADS, HEAD_DIM), p['q_norm_w']), rope)
    kx = apply_rope_2d(kx, rope)
    attn = block_attention(q, jnp.concatenate([kx, kc], axis=1), jnp.concatenate([vx, vc], axis=1))
    yf, _ = ssd_chunked(xs, dtf, a_f, bm, cm, hf_c)
    yb, _ = ssd_chunked(flip_seq(xs), flip_seq(dtb), a_b, flip_seq(bm), flip_seq(cm), hb_c)
    x = x + gt_m * merge_branches(attn, yf + flip_seq(yb), xs, z, gates, p)

    hx2 = modulate(rms_norm(x, p['norm_ffn_w']), sh_f, sc_f)
    x = x + gt_f * hier_moe(hx2.reshape(-1, d), p).reshape(b, s, d)
    return x, ctx_next


def setup_inputs(seed: int = 0) -> dict:
    key = jax.random.key(seed)
    ks = jax.random.split(key, 32)
    f32 = jnp.float32
    L = DEPTH

    def nrm(k, shape, scale):
        return jax.random.normal(k, shape, f32) * scale

    def gain(k, shape):
        return 1.0 + 0.01 * jax.random.normal(k, shape, f32)

    def dt_bias(k):
        dt0 = jnp.exp(jax.random.uniform(k, (L, SSD_HEADS), f32, math.log(1e-3), math.log(1e-1)))
        return dt0 + jnp.log(-jnp.expm1(-dt0))

    dinv = D_MODEL ** -0.5
    return {
        'x': nrm(ks[0], (BATCH, SEQ, D_MODEL), 1.0),
        'c': nrm(ks[1], (BATCH, D_MODEL), 1.0),
        'ctx': nrm(ks[2], (BATCH, CTX_LEN, D_MODEL), 1.0),
        'c_ctx': nrm(ks[3], (D_MODEL,), 1.0),
        'w_ada': nrm(ks[4], (L, D_MODEL, N_MOD * D_MODEL), 0.5 * dinv),
        'b_ada': nrm(ks[5], (L, N_MOD * D_MODEL), 0.01),
        'norm_mix_w': gain(ks[6], (L, D_MODEL)),
        'norm_ffn_w': gain(ks[7], (L, D_MODEL)),
        'w_in': nrm(ks[8], (L, D_MODEL, IN_COLS), dinv),
        'q_norm_w': gain(ks[9], (L, HEAD_DIM)),
        'k_norm_w': gain(ks[10], (L, HEAD_DIM)),
        'conv_w': nrm(ks[11], (L, CONV_W, XBC_DIM), CONV_W ** -0.5),
        'conv_b': nrm(ks[12], (L, XBC_DIM), 0.01),
        'a_log_f': jnp.log(jax.random.uniform(ks[13], (L, SSD_HEADS), f32, 1.0, 16.0)),
        'a_log_b': jnp.log(jax.random.uniform(ks[14], (L, SSD_HEADS), f32, 1.0, 16.0)),
        'dt_bias_f': dt_bias(ks[15]),
        'dt_bias_b': dt_bias(ks[16]),
        'd_skip': gain(ks[17], (L, SSD_HEADS)),
        'ssd_norm_w': gain(ks[18], (L, SSD_INNER)),
        'w_attn_proj': nrm(ks[19], (L, ATTN_Q_DIM, D_MODEL), ATTN_Q_DIM ** -0.5),
        'w_ssd_proj': nrm(ks[20], (L, SSD_INNER, D_MODEL), SSD_INNER ** -0.5),
        'w_out': nrm(ks[21], (L, D_MODEL, D_MODEL), dinv),
        'w_router_group': nrm(ks[22], (L, D_MODEL, N_GROUPS), dinv),
        'b_router_group': nrm(ks[23], (L, N_GROUPS), 0.01),
        'w_router_expert': nrm(ks[24], (L, D_MODEL, N_EXPERTS), dinv),
        'b_router_expert': nrm(ks[25], (L, N_EXPERTS), 0.01),
        'w_exp_gate': nrm(ks[26], (L, N_EXPERTS, D_MODEL, D_EXPERT), dinv),
        'w_exp_up': nrm(ks[27], (L, N_EXPERTS, D_MODEL, D_EXPERT), dinv),
        'w_exp_down': nrm(ks[28], (L, N_EXPERTS, D_EXPERT, D_MODEL), D_EXPERT ** -0.5),
    }


def reference(x, c, ctx, c_ctx, w_ada, b_ada, norm_mix_w, norm_ffn_w, w_in, q_norm_w, k_norm_w,
              conv_w, conv_b, a_log_f, a_log_b, dt_bias_f, dt_bias_b, d_skip, ssd_norm_w,
              w_attn_proj, w_ssd_proj, w_out, w_router_group, b_router_group, w_router_expert,
              b_router_expert, w_exp_gate, w_exp_up, w_exp_down):
    rope = rope_tables(x.shape[1])
    for i in range(DEPTH):
        p = {
            'w_ada': w_ada[i], 'b_ada': b_ada[i],
            'norm_mix_w': norm_mix_w[i], 'norm_ffn_w': norm_ffn_w[i],
            'w_in': w_in[i], 'q_norm_w': q_norm_w[i], 'k_norm_w': k_norm_w[i],
            'conv_w': conv_w[i], 'conv_b': conv_b[i],
            'a_log_f': a_log_f[i], 'a_log_b': a_log_b[i],
            'dt_bias_f': dt_bias_f[i], 'dt_bias_b': dt_bias_b[i],
            'd_skip': d_skip[i], 'ssd_norm_w': ssd_norm_w[i],
            'w_attn_proj': w_attn_proj[i], 'w_ssd_proj': w_ssd_proj[i], 'w_out': w_out[i],
            'w_router_group': w_router_group[i], 'b_router_group': b_router_group[i],
            'w_router_expert': w_router_expert[i], 'b_router_expert': b_router_expert[i],
            'w_exp_gate': w_exp_gate[i], 'w_exp_up': w_exp_up[i], 'w_exp_down': w_exp_down[i],
        }
        x, ctx = trunk_layer(x, ctx, c, c_ctx, p, rope, i < DEPTH - 1)
    return x
```

```python
import functools

import jax
import jax.numpy as jnp
from jax import lax
from jax.experimental import pallas as pl
from jax.experimental.pallas import tpu as pltpu

F32 = jnp.float32
BF16 = jnp.bfloat16

N_MOD = 6
EPS = 1e-6
GRID_W = 64
N_Q_HEADS = 32
N_KV_HEADS = 8
HEAD_DIM = 128
GQA_GROUP = N_Q_HEADS // N_KV_HEADS
ROPE_THETA = 10000.0
ROPE_AXIS_FREQS = HEAD_DIM // 4
SSD_HEAD_DIM = 64
SSD_GROUPS = 8
SSD_STATE = 128
CONV_W = 5
CHUNK = 128
N_GROUPS = 4
EXPERTS_PER_GROUP = 8
N_EXPERTS = N_GROUPS * EXPERTS_PER_GROUP
TOP_K = 2

LANES = 128
HALO = 16
MOE_BLK = 256
VMEM_LIMIT = 56 * 1024 * 1024


def _cparams(sem):
    return pltpu.CompilerParams(dimension_semantics=sem, vmem_limit_bytes=VMEM_LIMIT)


def _tile(n, pref):
    t = min(n, pref)
    while n % t:
        t //= 2
    return t


def _silu(v):
    return v * jax.nn.sigmoid(v)


def _split3(v):
    hi = v.astype(BF16)
    r1 = v - hi.astype(F32)
    mid = r1.astype(BF16)
    lo = (r1 - mid.astype(F32)).astype(BF16)
    return hi, mid, lo


def _dot_exact_lhs(m_bf16, v, dims=(((1,), (0,)), ((), ()))):
    hi, mid, lo = _split3(v)
    out = lax.dot_general(m_bf16, hi, dims, preferred_element_type=F32)
    out = out + lax.dot_general(m_bf16, mid, dims, preferred_element_type=F32)
    return out + lax.dot_general(m_bf16, lo, dims, preferred_element_type=F32)


def _adaln_kernel(c_ref, w_ref, b_ref, o_ref):
    s = _silu(c_ref[...]).astype(BF16)
    o_ref[...] = jnp.dot(s, w_ref[...].astype(BF16), preferred_element_type=F32) + b_ref[...]


def _adaln(cond8, w_ada, b_ada):
    d, n = w_ada.shape
    tn = _tile(n, 512)
    return pl.pallas_call(
        _adaln_kernel,
        grid=(n // tn,),
        in_specs=[pl.BlockSpec((8, d), lambda j: (0, 0)),
                  pl.BlockSpec((d, tn), lambda j: (0, j)),
                  pl.BlockSpec((1, tn), lambda j: (0, j))],
        out_specs=pl.BlockSpec((8, tn), lambda j: (0, j)),
        out_shape=jax.ShapeDtypeStruct((8, n), F32),
        compiler_params=_cparams(("arbitrary",)),
        name="adaln",
    )(cond8, w_ada, b_ada.reshape(1, n))


def _norm_mod(xv, w, shift, scale):
    ms = jnp.mean(xv * xv, axis=-1, keepdims=True)
    h = xv * lax.rsqrt(ms + EPS) * w
    return h * (1.0 + scale) + shift


def _norm_mix_kernel(x_ref, c_ref, w_ref, m_ref, o_ref, *, n_x_tiles):
    i = pl.program_id(0)

    def emit(src):
        o_ref[...] = _norm_mod(src[...], w_ref[...], m_ref[0, 0:1, :], m_ref[0, 1:2, :]).astype(o_ref.dtype)

    @pl.when(i < n_x_tiles)
    def _():
        emit(x_ref)

    @pl.when(i >= n_x_tiles)
    def _():
        emit(c_ref)


def _norm_mix(x2, c2, w, mod3, seq):
    mx, d = x2.shape
    mc = c2.shape[0]
    tm = _tile(min(seq, mc), 256)
    nx, nc = mx // tm, mc // tm
    tpb = seq // tm
    return pl.pallas_call(
        functools.partial(_norm_mix_kernel, n_x_tiles=nx),
        grid=(nx + nc,),
        in_specs=[pl.BlockSpec((tm, d), lambda i: (jnp.minimum(i, nx - 1), 0)),
                  pl.BlockSpec((tm, d), lambda i: (jnp.maximum(i - nx, 0), 0)),
                  pl.BlockSpec((1, d), lambda i: (0, 0)),
                  pl.BlockSpec((1, N_MOD, d), lambda i: (jnp.where(i < nx, i // tpb, 2), 0, 0))],
        out_specs=pl.BlockSpec((tm, d), lambda i: (i, 0)),
        out_shape=jax.ShapeDtypeStruct((mx + mc, d), BF16),
        compiler_params=_cparams(("arbitrary",)),
        name="norm_mix",
    )(x2, c2, w.reshape(1, d), mod3)


def _mm_kernel(a_ref, w_ref, o_ref, wbf_ref):
    @pl.when(pl.program_id(1) == 0)
    def _():
        wbf_ref[...] = w_ref[...].astype(BF16)

    o_ref[...] = jnp.dot(a_ref[...], wbf_ref[...], preferred_element_type=F32).astype(o_ref.dtype)


def _mm(a, w, col0, ncols, m_rows, tm, tn, out_dtype, name):
    k = a.shape[1]
    assert ncols % tn == 0 and m_rows % tm == 0 and col0 % LANES == 0
    return pl.pallas_call(
        _mm_kernel,
        grid=(ncols // tn, m_rows // tm),
        in_specs=[pl.BlockSpec((tm, k), lambda j, i: (i, 0)),
                  pl.BlockSpec((pl.Element(k), pl.Element(tn)), lambda j, i: (0, pl.multiple_of(col0 + j * tn, LANES)))],
        out_specs=pl.BlockSpec((tm, tn), lambda j, i: (i, j)),
        out_shape=jax.ShapeDtypeStruct((m_rows, ncols), out_dtype),
        scratch_shapes=[pltpu.VMEM((k, tn), BF16)],
        compiler_params=_cparams(("arbitrary", "arbitrary")),
        name=name,
    )(a, w)


def _qk_post_kernel(x_ref, w_ref, cos_ref, sin_ref, o_ref, *, n_heads, scale):
    lane = lax.broadcasted_iota(jnp.int32, (x_ref.shape[0], HEAD_DIM), 1)
    first = (lane % (HEAD_DIM // 2)) < (HEAD_DIM // 4)
    for h in range(n_heads):
        sl = slice(h * HEAD_DIM, (h + 1) * HEAD_DIM)
        xh = x_ref[:, sl].astype(F32)
        ms = jnp.mean(xh * xh, axis=-1, keepdims=True)
        y = xh * lax.rsqrt(ms + EPS) * w_ref[...]
        partner = jnp.where(first, pltpu.roll(y, HEAD_DIM - HEAD_DIM // 4, 1), pltpu.roll(y, HEAD_DIM // 4, 1))
        y = y * cos_ref[...] + partner * sin_ref[...]
        o_ref[:, sl] = (y * scale).astype(o_ref.dtype)


def _qk_post(src, col_blk0, n_heads_total, rows, w, cos_t, sin_t, seq, n_x_rows, scale, name):
    tm = _tile(min(seq, rows), 256)
    hpb = 4
    tpb = seq // tm
    nx = n_x_rows // tm
    tab_map = lambda i, j: (jnp.where(i < nx, i % tpb, tpb), 0)
    return pl.pallas_call(
        functools.partial(_qk_post_kernel, n_heads=hpb, scale=scale),
        grid=(rows // tm, n_heads_total // hpb),
        in_specs=[pl.BlockSpec((tm, hpb * HEAD_DIM), lambda i, j: (i, col_blk0 + j)),
                  pl.BlockSpec((1, HEAD_DIM), lambda i, j: (0, 0)),
                  pl.BlockSpec((tm, HEAD_DIM), tab_map),
                  pl.BlockSpec((tm, HEAD_DIM), tab_map)],
        out_specs=pl.BlockSpec((tm, hpb * HEAD_DIM), lambda i, j: (i, j)),
        out_shape=jax.ShapeDtypeStruct((rows, n_heads_total * HEAD_DIM), BF16),
        compiler_params=_cparams(("arbitrary", "arbitrary")),
        name=name,
    )(src, w.reshape(1, HEAD_DIM), cos_t, sin_t)


def _rope_tables(seq, tm):
    rows = seq // GRID_W
    row_pos = jnp.repeat(jnp.arange(rows, dtype=F32), GRID_W)
    col_pos = (jnp.arange(seq) % GRID_W).astype(F32)
    inv_freq = ROPE_THETA ** (-jnp.arange(ROPE_AXIS_FREQS, dtype=F32) / ROPE_AXIS_FREQS)
    ar = row_pos[:, None] * inv_freq
    ac = col_pos[:, None] * inv_freq
    cos_t = jnp.concatenate([jnp.cos(ar), jnp.cos(ar), jnp.cos(ac), jnp.cos(ac)], axis=-1)
    sin_t = jnp.concatenate([-jnp.sin(ar), jnp.sin(ar), -jnp.sin(ac), jnp.sin(ac)], axis=-1)
    cos_t = jnp.concatenate([cos_t, jnp.ones((tm, HEAD_DIM), F32)], axis=0)
    sin_t = jnp.concatenate([sin_t, jnp.zeros((tm, HEAD_DIM), F32)], axis=0)
    return cos_t, sin_t


def _attn_kernel(q_ref, kx_ref, kc_ref, vx_ref, vc_ref, o_ref):
    nt = (((1,), (1,)), ((), ()))
    kx, kc, vx, vc = kx_ref[...], kc_ref[...], vx_ref[...], vc_ref[...]
    for g in range(GQA_GROUP):
        sl = slice(g * HEAD_DIM, (g + 1) * HEAD_DIM)
        q = q_ref[:, sl]
        s1 = lax.dot_general(q, kx, nt, preferred_element_type=F32)
        s2 = lax.dot_general(q, kc, nt, preferred_element_type=F32)
        m = jnp.maximum(jnp.max(s1, axis=-1, keepdims=True), jnp.max(s2, axis=-1, keepdims=True))
        p1 = jnp.exp(s1 - m)
        p2 = jnp.exp(s2 - m)
        l = jnp.sum(p1, axis=-1, keepdims=True) + jnp.sum(p2, axis=-1, keepdims=True)
        o = jnp.dot(p1.astype(BF16), vx, preferred_element_type=F32)
        o = o + jnp.dot(p2.astype(BF16), vc, preferred_element_type=F32)
        o_ref[:, sl] = (o / l).astype(o_ref.dtype)


def _attention(q, k, p1, batch, seq, ctx_len):
    tq = _tile(seq, 256)
    qpb = seq // tq
    gw = GQA_GROUP * HEAD_DIM
    cblk0 = batch * seq // ctx_len
    return pl.pallas_call(
        _attn_kernel,
        grid=(batch, N_KV_HEADS, qpb),
        in_specs=[pl.BlockSpec((tq, gw), lambda b, h, i: (b * qpb + i, h)),
                  pl.BlockSpec((seq, HEAD_DIM), lambda b, h, i: (b, h)),
                  pl.BlockSpec((ctx_len, HEAD_DIM), lambda b, h, i: (cblk0 + b, h)),
                  pl.BlockSpec((seq, HEAD_DIM), lambda b, h, i: (b, N_KV_HEADS + h)),
                  pl.BlockSpec((ctx_len, HEAD_DIM), lambda b, h, i: (cblk0 + b, N_KV_HEADS + h))],
        out_specs=pl.BlockSpec((tq, gw), lambda b, h, i: (b * qpb + i, h)),
        out_shape=jax.ShapeDtypeStruct(q.shape, BF16),
        compiler_params=_cparams(("arbitrary", "arbitrary", "arbitrary")),
        name="attention",
    )(q, k, k, p1, p1)


def _conv_kernel(prev_ref, cur_ref, next_ref, w_ref, b_ref, o_ref, buf_ref, *, tl, x_tiles, x_tpb, c_tpb):
    i = pl.program_id(0)
    j = jnp.where(i < x_tiles, i % x_tpb, (i - x_tiles) % c_tpb)
    n = jnp.where(i < x_tiles, x_tpb, c_tpb)
    buf_ref[0:HALO, :] = jnp.where(j == 0, 0.0, prev_ref[...].astype(F32))
    buf_ref[HALO:HALO + tl, :] = cur_ref[...].astype(F32)
    buf_ref[HALO + tl:, :] = jnp.where(j == n - 1, 0.0, next_ref[...].astype(F32))
    pad = CONV_W // 2
    acc = b_ref[...] + w_ref[0:1, :] * buf_ref[HALO - pad:HALO - pad + tl, :]
    for t in range(1, CONV_W):
        acc = acc + w_ref[t:t + 1, :] * buf_ref[HALO - pad + t:HALO - pad + t + tl, :]
    o_ref[...] = _silu(acc).astype(o_ref.dtype)


def _conv_silu(p1, col0, ncols, conv_w, conv_b, seq, ctx_len, n_x_rows):
    rows = p1.shape[0]
    tl = _tile(min(seq, ctx_len), 256)
    tc = _tile(ncols, 2048)
    assert col0 % tc == 0
    cb0 = col0 // tc
    hb = tl // HALO
    last_hb = rows // HALO - 1
    return pl.pallas_call(
        functools.partial(_conv_kernel, tl=tl, x_tiles=n_x_rows // tl, x_tpb=seq // tl, c_tpb=ctx_len // tl),
        grid=(rows // tl, ncols // tc),
        in_specs=[pl.BlockSpec((HALO, tc), lambda i, j: (jnp.maximum(i * hb - 1, 0), cb0 + j)),
                  pl.BlockSpec((tl, tc), lambda i, j: (i, cb0 + j)),
                  pl.BlockSpec((HALO, tc), lambda i, j: (jnp.minimum((i + 1) * hb, last_hb), cb0 + j)),
                  pl.BlockSpec((CONV_W, tc), lambda i, j: (0, j)),
                  pl.BlockSpec((1, tc), lambda i, j: (0, j))],
        out_specs=pl.BlockSpec((tl, tc), lambda i, j: (i, j)),
        out_shape=jax.ShapeDtypeStruct((rows, ncols), BF16),
        scratch_shapes=[pltpu.VMEM((tl + 2 * HALO, tc), F32)],
        compiler_params=_cparams(("arbitrary", "arbitrary")),
        name="conv_silu",
    )(p1, p1, p1, conv_w, conv_b.reshape(1, ncols))


def _dt_prep_kernel(raw_ref, bias_ref, alog_ref, acum_ref, dt_ref, *, n_heads):
    v = raw_ref[...] + bias_ref[...]
    dt = jnp.maximum(v, 0.0) + jnp.log(1.0 + jnp.exp(-jnp.abs(v)))
    dta = dt * (-jnp.exp(alog_ref[...]))
    r = lax.broadcasted_iota(jnp.int32, (CHUNK, CHUNK), 0)
    c = lax.broadcasted_iota(jnp.int32, (CHUNK, CHUNK), 1)
    tril = jnp.where(r >= c, 1.0, 0.0).astype(BF16)
    triu = jnp.where(r <= c, 1.0, 0.0).astype(BF16)
    lane = lax.broadcasted_iota(jnp.int32, dta.shape, 1)
    acum = jnp.where(lane < n_heads, _dot_exact_lhs(tril, dta), _dot_exact_lhs(triu, dta))
    acum_ref[0] = acum.T
    dt_ref[0] = dt.T


def _dt_prep(dt_raw, bias2, alog2, n_heads):
    rows, w = dt_raw.shape
    nch = rows // CHUNK
    out = jax.ShapeDtypeStruct((nch, w, CHUNK), F32)
    return pl.pallas_call(
        functools.partial(_dt_prep_kernel, n_heads=n_heads),
        grid=(nch,),
        in_specs=[pl.BlockSpec((CHUNK, w), lambda i: (i, 0)),
                  pl.BlockSpec((1, w), lambda i: (0, 0)),
                  pl.BlockSpec((1, w), lambda i: (0, 0))],
        out_specs=[pl.BlockSpec((1, w, CHUNK), lambda i: (i, 0, 0))] * 2,
        out_shape=[out, out],
        compiler_params=_cparams(("arbitrary",)),
        name="dt_prep",
    )(dt_raw, bias2, alog2)


def _ssd_kernel(xf_ref, bf_ref, cf_ref, af_ref, df_ref, xb_ref, bb_ref, cb_ref, ab_ref, db_ref, h0_ref,
                yf_ref, yb_ref, hfin_ref, h_scr, xw_scr, *, hpg):
    k = pl.program_id(2)
    nck = pl.num_programs(2)
    p = SSD_HEAD_DIM

    @pl.when(k == 0)
    def _():
        h_scr[...] = h0_ref[0, 0]

    ri = lax.broadcasted_iota(jnp.int32, (CHUNK, CHUNK), 0)
    ci = lax.broadcasted_iota(jnp.int32, (CHUNK, CHUNK), 1)
    eye = jnp.where(ri == ci, 1.0, 0.0).astype(BF16)
    nt = (((1,), (1,)), ((), ()))

    for d, (x_ref, b_ref, c_ref, a_ref, dt_ref, y_ref) in enumerate(
            ((xf_ref, bf_ref, cf_ref, af_ref, df_ref, yf_ref), (xb_ref, bb_ref, cb_ref, ab_ref, db_ref, yb_ref))):
        row_a = a_ref[0]
        row_dt = dt_ref[0]
        col_a = _dot_exact_lhs(eye, row_a, nt)
        col_dt = _dot_exact_lhs(eye, row_dt, nt)
        if d == 0:
            mask = ri >= ci
            tot = col_a[CHUNK - 1:CHUNK, :]
        else:
            mask = ri <= ci
            tot = col_a[0:1, :]
        exp_a = jnp.exp(col_a)
        to_end = jnp.exp(tot - col_a) * col_dt
        gain = jnp.exp(tot)
        bc = b_ref[...]
        cc = c_ref[...]
        cb = lax.dot_general(cc, bc, nt, preferred_element_type=F32)
        y_state = jnp.dot(cc, h_scr[d].astype(BF16), preferred_element_type=F32)
        for r in range(hpg):
            sl = slice(r * p, (r + 1) * p)
            xr = x_ref[:, sl]
            seg = col_a[:, r:r + 1] - row_a[r:r + 1, :]
            w = cb * jnp.exp(jnp.where(mask, seg, -jnp.inf)) * row_dt[r:r + 1, :]
            y = jnp.dot(w.astype(BF16), xr, preferred_element_type=F32)
            y = y + y_state[:, sl] * exp_a[:, r:r + 1]
            y_ref[:, sl] = y.astype(y_ref.dtype)
            xw_scr[:, sl] = (xr.astype(F32) * to_end[:, r:r + 1]).astype(BF16)
        bct = bc.astype(F32).T.astype(BF16)
        upd = jnp.dot(bct, xw_scr[...], preferred_element_type=F32)
        for r in range(hpg):
            sl = slice(r * p, (r + 1) * p)
            h_scr[d, :, sl] = h_scr[d, :, sl] * gain[:, r:r + 1] + upd[:, sl]

    @pl.when(k == nck - 1)
    def _():
        hfin_ref[0, 0] = h_scr[...]


def _ssd(xbc, acum_t, dt_t, h0, batch, length, chunk0, hpg, d_inner):
    nck = length // CHUNK
    gw = hpg * SSD_HEAD_DIM
    n_heads = SSD_GROUPS * hpg
    b_blk0 = d_inner // SSD_STATE
    c_blk0 = b_blk0 + SSD_GROUPS
    fwd = lambda b, g, k: chunk0 + b * nck + k
    bwd = lambda b, g, k: chunk0 + b * nck + (nck - 1 - k)

    def dir_specs(ch, d):
        return [pl.BlockSpec((CHUNK, gw), lambda b, g, k: (ch(b, g, k), g)),
                pl.BlockSpec((CHUNK, SSD_STATE), lambda b, g, k: (ch(b, g, k), b_blk0 + g)),
                pl.BlockSpec((CHUNK, SSD_STATE), lambda b, g, k: (ch(b, g, k), c_blk0 + g)),
                pl.BlockSpec((1, hpg, CHUNK), lambda b, g, k: (ch(b, g, k), d * SSD_GROUPS + g, 0)),
                pl.BlockSpec((1, hpg, CHUNK), lambda b, g, k: (ch(b, g, k), d * SSD_GROUPS + g, 0))]

    st_spec = pl.BlockSpec((1, 1, 2, SSD_STATE, gw), lambda b, g, k: (b, g, 0, 0, 0))
    y_shape = jax.ShapeDtypeStruct((batch * length, d_inner), BF16)
    return pl.pallas_call(
        functools.partial(_ssd_kernel, hpg=hpg),
        grid=(batch, SSD_GROUPS, nck),
        in_specs=dir_specs(fwd, 0) + dir_specs(bwd, 1) + [st_spec],
        out_specs=[pl.BlockSpec((CHUNK, gw), lambda b, g, k: (b * nck + k, g)),
                   pl.BlockSpec((CHUNK, gw), lambda b, g, k: (b * nck + (nck - 1 - k), g)),
                   st_spec],
        out_shape=[y_shape, y_shape, jax.ShapeDtypeStruct(h0.shape, F32)],
        scratch_shapes=[pltpu.VMEM((2, SSD_STATE, gw), F32), pltpu.VMEM((CHUNK, gw), BF16)],
        compiler_params=_cparams(("arbitrary", "arbitrary", "arbitrary")),
        name="ssd",
    )(xbc, xbc, xbc, acum_t, dt_t, xbc, xbc, xbc, acum_t, dt_t, h0)


def _ssd_norm_kernel(yf_ref, yb_ref, xs_ref, z_ref, ds_ref, w_ref, o_ref):
    y = yf_ref[...].astype(F32) + yb_ref[...].astype(F32) + ds_ref[...] * xs_ref[...].astype(F32)
    y = y * _silu(z_ref[...].astype(F32))
    ms = jnp.mean(y * y, axis=-1, keepdims=True)
    o_ref[...] = (y * lax.rsqrt(ms + EPS) * w_ref[...]).astype(o_ref.dtype)


def _ssd_norm(yf, yb, xbc, p3, z_blk, dskip_row, w):
    m, d = yf.shape
    tm = _tile(m, 256)
    row = lambda i: (i, 0)
    return pl.pallas_call(
        _ssd_norm_kernel,
        grid=(m // tm,),
        in_specs=[pl.BlockSpec((tm, d), row), pl.BlockSpec((tm, d), row), pl.BlockSpec((tm, d), row),
                  pl.BlockSpec((tm, d), lambda i: (i, z_blk)),
                  pl.BlockSpec((1, d), lambda i: (0, 0)), pl.BlockSpec((1, d), lambda i: (0, 0))],
        out_specs=pl.BlockSpec((tm, d), row),
        out_shape=jax.ShapeDtypeStruct((m, d), BF16),
        compiler_params=_cparams(("arbitrary",)),
        name="ssd_norm",
    )(yf, yb, xbc, p3, dskip_row, w.reshape(1, d))


def _merge_kernel(a1_ref, a2_ref, w1_ref, w2_ref, g1_ref, g2_ref, o_ref, w1b_ref, w2b_ref):
    @pl.when(pl.program_id(1) == 0)
    def _():
        w1b_ref[...] = w1_ref[...].astype(BF16)
        w2b_ref[...] = w2_ref[...].astype(BF16)

    t1 = jnp.dot(a1_ref[...], w1b_ref[...], preferred_element_type=F32)
    t2 = jnp.dot(a2_ref[...], w2b_ref[...], preferred_element_type=F32)
    g1 = jax.nn.sigmoid(g1_ref[...].astype(F32))
    g2 = jax.nn.sigmoid(g2_ref[...].astype(F32))
    o_ref[...] = (g1 * t1 + g2 * t2).astype(o_ref.dtype)


def _merge(attn, ynorm, w1, w2, p3, g1_col0, g2_col0):
    m, k = attn.shape
    n = w1.shape[1]
    tm, tn = _tile(m, 512), _tile(n, 256)
    return pl.pallas_call(
        _merge_kernel,
        grid=(n // tn, m // tm),
        in_specs=[pl.BlockSpec((tm, k), lambda j, i: (i, 0)),
                  pl.BlockSpec((tm, k), lambda j, i: (i, 0)),
                  pl.BlockSpec((k, tn), lambda j, i: (0, j)),
                  pl.BlockSpec((k, tn), lambda j, i: (0, j)),
                  pl.BlockSpec((tm, tn), lambda j, i: (i, g1_col0 // tn + j)),
                  pl.BlockSpec((tm, tn), lambda j, i: (i, g2_col0 // tn + j))],
        out_specs=pl.BlockSpec((tm, tn), lambda j, i: (i, j)),
        out_shape=jax.ShapeDtypeStruct((m, n), BF16),
        scratch_shapes=[pltpu.VMEM((k, tn), BF16), pltpu.VMEM((k, tn), BF16)],
        compiler_params=_cparams(("arbitrary", "arbitrary")),
        name="merge",
    )(attn, ynorm, w1, w2, p3, p3)


def _out_proj_kernel(a_ref, w_ref, x_ref, m_ref, o_ref, wb_ref, *, gate_row):
    @pl.when(pl.program_id(1) == 0)
    def _():
        wb_ref[...] = w_ref[...].astype(BF16)

    t = jnp.dot(a_ref[...], wb_ref[...], preferred_element_type=F32)
    o_ref[...] = x_ref[...] + m_ref[0, gate_row:gate_row + 1, :] * t


def _out_proj(merged, w, x2, mod3, seq, gate_row):
    m, k = merged.shape
    n = w.shape[1]
    tm, tn = _tile(seq, 512), _tile(n, 512)
    tpb = seq // tm
    return pl.pallas_call(
        functools.partial(_out_proj_kernel, gate_row=gate_row),
        grid=(n // tn, m // tm),
        in_specs=[pl.BlockSpec((tm, k), lambda j, i: (i, 0)),
                  pl.BlockSpec((k, tn), lambda j, i: (0, j)),
                  pl.BlockSpec((tm, tn), lambda j, i: (i, j)),
                  pl.BlockSpec((1, N_MOD, tn), lambda j, i: (i // tpb, 0, j))],
        out_specs=pl.BlockSpec((tm, tn), lambda j, i: (i, j)),
        out_shape=jax.ShapeDtypeStruct((m, n), F32),
        scratch_shapes=[pltpu.VMEM((k, tn), BF16)],
        compiler_params=_cparams(("arbitrary", "arbitrary")),
        name="out_proj",
    )(merged, w, x2, mod3)


def _ffn_norm_route_kernel(x_ref, w_ref, m_ref, wr_ref, br_ref, h_ref, route_ref):
    h = _norm_mod(x_ref[...], w_ref[...], m_ref[0, 3:4, :], m_ref[0, 4:5, :])
    h_ref[...] = h
    h_hi = h.astype(BF16)
    h_lo = (h - h_hi.astype(F32)).astype(BF16)
    wr = wr_ref[...]
    w_hi = wr.astype(BF16)
    w_lo = (wr - w_hi.astype(F32)).astype(BF16)
    logits = (jnp.dot(h_hi, w_hi, preferred_element_type=F32) + jnp.dot(h_hi, w_lo, preferred_element_type=F32)
              + jnp.dot(h_lo, w_hi, preferred_element_type=F32)) + br_ref[...]
    lane = lax.broadcasted_iota(jnp.int32, logits.shape, 1).astype(F32)
    big = float(LANES)
    neg = -jnp.inf
    gl = jnp.where(lane < N_GROUPS, logits, neg)
    gmax = jnp.max(gl, axis=-1, keepdims=True)
    gidx = jnp.min(jnp.where(gl == gmax, lane, big), axis=-1, keepdims=True)
    g_prob = 1.0 / jnp.sum(jnp.exp(gl - gmax), axis=-1, keepdims=True)
    lo = N_GROUPS + gidx * EXPERTS_PER_GROUP
    el = jnp.where((lane >= lo) & (lane < lo + EXPERTS_PER_GROUP), logits, neg)
    m1 = jnp.max(el, axis=-1, keepdims=True)
    i1 = jnp.min(jnp.where(el == m1, lane, big), axis=-1, keepdims=True)
    el2 = jnp.where(lane == i1, neg, el)
    m2 = jnp.max(el2, axis=-1, keepdims=True)
    i2 = jnp.min(jnp.where(el2 == m2, lane, big), axis=-1, keepdims=True)
    z = jnp.sum(jnp.exp(el - m1), axis=-1, keepdims=True)
    p1 = 1.0 / z
    p2 = jnp.exp(m2 - m1) / z
    w1 = g_prob * p1 / (p1 + p2)
    w2 = g_prob * p2 / (p1 + p2)
    route = jnp.where(lane == 0, i1 - N_GROUPS, jnp.where(lane == 1, i2 - N_GROUPS,
                      jnp.where(lane == 2, w1, jnp.where(lane == 3, w2, 0.0))))
    route_ref[...] = route


def _ffn_norm_route(x1, w, mod3, wr, br, seq):
    m, d = x1.shape
    tm = _tile(seq, 256)
    tpb = seq // tm
    return pl.pallas_call(
        _ffn_norm_route_kernel,
        grid=(m // tm,),
        in_specs=[pl.BlockSpec((tm, d), lambda i: (i, 0)),
                  pl.BlockSpec((1, d), lambda i: (0, 0)),
                  pl.BlockSpec((1, N_MOD, d), lambda i: (i // tpb, 0, 0)),
                  pl.BlockSpec((d, LANES), lambda i: (0, 0)),
                  pl.BlockSpec((1, LANES), lambda i: (0, 0))],
        out_specs=[pl.BlockSpec((tm, d), lambda i: (i, 0)), pl.BlockSpec((tm, LANES), lambda i: (i, 0))],
        out_shape=[jax.ShapeDtypeStruct((m, d), F32), jax.ShapeDtypeStruct((m, LANES), F32)],
        compiler_params=_cparams(("arbitrary",)),
        name="ffn_norm_route",
    )(x1, w.reshape(1, d), mod3, wr, br)


def _gather_kernel(tok_ref, src_ref, o_ref, sem, *, blk):
    base = pl.program_id(0) * blk

    def issue(r, c):
        pltpu.make_async_copy(src_ref.at[pl.ds(tok_ref[base + r], 1)], o_ref.at[pl.ds(r, 1)], sem).start()
        return c

    lax.fori_loop(0, blk, issue, 0)

    def drain(r, c):
        pltpu.make_async_copy(src_ref.at[pl.ds(0, 1)], o_ref.at[pl.ds(0, 1)], sem).wait()
        return c

    lax.fori_loop(0, blk, drain, 0)


def _gather_rows(row_tok, src, blk):
    n_rows = row_tok.shape[0]
    d = src.shape[1]
    return pl.pallas_call(
        functools.partial(_gather_kernel, blk=blk),
        grid_spec=pltpu.PrefetchScalarGridSpec(
            num_scalar_prefetch=1,
            grid=(n_rows // blk,),
            in_specs=[pl.BlockSpec(memory_space=pl.ANY)],
            out_specs=pl.BlockSpec((blk, d), lambda i, tok: (i, 0)),
            scratch_shapes=[pltpu.SemaphoreType.DMA(())]),
        out_shape=jax.ShapeDtypeStruct((n_rows, d), src.dtype),
        compiler_params=_cparams(("arbitrary",)),
        name="moe_gather",
    )(row_tok, src)


def _moe_up_kernel(ie_ref, it_ref, ib_ref, if_ref, iv_ref, x_ref, wg_ref, wu_ref, o_ref, wgb_ref, wub_ref):
    i = pl.program_id(0)

    @pl.when(if_ref[i] == 1)
    def _():
        wgb_ref[...] = wg_ref[0].astype(BF16)
        wub_ref[...] = wu_ref[0].astype(BF16)

    @pl.when(iv_ref[i] == 1)
    def _():
        xb = x_ref[...].astype(BF16)
        a = jnp.dot(xb, wgb_ref[...], preferred_element_type=F32)
        u = jnp.dot(xb, wub_ref[...], preferred_element_type=F32)
        o_ref[...] = (_silu(a) * u).astype(o_ref.dtype)

    @pl.when(iv_ref[i] == 0)
    def _():
        o_ref[...] = jnp.zeros(o_ref.shape, o_ref.dtype)


def _moe_up(items, xg, wg, wu, blk, tn):
    n_items = items[0].shape[0]
    n_rows, d = xg.shape
    de = wg.shape[2]
    return pl.pallas_call(
        _moe_up_kernel,
        grid_spec=pltpu.PrefetchScalarGridSpec(
            num_scalar_prefetch=5,
            grid=(n_items,),
            in_specs=[pl.BlockSpec((blk, d), lambda i, ie, it, ib, fi, iv: (ib[i], 0)),
                      pl.BlockSpec((1, d, tn), lambda i, ie, it, ib, fi, iv: (ie[i], 0, it[i])),
                      pl.BlockSpec((1, d, tn), lambda i, ie, it, ib, fi, iv: (ie[i], 0, it[i]))],
            out_specs=pl.BlockSpec((blk, tn), lambda i, ie, it, ib, fi, iv: (ib[i], it[i])),
            scratch_shapes=[pltpu.VMEM((d, tn), BF16), pltpu.VMEM((d, tn), BF16)]),
        out_shape=jax.ShapeDtypeStruct((n_rows, de), BF16),
        compiler_params=_cparams(("arbitrary",)),
        name="moe_up",
    )(*items, xg, wg, wu)


def _moe_down_kernel(ie_ref, it_ref, ib_ref, if_ref, iv_ref, h_ref, w_ref, o_ref, wb_ref):
    i = pl.program_id(0)

    @pl.when(if_ref[i] == 1)
    def _():
        wb_ref[...] = w_ref[0].astype(BF16)

    @pl.when(iv_ref[i] == 1)
    def _():
        o_ref[...] = jnp.dot(h_ref[...], wb_ref[...], preferred_element_type=F32)

    @pl.when(iv_ref[i] == 0)
    def _():
        o_ref[...] = jnp.zeros(o_ref.shape, o_ref.dtype)


def _moe_down(items, hid, wd, blk, tn):
    n_items = items[0].shape[0]
    n_rows, de = hid.shape
    d = wd.shape[2]
    return pl.pallas_call(
        _moe_down_kernel,
        grid_spec=pltpu.PrefetchScalarGridSpec(
            num_scalar_prefetch=5,
            grid=(n_items,),
            in_specs=[pl.BlockSpec((blk, de), lambda i, ie, it, ib, fi, iv: (ib[i], 0)),
                      pl.BlockSpec((1, de, tn), lambda i, ie, it, ib, fi, iv: (ie[i], 0, it[i]))],
            out_specs=pl.BlockSpec((blk, tn), lambda i, ie, it, ib, fi, iv: (ib[i], it[i])),
            scratch_shapes=[pltpu.VMEM((de, tn), BF16)]),
        out_shape=jax.ShapeDtypeStruct((n_rows, d), F32),
        compiler_params=_cparams(("arbitrary",)),
        name="moe_down",
    )(*items, hid, wd)


def _combine_kernel(pos_ref, y_ref, x_ref, rt_ref, m_ref, o_ref, buf_ref, sem, *, tm):
    base = pl.program_id(0) * tm

    def issue(r, c):
        for kk in range(TOP_K):
            pltpu.make_async_copy(y_ref.at[pl.ds(pos_ref[TOP_K * (base + r) + kk], 1)],
                                  buf_ref.at[kk, pl.ds(r, 1)], sem).start()
        return c

    lax.fori_loop(0, tm, issue, 0)

    def drain(r, c):
        pltpu.make_async_copy(y_ref.at[pl.ds(0, 1)], buf_ref.at[0, pl.ds(0, 1)], sem).wait()
        return c

    lax.fori_loop(0, TOP_K * tm, drain, 0)
    rt = rt_ref[...]
    moe = rt[:, 2:3] * buf_ref[0] + rt[:, 3:4] * buf_ref[1]
    o_ref[...] = x_ref[...] + m_ref[0, 5:6, :] * moe


def _combine(pos, yexp, x1, route, mod3, seq):
    m, d = x1.shape
    tm = _tile(seq, 128)
    tpb = seq // tm
    return pl.pallas_call(
        functools.partial(_combine_kernel, tm=tm),
        grid_spec=pltpu.PrefetchScalarGridSpec(
            num_scalar_prefetch=1,
            grid=(m // tm,),
            in_specs=[pl.BlockSpec(memory_space=pl.ANY),
                      pl.BlockSpec((tm, d), lambda i, pos: (i, 0)),
                      pl.BlockSpec((tm, LANES), lambda i, pos: (i, 0)),
                      pl.BlockSpec((1, N_MOD, d), lambda i, pos: (i // tpb, 0, 0))],
            out_specs=pl.BlockSpec((tm, d), lambda i, pos: (i, 0)),
            scratch_shapes=[pltpu.VMEM((TOP_K, tm, d), F32), pltpu.SemaphoreType.DMA(())]),
        out_shape=jax.ShapeDtypeStruct((m, d), F32),
        compiler_params=_cparams(("arbitrary",)),
        name="moe_combine",
    )(pos, yexp, x1, route, mod3)


def _moe_plan(route, blk, n_tiles_up, n_tiles_down):
    t = route.shape[0]
    n_assign = t * TOP_K
    nb_max = n_assign // blk + N_EXPERTS
    flat_e = route[:, 0:TOP_K].astype(jnp.int32).reshape(-1)
    onehot = (flat_e[:, None] == jnp.arange(N_EXPERTS, dtype=jnp.int32)[None, :]).astype(jnp.int32)
    cum = jnp.cumsum(onehot, axis=0)
    rank = jnp.take_along_axis(cum, flat_e[:, None], axis=1)[:, 0] - 1
    counts = cum[-1]
    nblk = (counts + blk - 1) // blk
    pend = jnp.cumsum(nblk)
    pstart = pend - nblk
    dest = pstart[flat_e] * blk + rank
    row_tok = jnp.zeros((nb_max * blk,), jnp.int32).at[dest].set(jnp.arange(n_assign, dtype=jnp.int32) // TOP_K)
    total = pend[-1]
    unused = jnp.maximum(nb_max - total, 1)

    def items(n_tiles):
        j = jnp.arange(n_tiles * nb_max, dtype=jnp.int32)
        valid = j < n_tiles * total
        jj = jnp.minimum(j, n_tiles * total - 1)
        e = jnp.minimum(jnp.searchsorted(n_tiles * pend, jj, side='right').astype(jnp.int32), N_EXPERTS - 1)
        local = jj - n_tiles * pstart[e]
        nb_e = jnp.maximum(nblk[e], 1)
        u = j - n_tiles * total
        tile = jnp.where(valid, local // nb_e, u // unused)
        b = jnp.where(valid, pstart[e] + local % nb_e, total + u % unused)
        first = valid & (local % nb_e == 0)
        return (e, tile, b, first.astype(jnp.int32), valid.astype(jnp.int32))

    return row_tok, dest, items(n_tiles_up), items(n_tiles_down)


def kernel(x, c, ctx, c_ctx, w_ada, b_ada, norm_mix_w, norm_ffn_w, w_in, q_norm_w, k_norm_w, conv_w, conv_b,
           a_log_f, a_log_b, dt_bias_f, dt_bias_b, d_skip, ssd_norm_w, w_attn_proj, w_ssd_proj, w_out,
           w_router_group, b_router_group, w_router_expert, b_router_expert, w_exp_gate, w_exp_up, w_exp_down):
    batch, seq, d = x.shape
    ctx_len = ctx.shape[1]
    assert w_ada.shape[0] == 1, "single layer: the context stream is read, never updated"
    d_inner = d
    n_ssd_heads = d_inner // SSD_HEAD_DIM
    hpg = n_ssd_heads // SSD_GROUPS
    kv_dim = N_KV_HEADS * HEAD_DIM
    q_dim = N_Q_HEADS * HEAD_DIM
    bc_dim = SSD_GROUPS * SSD_STATE
    xbc_dim = d_inner + 2 * bc_dim
    p1_cols = 2 * kv_dim + xbc_dim
    dt_cols = 2 * n_ssd_heads
    p3_col0 = p1_cols + dt_cols
    p3_cols = q_dim + 2 * d + d_inner
    assert dt_cols == LANES
    mx, mc = batch * seq, batch * ctx_len

    x2 = x.reshape(mx, d)
    c2 = ctx.reshape(mc, d)
    w_in0 = w_in[0]

    cond8 = jnp.zeros((8, d), F32).at[0:batch].set(c).at[batch].set(c_ctx)
    assert batch == 2
    mod3 = _adaln(cond8, w_ada[0], b_ada[0]).reshape(8, N_MOD, d)

    h_all = _norm_mix(x2, c2, norm_mix_w[0], mod3, seq)
    p1 = _mm(h_all, w_in0, 0, p1_cols, mx + mc, _tile(mx + mc, 512), 512, BF16, "in_proj_kvx")
    dt_raw = _mm(h_all, w_in0, p1_cols, dt_cols, mx + mc, _tile(mx + mc, 512), dt_cols, F32, "in_proj_dt")
    p3 = _mm(h_all, w_in0, p3_col0, p3_cols, mx, _tile(mx, 1024), 512, BF16, "in_proj_qgz")

    tmq = _tile(min(seq, mc), 256)
    cos_t, sin_t = _rope_tables(seq, tmq)
    q = _qk_post(p3, 0, N_Q_HEADS, mx, q_norm_w[0], cos_t, sin_t, seq, mx, HEAD_DIM ** -0.5, "q_post")
    k = _qk_post(p1, 0, N_KV_HEADS, mx + mc, k_norm_w[0], cos_t, sin_t, seq, mx, 1.0, "k_post")
    attn = _attention(q, k, p1, batch, seq, ctx_len)

    xbc = _conv_silu(p1, 2 * kv_dim, xbc_dim, conv_w[0], conv_b[0], seq, ctx_len, mx)
    bias2 = jnp.concatenate([dt_bias_f[0], dt_bias_b[0]]).reshape(1, dt_cols)
    alog2 = jnp.concatenate([a_log_f[0], a_log_b[0]]).reshape(1, dt_cols)
    acum_t, dt_t = _dt_prep(dt_raw, bias2, alog2, n_ssd_heads)
    h_zero = jnp.zeros((batch, SSD_GROUPS, 2, SSD_STATE, hpg * SSD_HEAD_DIM), F32)
    _, _, h_ctx = _ssd(xbc, acum_t, dt_t, h_zero, batch, ctx_len, mx // CHUNK, hpg, d_inner)
    yf, yb, _ = _ssd(xbc, acum_t, dt_t, h_ctx, batch, seq, 0, hpg, d_inner)
    dskip_row = jnp.repeat(d_skip[0], SSD_HEAD_DIM).reshape(1, d_inner)
    ynorm = _ssd_norm(yf, yb, xbc, p3, (q_dim + 2 * d) // d_inner, dskip_row, ssd_norm_w[0])

    merged = _merge(attn, ynorm, w_attn_proj[0], w_ssd_proj[0], p3, q_dim, q_dim + d)
    x1 = _out_proj(merged, w_out[0], x2, mod3, seq, 2)

    wr = jnp.zeros((d, LANES), F32).at[:, :N_GROUPS].set(w_router_group[0])
    wr = wr.at[:, N_GROUPS:N_GROUPS + N_EXPERTS].set(w_router_expert[0])
    br = jnp.zeros((1, LANES), F32).at[0, :N_GROUPS].set(b_router_group[0])
    br = br.at[0, N_GROUPS:N_GROUPS + N_EXPERTS].set(b_router_expert[0])
    h2, route = _ffn_norm_route(x1, norm_ffn_w[0], mod3, wr, br, seq)
    d_exp = w_exp_gate.shape[-1]
    tn_up, tn_down = _tile(d_exp, 256), _tile(d, 2048)
    row_tok, dest, items_up, items_down = _moe_plan(route, MOE_BLK, d_exp // tn_up, d // tn_down)
    xg = _gather_rows(row_tok, h2, MOE_BLK)
    hid = _moe_up(items_up, xg, w_exp_gate[0], w_exp_up[0], MOE_BLK, tn_up)
    yexp = _moe_down(items_down, hid, w_exp_down[0], MOE_BLK, tn_down)
    out = _combine(dest, yexp, x1, route, mod3, seq)
    return out.reshape(batch, seq, d)
```

```python
import functools

import numpy as np
import jax
import jax.numpy as jnp
from jax import lax
from jax.experimental import pallas as pl
from jax.experimental.pallas import tpu as pltpu

F32 = jnp.float32
BF16 = jnp.bfloat16

N_MOD = 6
EPS = 1e-6
GRID_W = 64
N_Q_HEADS = 32
N_KV_HEADS = 8
HEAD_DIM = 128
GQA_GROUP = N_Q_HEADS // N_KV_HEADS
ROPE_THETA = 10000.0
ROPE_AXIS_FREQS = HEAD_DIM // 4
SSD_HEAD_DIM = 64
SSD_GROUPS = 8
SSD_STATE = 128
CONV_W = 5
CHUNK = 128
N_GROUPS = 4
EXPERTS_PER_GROUP = 8
N_EXPERTS = N_GROUPS * EXPERTS_PER_GROUP
TOP_K = 2

LOG2_E = 1.4426950408889634
LANES = 128
HALO = 16
MOE_BLK = 256
ATTN_TQ = 512
ATTN_ROW_SPLIT = 4
VMEM_LIMIT = 56 * 1024 * 1024


def _cparams(sem):
    return pltpu.CompilerParams(dimension_semantics=sem, vmem_limit_bytes=VMEM_LIMIT)


def _tile(n, pref):
    t = min(n, pref)
    while n % t:
        t //= 2
    return t


def _silu(v):
    return v * jax.nn.sigmoid(v)


def _split3(v):
    hi = v.astype(BF16)
    r1 = v - hi.astype(F32)
    mid = r1.astype(BF16)
    lo = (r1 - mid.astype(F32)).astype(BF16)
    return hi, mid, lo


def _dot_exact_lhs(m_bf16, v, dims=(((1,), (0,)), ((), ()))):
    hi, mid, lo = _split3(v)
    out = lax.dot_general(m_bf16, hi, dims, preferred_element_type=F32)
    out = out + lax.dot_general(m_bf16, mid, dims, preferred_element_type=F32)
    return out + lax.dot_general(m_bf16, lo, dims, preferred_element_type=F32)


def _adaln_kernel(c_ref, w_ref, b_ref, o_ref):
    s = _silu(c_ref[...]).astype(BF16)
    o_ref[...] = jnp.dot(s, w_ref[...].astype(BF16), preferred_element_type=F32) + b_ref[...]


def _adaln(cond8, w_ada, b_ada):
    d, n = w_ada.shape
    tn = _tile(n, 512)
    return pl.pallas_call(
        _adaln_kernel,
        grid=(n // tn,),
        in_specs=[pl.BlockSpec((8, d), lambda j: (0, 0)),
                  pl.BlockSpec((d, tn), lambda j: (0, j)),
                  pl.BlockSpec((1, tn), lambda j: (0, j))],
        out_specs=pl.BlockSpec((8, tn), lambda j: (0, j)),
        out_shape=jax.ShapeDtypeStruct((8, n), F32),
        compiler_params=_cparams(("arbitrary",)),
        name="adaln",
    )(cond8, w_ada, b_ada.reshape(1, n))


def _norm_mod(xv, w, shift, scale):
    ms = jnp.mean(xv * xv, axis=-1, keepdims=True)
    h = xv * lax.rsqrt(ms + EPS) * w
    return h * (1.0 + scale) + shift


def _norm_mix_kernel(x_ref, c_ref, w_ref, m_ref, o_ref, *, n_x_tiles):
    i = pl.program_id(0)

    def emit(src):
        o_ref[...] = _norm_mod(src[...], w_ref[...], m_ref[0, 0:1, :], m_ref[0, 1:2, :]).astype(o_ref.dtype)

    @pl.when(i < n_x_tiles)
    def _():
        emit(x_ref)

    @pl.when(i >= n_x_tiles)
    def _():
        emit(c_ref)


def _norm_mix(x2, c2, w, mod3, seq):
    mx, d = x2.shape
    mc = c2.shape[0]
    tm = _tile(min(seq, mc), 256)
    nx, nc = mx // tm, mc // tm
    tpb = seq // tm
    return pl.pallas_call(
        functools.partial(_norm_mix_kernel, n_x_tiles=nx),
        grid=(nx + nc,),
        in_specs=[pl.BlockSpec((tm, d), lambda i: (jnp.minimum(i, nx - 1), 0)),
                  pl.BlockSpec((tm, d), lambda i: (jnp.maximum(i - nx, 0), 0)),
                  pl.BlockSpec((1, d), lambda i: (0, 0)),
                  pl.BlockSpec((1, N_MOD, d), lambda i: (jnp.where(i < nx, i // tpb, 2), 0, 0))],
        out_specs=pl.BlockSpec((tm, d), lambda i: (i, 0)),
        out_shape=jax.ShapeDtypeStruct((mx + mc, d), BF16),
        compiler_params=_cparams(("arbitrary",)),
        name="norm_mix",
    )(x2, c2, w.reshape(1, d), mod3)


def _mm_kernel(a_ref, w_ref, o_ref, wbf_ref):
    @pl.when(pl.program_id(1) == 0)
    def _():
        wbf_ref[...] = w_ref[...].astype(BF16)

    o_ref[...] = jnp.dot(a_ref[...], wbf_ref[...], preferred_element_type=F32).astype(o_ref.dtype)


def _mm(a, w, col0, ncols, m_rows, tm, tn, out_dtype, name):
    k = a.shape[1]
    assert ncols % tn == 0 and m_rows % tm == 0 and col0 % LANES == 0
    return pl.pallas_call(
        _mm_kernel,
        grid=(ncols // tn, m_rows // tm),
        in_specs=[pl.BlockSpec((tm, k), lambda j, i: (i, 0)),
                  pl.BlockSpec((pl.Element(k), pl.Element(tn)), lambda j, i: (0, pl.multiple_of(col0 + j * tn, LANES)))],
        out_specs=pl.BlockSpec((tm, tn), lambda j, i: (i, j)),
        out_shape=jax.ShapeDtypeStruct((m_rows, ncols), out_dtype),
        scratch_shapes=[pltpu.VMEM((k, tn), BF16)],
        compiler_params=_cparams(("arbitrary", "arbitrary")),
        name=name,
    )(a, w)


def _qk_post_kernel(x_ref, w_ref, cos_ref, sin_ref, o_ref, *, n_heads, scale):
    lane = lax.broadcasted_iota(jnp.int32, (x_ref.shape[0], HEAD_DIM), 1)
    first = (lane % (HEAD_DIM // 2)) < (HEAD_DIM // 4)
    for h in range(n_heads):
        sl = slice(h * HEAD_DIM, (h + 1) * HEAD_DIM)
        xh = x_ref[:, sl].astype(F32)
        ms = jnp.mean(xh * xh, axis=-1, keepdims=True)
        y = xh * lax.rsqrt(ms + EPS) * w_ref[...]
        partner = jnp.where(first, pltpu.roll(y, HEAD_DIM - HEAD_DIM // 4, 1), pltpu.roll(y, HEAD_DIM // 4, 1))
        y = y * cos_ref[...] + partner * sin_ref[...]
        o_ref[:, sl] = (y * scale).astype(o_ref.dtype)


def _qk_post(src, col_blk0, n_heads_total, rows, w, cos_t, sin_t, seq, n_x_rows, scale, name):
    tm = _tile(min(seq, rows), 256)
    hpb = 4
    tpb = seq // tm
    nx = n_x_rows // tm
    tab_map = lambda i, j: (jnp.where(i < nx, i % tpb, tpb), 0)
    return pl.pallas_call(
        functools.partial(_qk_post_kernel, n_heads=hpb, scale=scale),
        grid=(rows // tm, n_heads_total // hpb),
        in_specs=[pl.BlockSpec((tm, hpb * HEAD_DIM), lambda i, j: (i, col_blk0 + j)),
                  pl.BlockSpec((1, HEAD_DIM), lambda i, j: (0, 0)),
                  pl.BlockSpec((tm, HEAD_DIM), tab_map),
                  pl.BlockSpec((tm, HEAD_DIM), tab_map)],
        out_specs=pl.BlockSpec((tm, hpb * HEAD_DIM), lambda i, j: (i, j)),
        out_shape=jax.ShapeDtypeStruct((rows, n_heads_total * HEAD_DIM), BF16),
        compiler_params=_cparams(("arbitrary", "arbitrary")),
        name=name,
    )(src, w.reshape(1, HEAD_DIM), cos_t, sin_t)


def _rope_tables(seq, tm):
    rows = seq // GRID_W
    row_pos = jnp.repeat(jnp.arange(rows, dtype=F32), GRID_W)
    col_pos = (jnp.arange(seq) % GRID_W).astype(F32)
    inv_freq = ROPE_THETA ** (-jnp.arange(ROPE_AXIS_FREQS, dtype=F32) / ROPE_AXIS_FREQS)
    ar = row_pos[:, None] * inv_freq
    ac = col_pos[:, None] * inv_freq
    cos_t = jnp.concatenate([jnp.cos(ar), jnp.cos(ar), jnp.cos(ac), jnp.cos(ac)], axis=-1)
    sin_t = jnp.concatenate([-jnp.sin(ar), jnp.sin(ar), -jnp.sin(ac), jnp.sin(ac)], axis=-1)
    cos_t = jnp.concatenate([cos_t, jnp.ones((tm, HEAD_DIM), F32)], axis=0)
    sin_t = jnp.concatenate([sin_t, jnp.zeros((tm, HEAD_DIM), F32)], axis=0)
    return cos_t, sin_t


def _attn_kernel(q_ref, kx_ref, kc_ref, vx_ref, vc_ref, o_ref):
    nt = (((1,), (1,)), ((), ()))
    kx, kc = kx_ref[...], kc_ref[...]

    def with_ones(v):
        lane = lax.broadcasted_iota(jnp.int32, v.shape, 1)
        return jnp.concatenate([v, jnp.where(lane == 0, 1.0, 0.0).astype(v.dtype)], axis=1)

    vx, vc = with_ones(vx_ref[...]), with_ones(vc_ref[...])
    tq = q_ref.shape[0]
    rows = tq // ATTN_ROW_SPLIT
    units = [(g, r) for g in range(GQA_GROUP) for r in range(ATTN_ROW_SPLIT)]

    def scores(u):
        g, r = u
        q = q_ref[r * rows:(r + 1) * rows, g * HEAD_DIM:(g + 1) * HEAD_DIM]
        return (lax.dot_general(q, kx, nt, preferred_element_type=F32),
                lax.dot_general(q, kc, nt, preferred_element_type=F32))

    nxt = scores(units[0])
    for n, (g, r) in enumerate(units):
        s1, s2 = nxt
        if n + 1 < len(units):
            nxt = scores(units[n + 1])
        m = jnp.maximum(jnp.max(s1, axis=-1, keepdims=True), jnp.max(s2, axis=-1, keepdims=True))
        p1 = jnp.exp2(s1 - m).astype(BF16)
        p2 = jnp.exp2(s2 - m).astype(BF16)
        o = jnp.dot(p1, vx, preferred_element_type=F32) + jnp.dot(p2, vc, preferred_element_type=F32)
        o_ref[r * rows:(r + 1) * rows, g * HEAD_DIM:(g + 1) * HEAD_DIM] = (
            o[:, :HEAD_DIM] / o[:, HEAD_DIM:HEAD_DIM + 1]).astype(o_ref.dtype)


def _attention(q, k, p1, batch, seq, ctx_len):
    tq = _tile(seq, ATTN_TQ)
    qpb = seq // tq
    gw = GQA_GROUP * HEAD_DIM
    cblk0 = batch * seq // ctx_len
    return pl.pallas_call(
        _attn_kernel,
        grid=(batch, N_KV_HEADS, qpb),
        in_specs=[pl.BlockSpec((tq, gw), lambda b, h, i: (b * qpb + i, h)),
                  pl.BlockSpec((seq, HEAD_DIM), lambda b, h, i: (b, h)),
                  pl.BlockSpec((ctx_len, HEAD_DIM), lambda b, h, i: (cblk0 + b, h)),
                  pl.BlockSpec((seq, HEAD_DIM), lambda b, h, i: (b, N_KV_HEADS + h)),
                  pl.BlockSpec((ctx_len, HEAD_DIM), lambda b, h, i: (cblk0 + b, N_KV_HEADS + h))],
        out_specs=pl.BlockSpec((tq, gw), lambda b, h, i: (b * qpb + i, h)),
        out_shape=jax.ShapeDtypeStruct(q.shape, BF16),
        compiler_params=_cparams(("arbitrary", "arbitrary", "arbitrary")),
        name="attention",
    )(q, k, k, p1, p1)


def _conv_kernel(prev_ref, cur_ref, next_ref, w_ref, b_ref, o_ref, buf_ref, *, tl, x_tiles, x_tpb, c_tpb):
    i = pl.program_id(0)
    j = jnp.where(i < x_tiles, i % x_tpb, (i - x_tiles) % c_tpb)
    n = jnp.where(i < x_tiles, x_tpb, c_tpb)
    buf_ref[0:HALO, :] = jnp.where(j == 0, 0.0, prev_ref[...].astype(F32))
    buf_ref[HALO:HALO + tl, :] = cur_ref[...].astype(F32)
    buf_ref[HALO + tl:, :] = jnp.where(j == n - 1, 0.0, next_ref[...].astype(F32))
    pad = CONV_W // 2
    acc = b_ref[...] + w_ref[0:1, :] * buf_ref[HALO - pad:HALO - pad + tl, :]
    for t in range(1, CONV_W):
        acc = acc + w_ref[t:t + 1, :] * buf_ref[HALO - pad + t:HALO - pad + t + tl, :]
    o_ref[...] = _silu(acc).astype(o_ref.dtype)


def _conv_silu(p1, col0, ncols, conv_w, conv_b, seq, ctx_len, n_x_rows):
    rows = p1.shape[0]
    tl = _tile(min(seq, ctx_len), 256)
    tc = _tile(ncols, 2048)
    assert col0 % tc == 0
    cb0 = col0 // tc
    hb = tl // HALO
    last_hb = rows // HALO - 1
    return pl.pallas_call(
        functools.partial(_conv_kernel, tl=tl, x_tiles=n_x_rows // tl, x_tpb=seq // tl, c_tpb=ctx_len // tl),
        grid=(rows // tl, ncols // tc),
        in_specs=[pl.BlockSpec((HALO, tc), lambda i, j: (jnp.maximum(i * hb - 1, 0), cb0 + j)),
                  pl.BlockSpec((tl, tc), lambda i, j: (i, cb0 + j)),
                  pl.BlockSpec((HALO, tc), lambda i, j: (jnp.minimum((i + 1) * hb, last_hb), cb0 + j)),
                  pl.BlockSpec((CONV_W, tc), lambda i, j: (0, j)),
                  pl.BlockSpec((1, tc), lambda i, j: (0, j))],
        out_specs=pl.BlockSpec((tl, tc), lambda i, j: (i, j)),
        out_shape=jax.ShapeDtypeStruct((rows, ncols), BF16),
        scratch_shapes=[pltpu.VMEM((tl + 2 * HALO, tc), F32)],
        compiler_params=_cparams(("arbitrary", "arbitrary")),
        name="conv_silu",
    )(p1, p1, p1, conv_w, conv_b.reshape(1, ncols))


def _dt_prep_kernel(raw_ref, bias_ref, alog_ref, acum_ref, dt_ref, *, n_heads):
    v = raw_ref[...] + bias_ref[...]
    dt = jnp.maximum(v, 0.0) + jnp.log(1.0 + jnp.exp(-jnp.abs(v)))
    dta = dt * (-jnp.exp(alog_ref[...]))
    r = lax.broadcasted_iota(jnp.int32, (CHUNK, CHUNK), 0)
    c = lax.broadcasted_iota(jnp.int32, (CHUNK, CHUNK), 1)
    tril = jnp.where(r >= c, 1.0, 0.0).astype(BF16)
    triu = jnp.where(r <= c, 1.0, 0.0).astype(BF16)
    lane = lax.broadcasted_iota(jnp.int32, dta.shape, 1)
    acum = jnp.where(lane < n_heads, _dot_exact_lhs(tril, dta), _dot_exact_lhs(triu, dta))
    acum_ref[0] = acum.T
    dt_ref[0] = dt.T


def _dt_prep(dt_raw, bias2, alog2, n_heads):
    rows, w = dt_raw.shape
    nch = rows // CHUNK
    out = jax.ShapeDtypeStruct((nch, w, CHUNK), F32)
    return pl.pallas_call(
        functools.partial(_dt_prep_kernel, n_heads=n_heads),
        grid=(nch,),
        in_specs=[pl.BlockSpec((CHUNK, w), lambda i: (i, 0)),
                  pl.BlockSpec((1, w), lambda i: (0, 0)),
                  pl.BlockSpec((1, w), lambda i: (0, 0))],
        out_specs=[pl.BlockSpec((1, w, CHUNK), lambda i: (i, 0, 0))] * 2,
        out_shape=[out, out],
        compiler_params=_cparams(("arbitrary",)),
        name="dt_prep",
    )(dt_raw, bias2, alog2)


def _split3_f32(v):
    hi = v.astype(BF16).astype(F32)
    r1 = v - hi
    mid = r1.astype(BF16).astype(F32)
    lo = (r1 - mid).astype(BF16).astype(F32)
    return [hi, mid, lo]


def _ssd_consts(hpg):
    p = SSD_HEAD_DIM
    gw = hpg * p
    k = np.arange(CHUNK)[:, None]

    def expand(base, width, per):
        col_head = (np.arange(width) // per)[None, :]
        kk = k - base
        return ((kk >= 0) & (kk < 3 * hpg) & (kk % hpg == col_head)).astype(np.float32)

    mats = [expand(0, gw, p), expand(3 * hpg, gw, p), expand(9 * hpg, gw, p), expand(6 * hpg, hpg * CHUNK, CHUNK)]
    return jnp.asarray(np.concatenate(mats, axis=1), dtype=BF16)


def _ssd_kernel(xf_ref, bf_ref, cf_ref, af_ref, df_ref, xb_ref, bb_ref, cb_ref, ab_ref, db_ref, h0_ref, k_ref,
                yf_ref, yb_ref, hfin_ref, h_scr, *, hpg):
    k = pl.program_id(2)
    nck = pl.num_programs(2)
    p = SSD_HEAD_DIM
    gw = hpg * p

    @pl.when(k == 0)
    def _():
        h_scr[...] = h0_ref[0, 0]

    ri = lax.broadcasted_iota(jnp.int32, (CHUNK, CHUNK), 0)
    ci = lax.broadcasted_iota(jnp.int32, (CHUNK, CHUNK), 1)
    low_half = (lax.broadcasted_iota(jnp.int32, (CHUNK, gw), 1) % (2 * p)) < p
    nt = (((1,), (1,)), ((), ()))
    e_exp_a, e_to_end, e_gain = k_ref[:, 0:gw], k_ref[:, gw:2 * gw], k_ref[:, 2 * gw:3 * gw]
    e_col_a = k_ref[:, 3 * gw:]
    pad_rows = jnp.zeros((CHUNK - 12 * hpg, CHUNK), F32)

    for d, (x_ref, b_ref, c_ref, a_ref, dt_ref, y_ref) in enumerate(
            ((xf_ref, bf_ref, cf_ref, af_ref, df_ref, yf_ref), (xb_ref, bb_ref, cb_ref, ab_ref, db_ref, yb_ref))):
        row_a = a_ref[0]
        row_dt = dt_ref[0]
        if d == 0:
            mask = ri >= ci
            tot = row_a[:, CHUNK - 1:CHUNK]
        else:
            mask = ri <= ci
            tot = row_a[:, 0:1]
        exp_a = jnp.exp(row_a)
        to_end = jnp.exp(tot - row_a) * row_dt
        gain = jnp.broadcast_to(jnp.exp(tot), row_a.shape)
        table = jnp.concatenate(_split3_f32(exp_a) + _split3_f32(to_end) + _split3_f32(row_a)
                                + _split3_f32(gain) + [pad_rows], axis=0)
        tab_t = table.T.astype(BF16)
        exp_a_full = jnp.dot(tab_t, e_exp_a, preferred_element_type=F32)
        to_end_full = jnp.dot(tab_t, e_to_end, preferred_element_type=F32)
        gain_full = jnp.dot(tab_t[0:16], e_gain, preferred_element_type=F32)[0:1]
        col_a = jnp.dot(tab_t, e_col_a, preferred_element_type=F32)
        bc = b_ref[...]
        cc = c_ref[...]
        cb = lax.dot_general(cc, bc, nt, preferred_element_type=F32)
        x32 = x_ref[...].astype(F32)
        x_lo = jnp.where(low_half, x32, 0.0).astype(BF16)
        x_hi = jnp.where(low_half, 0.0, x32).astype(BF16)
        y_parts = []
        for q in range(hpg // 2):
            ws = []
            for r in (2 * q, 2 * q + 1):
                seg = col_a[:, r * CHUNK:(r + 1) * CHUNK] - row_a[r:r + 1, :]
                ws.append((cb * jnp.exp(jnp.where(mask, seg, -jnp.inf)) * row_dt[r:r + 1, :]).astype(BF16))
            sl = slice(q * 2 * p, (q + 1) * 2 * p)
            y_parts.append(jnp.dot(jnp.concatenate(ws, axis=1), jnp.concatenate([x_lo[:, sl], x_hi[:, sl]], axis=0),
                                   preferred_element_type=F32))
        y_state = jnp.dot(cc, h_scr[d].astype(BF16), preferred_element_type=F32)
        y_ref[...] = (jnp.concatenate(y_parts, axis=1) + y_state * exp_a_full).astype(y_ref.dtype)
        bct = bc.astype(F32).T.astype(BF16)
        upd = jnp.dot(bct, (x32 * to_end_full).astype(BF16), preferred_element_type=F32)
        h_scr[d] = h_scr[d] * gain_full + upd

    @pl.when(k == nck - 1)
    def _():
        hfin_ref[0, 0] = h_scr[...]


def _ssd(xbc, acum_t, dt_t, h0, consts, batch, length, chunk0, hpg, d_inner):
    nck = length // CHUNK
    gw = hpg * SSD_HEAD_DIM
    assert hpg % 2 == 0 and 2 * SSD_HEAD_DIM == LANES and 12 * hpg <= CHUNK
    b_blk0 = d_inner // SSD_STATE
    c_blk0 = b_blk0 + SSD_GROUPS
    fwd = lambda b, g, k: chunk0 + b * nck + k
    bwd = lambda b, g, k: chunk0 + b * nck + (nck - 1 - k)

    def dir_specs(ch, d):
        return [pl.BlockSpec((CHUNK, gw), lambda b, g, k: (ch(b, g, k), g)),
                pl.BlockSpec((CHUNK, SSD_STATE), lambda b, g, k: (ch(b, g, k), b_blk0 + g)),
                pl.BlockSpec((CHUNK, SSD_STATE), lambda b, g, k: (ch(b, g, k), c_blk0 + g)),
                pl.BlockSpec((1, hpg, CHUNK), lambda b, g, k: (ch(b, g, k), d * SSD_GROUPS + g, 0)),
                pl.BlockSpec((1, hpg, CHUNK), lambda b, g, k: (ch(b, g, k), d * SSD_GROUPS + g, 0))]

    st_spec = pl.BlockSpec((1, 1, 2, SSD_STATE, gw), lambda b, g, k: (b, g, 0, 0, 0))
    y_shape = jax.ShapeDtypeStruct((batch * length, d_inner), BF16)
    return pl.pallas_call(
        functools.partial(_ssd_kernel, hpg=hpg),
        grid=(batch, SSD_GROUPS, nck),
        in_specs=dir_specs(fwd, 0) + dir_specs(bwd, 1) + [st_spec, pl.BlockSpec(consts.shape, lambda b, g, k: (0, 0))],
        out_specs=[pl.BlockSpec((CHUNK, gw), lambda b, g, k: (b * nck + k, g)),
                   pl.BlockSpec((CHUNK, gw), lambda b, g, k: (b * nck + (nck - 1 - k), g)),
                   st_spec],
        out_shape=[y_shape, y_shape, jax.ShapeDtypeStruct(h0.shape, F32)],
        scratch_shapes=[pltpu.VMEM((2, SSD_STATE, gw), F32)],
        compiler_params=_cparams(("arbitrary", "arbitrary", "arbitrary")),
        name="ssd",
    )(xbc, xbc, xbc, acum_t, dt_t, xbc, xbc, xbc, acum_t, dt_t, h0, consts)


def _ssd_norm_kernel(yf_ref, yb_ref, xs_ref, z_ref, ds_ref, w_ref, o_ref):
    y = yf_ref[...].astype(F32) + yb_ref[...].astype(F32) + ds_ref[...] * xs_ref[...].astype(F32)
    y = y * _silu(z_ref[...].astype(F32))
    ms = jnp.mean(y * y, axis=-1, keepdims=True)
    o_ref[...] = (y * lax.rsqrt(ms + EPS) * w_ref[...]).astype(o_ref.dtype)


def _ssd_norm(yf, yb, xbc, p3, z_blk, dskip_row, w):
    m, d = yf.shape
    tm = _tile(m, 256)
    row = lambda i: (i, 0)
    return pl.pallas_call(
        _ssd_norm_kernel,
        grid=(m // tm,),
        in_specs=[pl.BlockSpec((tm, d), row), pl.BlockSpec((tm, d), row), pl.BlockSpec((tm, d), row),
                  pl.BlockSpec((tm, d), lambda i: (i, z_blk)),
                  pl.BlockSpec((1, d), lambda i: (0, 0)), pl.BlockSpec((1, d), lambda i: (0, 0))],
        out_specs=pl.BlockSpec((tm, d), row),
        out_shape=jax.ShapeDtypeStruct((m, d), BF16),
        compiler_params=_cparams(("arbitrary",)),
        name="ssd_norm",
    )(yf, yb, xbc, p3, dskip_row, w.reshape(1, d))


def _merge_kernel(a1_ref, a2_ref, w1_ref, w2_ref, g1_ref, g2_ref, o_ref, w1b_ref, w2b_ref):
    @pl.when(pl.program_id(1) == 0)
    def _():
        w1b_ref[...] = w1_ref[...].astype(BF16)
        w2b_ref[...] = w2_ref[...].astype(BF16)

    t1 = jnp.dot(a1_ref[...], w1b_ref[...], preferred_element_type=F32)
    t2 = jnp.dot(a2_ref[...], w2b_ref[...], preferred_element_type=F32)
    g1 = jax.nn.sigmoid(g1_ref[...].astype(F32))
    g2 = jax.nn.sigmoid(g2_ref[...].astype(F32))
    o_ref[...] = (g1 * t1 + g2 * t2).astype(o_ref.dtype)


def _merge(attn, ynorm, w1, w2, p3, g1_col0, g2_col0):
    m, k = attn.shape
    n = w1.shape[1]
    tm, tn = _tile(m, 512), _tile(n, 256)
    return pl.pallas_call(
        _merge_kernel,
        grid=(n // tn, m // tm),
        in_specs=[pl.BlockSpec((tm, k), lambda j, i: (i, 0)),
                  pl.BlockSpec((tm, k), lambda j, i: (i, 0)),
                  pl.BlockSpec((k, tn), lambda j, i: (0, j)),
                  pl.BlockSpec((k, tn), lambda j, i: (0, j)),
                  pl.BlockSpec((tm, tn), lambda j, i: (i, g1_col0 // tn + j)),
                  pl.BlockSpec((tm, tn), lambda j, i: (i, g2_col0 // tn + j))],
        out_specs=pl.BlockSpec((tm, tn), lambda j, i: (i, j)),
        out_shape=jax.ShapeDtypeStruct((m, n), BF16),
        scratch_shapes=[pltpu.VMEM((k, tn), BF16), pltpu.VMEM((k, tn), BF16)],
        compiler_params=_cparams(("arbitrary", "arbitrary")),
        name="merge",
    )(attn, ynorm, w1, w2, p3, p3)


def _out_proj_kernel(a_ref, w_ref, x_ref, m_ref, o_ref, wb_ref, *, gate_row):
    @pl.when(pl.program_id(1) == 0)
    def _():
        wb_ref[...] = w_ref[...].astype(BF16)

    t = jnp.dot(a_ref[...], wb_ref[...], preferred_element_type=F32)
    o_ref[...] = x_ref[...] + m_ref[0, gate_row:gate_row + 1, :] * t


def _out_proj(merged, w, x2, mod3, seq, gate_row):
    m, k = merged.shape
    n = w.shape[1]
    tm, tn = _tile(seq, 512), _tile(n, 512)
    tpb = seq // tm
    return pl.pallas_call(
        functools.partial(_out_proj_kernel, gate_row=gate_row),
        grid=(n // tn, m // tm),
        in_specs=[pl.BlockSpec((tm, k), lambda j, i: (i, 0)),
                  pl.BlockSpec((k, tn), lambda j, i: (0, j)),
                  pl.BlockSpec((tm, tn), lambda j, i: (i, j)),
                  pl.BlockSpec((1, N_MOD, tn), lambda j, i: (i // tpb, 0, j))],
        out_specs=pl.BlockSpec((tm, tn), lambda j, i: (i, j)),
        out_shape=jax.ShapeDtypeStruct((m, n), F32),
        scratch_shapes=[pltpu.VMEM((k, tn), BF16)],
        compiler_params=_cparams(("arbitrary", "arbitrary")),
        name="out_proj",
    )(merged, w, x2, mod3)


def _ffn_norm_route_kernel(x_ref, w_ref, m_ref, wr_ref, br_ref, h_ref, route_ref):
    h = _norm_mod(x_ref[...], w_ref[...], m_ref[0, 3:4, :], m_ref[0, 4:5, :])
    h_ref[...] = h
    h_hi = h.astype(BF16)
    h_lo = (h - h_hi.astype(F32)).astype(BF16)
    wr = wr_ref[...]
    w_hi = wr.astype(BF16)
    w_lo = (wr - w_hi.astype(F32)).astype(BF16)
    logits = (jnp.dot(h_hi, w_hi, preferred_element_type=F32) + jnp.dot(h_hi, w_lo, preferred_element_type=F32)
              + jnp.dot(h_lo, w_hi, preferred_element_type=F32)) + br_ref[...]
    lane = lax.broadcasted_iota(jnp.int32, logits.shape, 1).astype(F32)
    big = float(LANES)
    neg = -jnp.inf
    gl = jnp.where(lane < N_GROUPS, logits, neg)
    gmax = jnp.max(gl, axis=-1, keepdims=True)
    gidx = jnp.min(jnp.where(gl == gmax, lane, big), axis=-1, keepdims=True)
    g_prob = 1.0 / jnp.sum(jnp.exp(gl - gmax), axis=-1, keepdims=True)
    lo = N_GROUPS + gidx * EXPERTS_PER_GROUP
    el = jnp.where((lane >= lo) & (lane < lo + EXPERTS_PER_GROUP), logits, neg)
    m1 = jnp.max(el, axis=-1, keepdims=True)
    i1 = jnp.min(jnp.where(el == m1, lane, big), axis=-1, keepdims=True)
    el2 = jnp.where(lane == i1, neg, el)
    m2 = jnp.max(el2, axis=-1, keepdims=True)
    i2 = jnp.min(jnp.where(el2 == m2, lane, big), axis=-1, keepdims=True)
    z = jnp.sum(jnp.exp(el - m1), axis=-1, keepdims=True)
    p1 = 1.0 / z
    p2 = jnp.exp(m2 - m1) / z
    w1 = g_prob * p1 / (p1 + p2)
    w2 = g_prob * p2 / (p1 + p2)
    route = jnp.where(lane == 0, i1 - N_GROUPS, jnp.where(lane == 1, i2 - N_GROUPS,
                      jnp.where(lane == 2, w1, jnp.where(lane == 3, w2, 0.0))))
    route_ref[...] = route


def _ffn_norm_route(x1, w, mod3, wr, br, seq):
    m, d = x1.shape
    tm = _tile(seq, 256)
    tpb = seq // tm
    return pl.pallas_call(
        _ffn_norm_route_kernel,
        grid=(m // tm,),
        in_specs=[pl.BlockSpec((tm, d), lambda i: (i, 0)),
                  pl.BlockSpec((1, d), lambda i: (0, 0)),
                  pl.BlockSpec((1, N_MOD, d), lambda i: (i // tpb, 0, 0)),
                  pl.BlockSpec((d, LANES), lambda i: (0, 0)),
                  pl.BlockSpec((1, LANES), lambda i: (0, 0))],
        out_specs=[pl.BlockSpec((tm, d), lambda i: (i, 0)), pl.BlockSpec((tm, LANES), lambda i: (i, 0))],
        out_shape=[jax.ShapeDtypeStruct((m, d), F32), jax.ShapeDtypeStruct((m, LANES), F32)],
        compiler_params=_cparams(("arbitrary",)),
        name="ffn_norm_route",
    )(x1, w.reshape(1, d), mod3, wr, br)


def _gather_kernel(tok_ref, nv_ref, src_ref, o_ref, buf_ref, sem, *, blk):
    i = pl.program_id(0)
    base = i * blk
    nv = nv_ref[i]

    def issue(r, c):
        pltpu.make_async_copy(src_ref.at[pl.ds(tok_ref[base + r], 1)], buf_ref.at[pl.ds(r, 1)], sem).start()
        return c

    lax.fori_loop(0, nv, issue, 0)

    def zero(r, c):
        buf_ref[pl.ds(r, 1), :] = jnp.zeros((1, buf_ref.shape[1]), buf_ref.dtype)
        return c

    lax.fori_loop(nv, blk, zero, 0)

    def drain(r, c):
        pltpu.make_async_copy(src_ref.at[pl.ds(0, 1)], buf_ref.at[pl.ds(0, 1)], sem).wait()
        return c

    lax.fori_loop(0, nv, drain, 0)
    o_ref[...] = buf_ref[...].astype(o_ref.dtype)


def _gather_rows(row_tok, n_valid, src, blk):
    n_rows = row_tok.shape[0]
    d = src.shape[1]
    return pl.pallas_call(
        functools.partial(_gather_kernel, blk=blk),
        grid_spec=pltpu.PrefetchScalarGridSpec(
            num_scalar_prefetch=2,
            grid=(n_rows // blk,),
            in_specs=[pl.BlockSpec(memory_space=pl.ANY)],
            out_specs=pl.BlockSpec((blk, d), lambda i, tok, nv: (i, 0)),
            scratch_shapes=[pltpu.VMEM((blk, d), src.dtype), pltpu.SemaphoreType.DMA(())]),
        out_shape=jax.ShapeDtypeStruct((n_rows, d), BF16),
        compiler_params=_cparams(("arbitrary",)),
        name="moe_gather",
    )(row_tok, n_valid, src)


def _moe_up_kernel(ie_ref, it_ref, ib_ref, if_ref, iv_ref, x_ref, wg_ref, wu_ref, o_ref, wgb_ref, wub_ref):
    i = pl.program_id(0)

    @pl.when(if_ref[i] == 1)
    def _():
        wgb_ref[...] = wg_ref[0].astype(BF16)
        wub_ref[...] = wu_ref[0].astype(BF16)

    @pl.when(iv_ref[i] == 1)
    def _():
        xb = x_ref[...]
        a = jnp.dot(xb, wgb_ref[...], preferred_element_type=F32)
        u = jnp.dot(xb, wub_ref[...], preferred_element_type=F32)
        o_ref[...] = (_silu(a) * u).astype(o_ref.dtype)

    @pl.when(iv_ref[i] == 0)
    def _():
        o_ref[...] = jnp.zeros(o_ref.shape, o_ref.dtype)


def _moe_up(items, xg, wg, wu, blk, tn):
    n_items = items[0].shape[0]
    n_rows, d = xg.shape
    de = wg.shape[2]
    return pl.pallas_call(
        _moe_up_kernel,
        grid_spec=pltpu.PrefetchScalarGridSpec(
            num_scalar_prefetch=5,
            grid=(n_items,),
            in_specs=[pl.BlockSpec((blk, d), lambda i, ie, it, ib, fi, iv: (ib[i], 0)),
                      pl.BlockSpec((1, d, tn), lambda i, ie, it, ib, fi, iv: (ie[i], 0, it[i])),
                      pl.BlockSpec((1, d, tn), lambda i, ie, it, ib, fi, iv: (ie[i], 0, it[i]))],
            out_specs=pl.BlockSpec((blk, tn), lambda i, ie, it, ib, fi, iv: (ib[i], it[i])),
            scratch_shapes=[pltpu.VMEM((d, tn), BF16), pltpu.VMEM((d, tn), BF16)]),
        out_shape=jax.ShapeDtypeStruct((n_rows, de), BF16),
        compiler_params=_cparams(("arbitrary",)),
        name="moe_up",
    )(*items, xg, wg, wu)


def _moe_down_kernel(ie_ref, it_ref, ib_ref, if_ref, iv_ref, h_ref, w_ref, o_ref, wb_ref):
    i = pl.program_id(0)

    @pl.when(if_ref[i] == 1)
    def _():
        wb_ref[...] = w_ref[0].astype(BF16)

    @pl.when(iv_ref[i] == 1)
    def _():
        o_ref[...] = jnp.dot(h_ref[...], wb_ref[...], preferred_element_type=F32)

    @pl.when(iv_ref[i] == 0)
    def _():
        o_ref[...] = jnp.zeros(o_ref.shape, o_ref.dtype)


def _moe_down(items, hid, wd, blk, tn):
    n_items = items[0].shape[0]
    n_rows, de = hid.shape
    d = wd.shape[2]
    return pl.pallas_call(
        _moe_down_kernel,
        grid_spec=pltpu.PrefetchScalarGridSpec(
            num_scalar_prefetch=5,
            grid=(n_items,),
            in_specs=[pl.BlockSpec((blk, de), lambda i, ie, it, ib, fi, iv: (ib[i], 0)),
                      pl.BlockSpec((1, de, tn), lambda i, ie, it, ib, fi, iv: (ie[i], 0, it[i]))],
            out_specs=pl.BlockSpec((blk, tn), lambda i, ie, it, ib, fi, iv: (ib[i], it[i])),
            scratch_shapes=[pltpu.VMEM((de, tn), BF16)]),
        out_shape=jax.ShapeDtypeStruct((n_rows, d), F32),
        compiler_params=_cparams(("arbitrary",)),
        name="moe_down",
    )(*items, hid, wd)


def _combine_kernel(pos_ref, y_ref, x_ref, rt_ref, m_ref, o_ref, buf_ref, sem, *, tm):
    base = pl.program_id(0) * tm

    def issue(r, c):
        for kk in range(TOP_K):
            pltpu.make_async_copy(y_ref.at[pl.ds(pos_ref[TOP_K * (base + r) + kk], 1)],
                                  buf_ref.at[kk, pl.ds(r, 1)], sem).start()
        return c

    lax.fori_loop(0, tm, issue, 0)

    def drain(r, c):
        pltpu.make_async_copy(y_ref.at[pl.ds(0, 1)], buf_ref.at[0, pl.ds(0, 1)], sem).wait()
        return c

    lax.fori_loop(0, TOP_K * tm, drain, 0)
    rt = rt_ref[...]
    moe = rt[:, 2:3] * buf_ref[0] + rt[:, 3:4] * buf_ref[1]
    o_ref[...] = x_ref[...] + m_ref[0, 5:6, :] * moe


def _combine(pos, yexp, x1, route, mod3, seq):
    m, d = x1.shape
    tm = _tile(seq, 128)
    tpb = seq // tm
    return pl.pallas_call(
        functools.partial(_combine_kernel, tm=tm),
        grid_spec=pltpu.PrefetchScalarGridSpec(
            num_scalar_prefetch=1,
            grid=(m // tm,),
            in_specs=[pl.BlockSpec(memory_space=pl.ANY),
                      pl.BlockSpec((tm, d), lambda i, pos: (i, 0)),
                      pl.BlockSpec((tm, LANES), lambda i, pos: (i, 0)),
                      pl.BlockSpec((1, N_MOD, d), lambda i, pos: (i // tpb, 0, 0))],
            out_specs=pl.BlockSpec((tm, d), lambda i, pos: (i, 0)),
            scratch_shapes=[pltpu.VMEM((TOP_K, tm, d), F32), pltpu.SemaphoreType.DMA(())]),
        out_shape=jax.ShapeDtypeStruct((m, d), F32),
        compiler_params=_cparams(("arbitrary",)),
        name="moe_combine",
    )(pos, yexp, x1, route, mod3)


def _moe_plan(route, blk, n_tiles_up, n_tiles_down):
    t = route.shape[0]
    n_assign = t * TOP_K
    nb_max = n_assign // blk + N_EXPERTS
    flat_e = route[:, 0:TOP_K].astype(jnp.int32).reshape(-1)
    onehot = (flat_e[:, None] == jnp.arange(N_EXPERTS, dtype=jnp.int32)[None, :]).astype(jnp.int32)
    cum = jnp.cumsum(onehot, axis=0)
    rank = jnp.take_along_axis(cum, flat_e[:, None], axis=1)[:, 0] - 1
    counts = cum[-1]
    nblk = (counts + blk - 1) // blk
    pend = jnp.cumsum(nblk)
    pstart = pend - nblk
    dest = pstart[flat_e] * blk + rank
    row_tok = jnp.zeros((nb_max * blk,), jnp.int32).at[dest].set(jnp.arange(n_assign, dtype=jnp.int32) // TOP_K)
    total = pend[-1]
    unused = jnp.maximum(nb_max - total, 1)

    def items(n_tiles):
        j = jnp.arange(n_tiles * nb_max, dtype=jnp.int32)
        valid = j < n_tiles * total
        jj = jnp.minimum(j, n_tiles * total - 1)
        e = jnp.minimum(jnp.sum((jj[:, None] >= n_tiles * pend[None, :]).astype(jnp.int32), axis=1), N_EXPERTS - 1)
        local = jj - n_tiles * pstart[e]
        nb_e = jnp.maximum(nblk[e], 1)
        u = j - n_tiles * total
        tile = jnp.where(valid, local // nb_e, u // unused)
        b = jnp.where(valid, pstart[e] + local % nb_e, total + u % unused)
        first = valid & (local % nb_e == 0)
        return (e, tile, b, first.astype(jnp.int32), valid.astype(jnp.int32))

    blk_ids = jnp.arange(nb_max, dtype=jnp.int32)
    blk_e = jnp.minimum(jnp.sum((blk_ids[:, None] >= pend[None, :]).astype(jnp.int32), axis=1), N_EXPERTS - 1)
    n_valid = jnp.clip(counts[blk_e] - (blk_ids - pstart[blk_e]) * blk, 0, blk)
    n_valid = jnp.where(blk_ids < total, n_valid, 0).astype(jnp.int32)
    return row_tok, n_valid, dest, items(n_tiles_up), items(n_tiles_down)


def kernel(x, c, ctx, c_ctx, w_ada, b_ada, norm_mix_w, norm_ffn_w, w_in, q_norm_w, k_norm_w, conv_w, conv_b,
           a_log_f, a_log_b, dt_bias_f, dt_bias_b, d_skip, ssd_norm_w, w_attn_proj, w_ssd_proj, w_out,
           w_router_group, b_router_group, w_router_expert, b_router_expert, w_exp_gate, w_exp_up, w_exp_down):
    batch, seq, d = x.shape
    ctx_len = ctx.shape[1]
    assert w_ada.shape[0] == 1, "single layer: the context stream is read, never updated"
    d_inner = d
    n_ssd_heads = d_inner // SSD_HEAD_DIM
    hpg = n_ssd_heads // SSD_GROUPS
    kv_dim = N_KV_HEADS * HEAD_DIM
    q_dim = N_Q_HEADS * HEAD_DIM
    bc_dim = SSD_GROUPS * SSD_STATE
    xbc_dim = d_inner + 2 * bc_dim
    p1_cols = 2 * kv_dim + xbc_dim
    dt_cols = 2 * n_ssd_heads
    p3_col0 = p1_cols + dt_cols
    p3_cols = q_dim + 2 * d + d_inner
    assert dt_cols == LANES
    mx, mc = batch * seq, batch * ctx_len

    x2 = x.reshape(mx, d)
    c2 = ctx.reshape(mc, d)
    w_in0 = w_in[0]

    cond8 = jnp.zeros((8, d), F32).at[0:batch].set(c).at[batch].set(c_ctx)
    assert batch == 2
    mod3 = _adaln(cond8, w_ada[0], b_ada[0]).reshape(8, N_MOD, d)

    h_all = _norm_mix(x2, c2, norm_mix_w[0], mod3, seq)
    p1 = _mm(h_all, w_in0, 0, p1_cols, mx + mc, _tile(mx + mc, 512), 512, BF16, "in_proj_kvx")
    dt_raw = _mm(h_all, w_in0, p1_cols, dt_cols, mx + mc, _tile(mx + mc, 512), dt_cols, F32, "in_proj_dt")
    p3 = _mm(h_all, w_in0, p3_col0, p3_cols, mx, _tile(mx, 1024), 512, BF16, "in_proj_qgz")

    tmq = _tile(min(seq, mc), 256)
    cos_t, sin_t = _rope_tables(seq, tmq)
    q = _qk_post(p3, 0, N_Q_HEADS, mx, q_norm_w[0], cos_t, sin_t, seq, mx,
                 HEAD_DIM ** -0.5 * LOG2_E, "q_post")
    k = _qk_post(p1, 0, N_KV_HEADS, mx + mc, k_norm_w[0], cos_t, sin_t, seq, mx, 1.0, "k_post")
    attn = _attention(q, k, p1, batch, seq, ctx_len)

    xbc = _conv_silu(p1, 2 * kv_dim, xbc_dim, conv_w[0], conv_b[0], seq, ctx_len, mx)
    bias2 = jnp.concatenate([dt_bias_f[0], dt_bias_b[0]]).reshape(1, dt_cols)
    alog2 = jnp.concatenate([a_log_f[0], a_log_b[0]]).reshape(1, dt_cols)
    acum_t, dt_t = _dt_prep(dt_raw, bias2, alog2, n_ssd_heads)
    h_zero = jnp.zeros((batch, SSD_GROUPS, 2, SSD_STATE, hpg * SSD_HEAD_DIM), F32)
    ssd_consts = _ssd_consts(hpg)
    _, _, h_ctx = _ssd(xbc, acum_t, dt_t, h_zero, ssd_consts, batch, ctx_len, mx // CHUNK, hpg, d_inner)
    yf, yb, _ = _ssd(xbc, acum_t, dt_t, h_ctx, ssd_consts, batch, seq, 0, hpg, d_inner)
    dskip_row = jnp.repeat(d_skip[0], SSD_HEAD_DIM).reshape(1, d_inner)
    ynorm = _ssd_norm(yf, yb, xbc, p3, (q_dim + 2 * d) // d_inner, dskip_row, ssd_norm_w[0])

    merged = _merge(attn, ynorm, w_attn_proj[0], w_ssd_proj[0], p3, q_dim, q_dim + d)
    x1 = _out_proj(merged, w_out[0], x2, mod3, seq, 2)

    wr = jnp.zeros((d, LANES), F32).at[:, :N_GROUPS].set(w_router_group[0])
    wr = wr.at[:, N_GROUPS:N_GROUPS + N_EXPERTS].set(w_router_expert[0])
    br = jnp.zeros((1, LANES), F32).at[0, :N_GROUPS].set(b_router_group[0])
    br = br.at[0, N_GROUPS:N_GROUPS + N_EXPERTS].set(b_router_expert[0])
    h2, route = _ffn_norm_route(x1, norm_ffn_w[0], mod3, wr, br, seq)
    d_exp = w_exp_gate.shape[-1]
    tn_up, tn_down = _tile(d_exp, 512), _tile(d, 2048)
    row_tok, n_valid, dest, items_up, items_down = _moe_plan(route, MOE_BLK, d_exp // tn_up, d // tn_down)
    xg = _gather_rows(row_tok, n_valid, h2, MOE_BLK)
    hid = _moe_up(items_up, xg, w_exp_gate[0], w_exp_up[0], MOE_BLK, tn_up)
    yexp = _moe_down(items_down, hid, w_exp_down[0], MOE_BLK, tn_down)
    out = _combine(dest, yexp, x1, route, mod3, seq)
    return out.reshape(batch, seq, d)
```

```python
import functools

import numpy as np
import jax
import jax.numpy as jnp
from jax import lax
from jax.experimental import pallas as pl
from jax.experimental.pallas import tpu as pltpu

F32 = jnp.float32
BF16 = jnp.bfloat16

N_MOD = 6
EPS = 1e-6
GRID_W = 64
N_Q_HEADS = 32
N_KV_HEADS = 8
HEAD_DIM = 128
GQA_GROUP = N_Q_HEADS // N_KV_HEADS
ROPE_THETA = 10000.0
ROPE_AXIS_FREQS = HEAD_DIM // 4
SSD_HEAD_DIM = 64
SSD_GROUPS = 8
SSD_STATE = 128
CONV_W = 5
CHUNK = 128
N_GROUPS = 4
EXPERTS_PER_GROUP = 8
N_EXPERTS = N_GROUPS * EXPERTS_PER_GROUP
TOP_K = 2

LOG2_E = 1.4426950408889634
LANES = 128
HALO = 16
MOE_BLK = 256
ATTN_TQ = 512
ATTN_ROW_SPLIT = 4
VMEM_LIMIT = 56 * 1024 * 1024


def _cparams(sem):
    return pltpu.CompilerParams(dimension_semantics=sem, vmem_limit_bytes=VMEM_LIMIT)


def _tile(n, pref):
    t = min(n, pref)
    while n % t:
        t //= 2
    return t


def _silu(v):
    return v * jax.nn.sigmoid(v)


def _split3(v):
    hi = v.astype(BF16)
    r1 = v - hi.astype(F32)
    mid = r1.astype(BF16)
    lo = (r1 - mid.astype(F32)).astype(BF16)
    return hi, mid, lo


def _dot_exact_lhs(m_bf16, v, dims=(((1,), (0,)), ((), ()))):
    hi, mid, lo = _split3(v)
    out = lax.dot_general(m_bf16, hi, dims, preferred_element_type=F32)
    out = out + lax.dot_general(m_bf16, mid, dims, preferred_element_type=F32)
    return out + lax.dot_general(m_bf16, lo, dims, preferred_element_type=F32)


def _adaln_kernel(c_ref, w_ref, b_ref, o_ref):
    s = _silu(c_ref[...]).astype(BF16)
    o_ref[...] = jnp.dot(s, w_ref[...].astype(BF16), preferred_element_type=F32) + b_ref[...]


def _adaln(cond8, w_ada, b_ada):
    d, n = w_ada.shape
    tn = _tile(n, 512)
    return pl.pallas_call(
        _adaln_kernel,
        grid=(n // tn,),
        in_specs=[pl.BlockSpec((8, d), lambda j: (0, 0)),
                  pl.BlockSpec((d, tn), lambda j: (0, j)),
                  pl.BlockSpec((1, tn), lambda j: (0, j))],
        out_specs=pl.BlockSpec((8, tn), lambda j: (0, j)),
        out_shape=jax.ShapeDtypeStruct((8, n), F32),
        compiler_params=_cparams(("arbitrary",)),
        name="adaln",
    )(cond8, w_ada, b_ada.reshape(1, n))


def _norm_mod(xv, w, shift, scale):
    ms = jnp.mean(xv * xv, axis=-1, keepdims=True)
    h = xv * lax.rsqrt(ms + EPS) * w
    return h * (1.0 + scale) + shift


def _norm_mix_kernel(x_ref, c_ref, w_ref, m_ref, o_ref, *, n_x_tiles):
    i = pl.program_id(0)

    def emit(src):
        o_ref[...] = _norm_mod(src[...], w_ref[...], m_ref[0, 0:1, :], m_ref[0, 1:2, :]).astype(o_ref.dtype)

    @pl.when(i < n_x_tiles)
    def _():
        emit(x_ref)

    @pl.when(i >= n_x_tiles)
    def _():
        emit(c_ref)


def _norm_mix(x2, c2, w, mod3, seq):
    mx, d = x2.shape
    mc = c2.shape[0]
    tm = _tile(min(seq, mc), 256)
    nx, nc = mx // tm, mc // tm
    tpb = seq // tm
    return pl.pallas_call(
        functools.partial(_norm_mix_kernel, n_x_tiles=nx),
        grid=(nx + nc,),
        in_specs=[pl.BlockSpec((tm, d), lambda i: (jnp.minimum(i, nx - 1), 0)),
                  pl.BlockSpec((tm, d), lambda i: (jnp.maximum(i - nx, 0), 0)),
                  pl.BlockSpec((1, d), lambda i: (0, 0)),
                  pl.BlockSpec((1, N_MOD, d), lambda i: (jnp.where(i < nx, i // tpb, 2), 0, 0))],
        out_specs=pl.BlockSpec((tm, d), lambda i: (i, 0)),
        out_shape=jax.ShapeDtypeStruct((mx + mc, d), BF16),
        compiler_params=_cparams(("arbitrary",)),
        name="norm_mix",
    )(x2, c2, w.reshape(1, d), mod3)


def _mm_kernel(a_ref, w_ref, *rest, epilogue):
    extra, o_ref, wbf_ref = rest[:-2], rest[-2], rest[-1]

    @pl.when(pl.program_id(1) == 0)
    def _():
        wbf_ref[...] = w_ref[...].astype(BF16)

    t = jnp.dot(a_ref[...], wbf_ref[...], preferred_element_type=F32)
    o_ref[...] = epilogue(t, *extra).astype(o_ref.dtype)


def _ep_plain(t):
    return t


def _ep_gate(t, g_ref):
    return jax.nn.sigmoid(g_ref[...].astype(F32)) * t


def _ep_gate_add(t, g_ref, prev_ref):
    return prev_ref[...].astype(F32) + jax.nn.sigmoid(g_ref[...].astype(F32)) * t


def _ep_residual(t, x_ref, m_ref, *, gate_row):
    return x_ref[...] + m_ref[0, gate_row:gate_row + 1, :] * t


def _mm(a, w, col0, ncols, m_rows, tm, tn, out_dtype, name, epilogue=_ep_plain, extra=(), extra_specs=()):
    k = a.shape[1]
    assert ncols % tn == 0 and m_rows % tm == 0 and col0 % LANES == 0
    return pl.pallas_call(
        functools.partial(_mm_kernel, epilogue=epilogue),
        grid=(ncols // tn, m_rows // tm),
        in_specs=[pl.BlockSpec((tm, k), lambda j, i: (i, 0)),
                  pl.BlockSpec((pl.Element(k), pl.Element(tn)), lambda j, i: (0, pl.multiple_of(col0 + j * tn, LANES)))]
                 + list(extra_specs),
        out_specs=pl.BlockSpec((tm, tn), lambda j, i: (i, j)),
        out_shape=jax.ShapeDtypeStruct((m_rows, ncols), out_dtype),
        scratch_shapes=[pltpu.VMEM((k, tn), BF16)],
        compiler_params=_cparams(("arbitrary", "arbitrary")),
        name=name,
    )(a, w, *extra)


def _qk_post_kernel(x_ref, w_ref, cos_ref, sin_ref, o_ref, *, n_heads, scale):
    lane = lax.broadcasted_iota(jnp.int32, (x_ref.shape[0], HEAD_DIM), 1)
    first = (lane % (HEAD_DIM // 2)) < (HEAD_DIM // 4)
    for h in range(n_heads):
        sl = slice(h * HEAD_DIM, (h + 1) * HEAD_DIM)
        xh = x_ref[:, sl].astype(F32)
        ms = jnp.mean(xh * xh, axis=-1, keepdims=True)
        y = xh * lax.rsqrt(ms + EPS) * w_ref[...]
        partner = jnp.where(first, pltpu.roll(y, HEAD_DIM - HEAD_DIM // 4, 1), pltpu.roll(y, HEAD_DIM // 4, 1))
        y = y * cos_ref[...] + partner * sin_ref[...]
        o_ref[:, sl] = (y * scale).astype(o_ref.dtype)


def _qk_post(src, col_blk0, n_heads_total, rows, w, cos_t, sin_t, seq, n_x_rows, scale, name):
    tm = _tile(min(seq, rows), 256)
    hpb = 4
    tpb = seq // tm
    nx = n_x_rows // tm
    tab_map = lambda i, j: (jnp.where(i < nx, i % tpb, tpb), 0)
    return pl.pallas_call(
        functools.partial(_qk_post_kernel, n_heads=hpb, scale=scale),
        grid=(rows // tm, n_heads_total // hpb),
        in_specs=[pl.BlockSpec((tm, hpb * HEAD_DIM), lambda i, j: (i, col_blk0 + j)),
                  pl.BlockSpec((1, HEAD_DIM), lambda i, j: (0, 0)),
                  pl.BlockSpec((tm, HEAD_DIM), tab_map),
                  pl.BlockSpec((tm, HEAD_DIM), tab_map)],
        out_specs=pl.BlockSpec((tm, hpb * HEAD_DIM), lambda i, j: (i, j)),
        out_shape=jax.ShapeDtypeStruct((rows, n_heads_total * HEAD_DIM), BF16),
        compiler_params=_cparams(("arbitrary", "arbitrary")),
        name=name,
    )(src, w.reshape(1, HEAD_DIM), cos_t, sin_t)


def _rope_tables(seq, tm):
    rows = seq // GRID_W
    row_pos = jnp.repeat(jnp.arange(rows, dtype=F32), GRID_W)
    col_pos = (jnp.arange(seq) % GRID_W).astype(F32)
    inv_freq = ROPE_THETA ** (-jnp.arange(ROPE_AXIS_FREQS, dtype=F32) / ROPE_AXIS_FREQS)
    ar = row_pos[:, None] * inv_freq
    ac = col_pos[:, None] * inv_freq
    cos_t = jnp.concatenate([jnp.cos(ar), jnp.cos(ar), jnp.cos(ac), jnp.cos(ac)], axis=-1)
    sin_t = jnp.concatenate([-jnp.sin(ar), jnp.sin(ar), -jnp.sin(ac), jnp.sin(ac)], axis=-1)
    cos_t = jnp.concatenate([cos_t, jnp.ones((tm, HEAD_DIM), F32)], axis=0)
    sin_t = jnp.concatenate([sin_t, jnp.zeros((tm, HEAD_DIM), F32)], axis=0)
    return cos_t, sin_t


def _attn_kernel(q_ref, wq_ref, cos_ref, sin_ref, kx_ref, kc_ref, vx_ref, vc_ref, o_ref):
    nt = (((1,), (1,)), ((), ()))
    kx, kc = kx_ref[...], kc_ref[...]

    def with_ones(v):
        lane = lax.broadcasted_iota(jnp.int32, v.shape, 1)
        return jnp.concatenate([v, jnp.where(lane == 0, 1.0, 0.0).astype(v.dtype)], axis=1)

    vx, vc = with_ones(vx_ref[...]), with_ones(vc_ref[...])
    tq = q_ref.shape[0]
    rows = tq // ATTN_ROW_SPLIT
    units = [(g, r) for g in range(GQA_GROUP) for r in range(ATTN_ROW_SPLIT)]

    lane = lax.broadcasted_iota(jnp.int32, (rows, HEAD_DIM), 1)
    first = (lane % (HEAD_DIM // 2)) < (HEAD_DIM // 4)

    def scores(u):
        g, r = u
        rs = slice(r * rows, (r + 1) * rows)
        xh = q_ref[rs, g * HEAD_DIM:(g + 1) * HEAD_DIM].astype(F32)
        y = xh * lax.rsqrt(jnp.mean(xh * xh, axis=-1, keepdims=True) + EPS) * wq_ref[...]
        partner = jnp.where(first, pltpu.roll(y, HEAD_DIM - HEAD_DIM // 4, 1), pltpu.roll(y, HEAD_DIM // 4, 1))
        q = (y * cos_ref[rs, :] + partner * sin_ref[rs, :]).astype(BF16)
        return (lax.dot_general(q, kx, nt, preferred_element_type=F32),
                lax.dot_general(q, kc, nt, preferred_element_type=F32))

    nxt = scores(units[0])
    for n, (g, r) in enumerate(units):
        s1, s2 = nxt
        if n + 1 < len(units):
            nxt = scores(units[n + 1])
        m = jnp.maximum(jnp.max(s1, axis=-1, keepdims=True), jnp.max(s2, axis=-1, keepdims=True))
        p1 = jnp.exp2(s1 - m).astype(BF16)
        p2 = jnp.exp2(s2 - m).astype(BF16)
        o = jnp.dot(p1, vx, preferred_element_type=F32) + jnp.dot(p2, vc, preferred_element_type=F32)
        o_ref[r * rows:(r + 1) * rows, g * HEAD_DIM:(g + 1) * HEAD_DIM] = (
            o[:, :HEAD_DIM] / o[:, HEAD_DIM:HEAD_DIM + 1]).astype(o_ref.dtype)


def _attention(p3, wq, cos_t, sin_t, k, p1, batch, seq, ctx_len):
    tq = _tile(seq, ATTN_TQ)
    qpb = seq // tq
    gw = GQA_GROUP * HEAD_DIM
    cblk0 = batch * seq // ctx_len
    return pl.pallas_call(
        _attn_kernel,
        grid=(batch, N_KV_HEADS, qpb),
        in_specs=[pl.BlockSpec((tq, gw), lambda b, h, i: (b * qpb + i, h)),
                  pl.BlockSpec((1, HEAD_DIM), lambda b, h, i: (0, 0)),
                  pl.BlockSpec((tq, HEAD_DIM), lambda b, h, i: (i, 0)),
                  pl.BlockSpec((tq, HEAD_DIM), lambda b, h, i: (i, 0)),
                  pl.BlockSpec((seq, HEAD_DIM), lambda b, h, i: (b, h)),
                  pl.BlockSpec((ctx_len, HEAD_DIM), lambda b, h, i: (cblk0 + b, h)),
                  pl.BlockSpec((seq, HEAD_DIM), lambda b, h, i: (b, N_KV_HEADS + h)),
                  pl.BlockSpec((ctx_len, HEAD_DIM), lambda b, h, i: (cblk0 + b, N_KV_HEADS + h))],
        out_specs=pl.BlockSpec((tq, gw), lambda b, h, i: (b * qpb + i, h)),
        out_shape=jax.ShapeDtypeStruct((batch * seq, N_Q_HEADS * HEAD_DIM), BF16),
        compiler_params=_cparams(("arbitrary", "arbitrary", "arbitrary")),
        name="attention",
    )(p3, wq.reshape(1, HEAD_DIM), cos_t, sin_t, k, k, p1, p1)


def _conv_kernel(prev_ref, cur_ref, next_ref, w_ref, b_ref, o_ref, buf_ref, *, tl, x_tiles, x_tpb, c_tpb):
    i = pl.program_id(0)
    j = jnp.where(i < x_tiles, i % x_tpb, (i - x_tiles) % c_tpb)
    n = jnp.where(i < x_tiles, x_tpb, c_tpb)
    buf_ref[0:HALO, :] = jnp.where(j == 0, 0.0, prev_ref[...].astype(F32))
    buf_ref[HALO:HALO + tl, :] = cur_ref[...].astype(F32)
    buf_ref[HALO + tl:, :] = jnp.where(j == n - 1, 0.0, next_ref[...].astype(F32))
    pad = CONV_W // 2
    acc = b_ref[...] + w_ref[0:1, :] * buf_ref[HALO - pad:HALO - pad + tl, :]
    for t in range(1, CONV_W):
        acc = acc + w_ref[t:t + 1, :] * buf_ref[HALO - pad + t:HALO - pad + t + tl, :]
    o_ref[...] = _silu(acc).astype(o_ref.dtype)


def _conv_silu(p1, col0, ncols, conv_w, conv_b, seq, ctx_len, n_x_rows):
    rows = p1.shape[0]
    tl = _tile(min(seq, ctx_len), 256)
    tc = _tile(ncols, 2048)
    assert col0 % tc == 0
    cb0 = col0 // tc
    hb = tl // HALO
    last_hb = rows // HALO - 1
    return pl.pallas_call(
        functools.partial(_conv_kernel, tl=tl, x_tiles=n_x_rows // tl, x_tpb=seq // tl, c_tpb=ctx_len // tl),
        grid=(rows // tl, ncols // tc),
        in_specs=[pl.BlockSpec((HALO, tc), lambda i, j: (jnp.maximum(i * hb - 1, 0), cb0 + j)),
                  pl.BlockSpec((tl, tc), lambda i, j: (i, cb0 + j)),
                  pl.BlockSpec((HALO, tc), lambda i, j: (jnp.minimum((i + 1) * hb, last_hb), cb0 + j)),
                  pl.BlockSpec((CONV_W, tc), lambda i, j: (0, j)),
                  pl.BlockSpec((1, tc), lambda i, j: (0, j))],
        out_specs=pl.BlockSpec((tl, tc), lambda i, j: (i, j)),
        out_shape=jax.ShapeDtypeStruct((rows, ncols), BF16),
        scratch_shapes=[pltpu.VMEM((tl + 2 * HALO, tc), F32)],
        compiler_params=_cparams(("arbitrary", "arbitrary")),
        name="conv_silu",
    )(p1, p1, p1, conv_w, conv_b.reshape(1, ncols))


def _dt_prep_kernel(raw_ref, bias_ref, alog_ref, acum_ref, dt_ref, *, n_heads):
    v = raw_ref[...] + bias_ref[...]
    dt = jnp.maximum(v, 0.0) + jnp.log(1.0 + jnp.exp(-jnp.abs(v)))
    dta = dt * (-jnp.exp(alog_ref[...]))
    r = lax.broadcasted_iota(jnp.int32, (CHUNK, CHUNK), 0)
    c = lax.broadcasted_iota(jnp.int32, (CHUNK, CHUNK), 1)
    tril = jnp.where(r >= c, 1.0, 0.0).astype(BF16)
    triu = jnp.where(r <= c, 1.0, 0.0).astype(BF16)
    lane = lax.broadcasted_iota(jnp.int32, dta.shape, 1)
    acum = jnp.where(lane < n_heads, _dot_exact_lhs(tril, dta), _dot_exact_lhs(triu, dta))
    acum_ref[0] = acum.T
    dt_ref[0] = dt.T


def _dt_prep(dt_raw, bias2, alog2, n_heads):
    rows, w = dt_raw.shape
    nch = rows // CHUNK
    out = jax.ShapeDtypeStruct((nch, w, CHUNK), F32)
    return pl.pallas_call(
        functools.partial(_dt_prep_kernel, n_heads=n_heads),
        grid=(nch,),
        in_specs=[pl.BlockSpec((CHUNK, w), lambda i: (i, 0)),
                  pl.BlockSpec((1, w), lambda i: (0, 0)),
                  pl.BlockSpec((1, w), lambda i: (0, 0))],
        out_specs=[pl.BlockSpec((1, w, CHUNK), lambda i: (i, 0, 0))] * 2,
        out_shape=[out, out],
        compiler_params=_cparams(("arbitrary",)),
        name="dt_prep",
    )(dt_raw, bias2, alog2)


def _split3_f32(v):
    hi = v.astype(BF16).astype(F32)
    r1 = v - hi
    mid = r1.astype(BF16).astype(F32)
    lo = (r1 - mid).astype(BF16).astype(F32)
    return [hi, mid, lo]


def _ssd_consts(hpg):
    p = SSD_HEAD_DIM
    gw = hpg * p
    k = np.arange(CHUNK)[:, None]

    def expand(base, width, per):
        col_head = (np.arange(width) // per)[None, :]
        kk = k - base
        return ((kk >= 0) & (kk < 3 * hpg) & (kk % hpg == col_head)).astype(np.float32)

    mats = [expand(0, gw, p), expand(3 * hpg, gw, p), expand(9 * hpg, gw, p), expand(6 * hpg, hpg * CHUNK, CHUNK)]
    return jnp.asarray(np.concatenate(mats, axis=1), dtype=BF16)


def _ssd_kernel(xf_ref, bf_ref, cf_ref, af_ref, df_ref, xb_ref, bb_ref, cb_ref, ab_ref, db_ref, h0_ref, k_ref,
                yf_ref, yb_ref, hfin_ref, h_scr, *, hpg):
    k = pl.program_id(2)
    nck = pl.num_programs(2)
    p = SSD_HEAD_DIM
    gw = hpg * p

    @pl.when(k == 0)
    def _():
        h_scr[...] = h0_ref[0, 0]

    ri = lax.broadcasted_iota(jnp.int32, (CHUNK, CHUNK), 0)
    ci = lax.broadcasted_iota(jnp.int32, (CHUNK, CHUNK), 1)
    low_half = (lax.broadcasted_iota(jnp.int32, (CHUNK, gw), 1) % (2 * p)) < p
    nt = (((1,), (1,)), ((), ()))
    e_exp_a, e_to_end, e_gain = k_ref[:, 0:gw], k_ref[:, gw:2 * gw], k_ref[:, 2 * gw:3 * gw]
    e_col_a = k_ref[:, 3 * gw:]
    pad_rows = jnp.zeros((CHUNK - 12 * hpg, CHUNK), F32)

    for d, (x_ref, b_ref, c_ref, a_ref, dt_ref, y_ref) in enumerate(
            ((xf_ref, bf_ref, cf_ref, af_ref, df_ref, yf_ref), (xb_ref, bb_ref, cb_ref, ab_ref, db_ref, yb_ref))):
        row_a = a_ref[0]
        row_dt = dt_ref[0]
        if d == 0:
            mask = ri >= ci
            tot = row_a[:, CHUNK - 1:CHUNK]
        else:
            mask = ri <= ci
            tot = row_a[:, 0:1]
        exp_a = jnp.exp(row_a)
        to_end = jnp.exp(tot - row_a) * row_dt
        gain = jnp.broadcast_to(jnp.exp(tot), row_a.shape)
        table = jnp.concatenate(_split3_f32(exp_a) + _split3_f32(to_end) + _split3_f32(row_a)
                                + _split3_f32(gain) + [pad_rows], axis=0)
        tab_t = table.T.astype(BF16)
        exp_a_full = jnp.dot(tab_t, e_exp_a, preferred_element_type=F32)
        to_end_full = jnp.dot(tab_t, e_to_end, preferred_element_type=F32)
        gain_full = jnp.dot(tab_t[0:16], e_gain, preferred_element_type=F32)[0:1]
        col_a = jnp.dot(tab_t, e_col_a, preferred_element_type=F32)
        bc = b_ref[...]
        cc = c_ref[...]
        cb = lax.dot_general(cc, bc, nt, preferred_element_type=F32)
        x32 = x_ref[...].astype(F32)
        x_lo = jnp.where(low_half, x32, 0.0).astype(BF16)
        x_hi = jnp.where(low_half, 0.0, x32).astype(BF16)
        y_parts = []
        for q in range(hpg // 2):
            ws = []
            for r in (2 * q, 2 * q + 1):
                seg = col_a[:, r * CHUNK:(r + 1) * CHUNK] - row_a[r:r + 1, :]
                ws.append((cb * jnp.exp(jnp.where(mask, seg, -jnp.inf)) * row_dt[r:r + 1, :]).astype(BF16))
            sl = slice(q * 2 * p, (q + 1) * 2 * p)
            y_parts.append(jnp.dot(jnp.concatenate(ws, axis=1), jnp.concatenate([x_lo[:, sl], x_hi[:, sl]], axis=0),
                                   preferred_element_type=F32))
        y_state = jnp.dot(cc, h_scr[d].astype(BF16), preferred_element_type=F32)
        y_ref[...] = (jnp.concatenate(y_parts, axis=1) + y_state * exp_a_full).astype(y_ref.dtype)
        bct = bc.astype(F32).T.astype(BF16)
        upd = jnp.dot(bct, (x32 * to_end_full).astype(BF16), preferred_element_type=F32)
        h_scr[d] = h_scr[d] * gain_full + upd

    @pl.when(k == nck - 1)
    def _():
        hfin_ref[0, 0] = h_scr[...]


def _ssd(xbc, acum_t, dt_t, h0, consts, batch, length, chunk0, hpg, d_inner):
    nck = length // CHUNK
    gw = hpg * SSD_HEAD_DIM
    assert hpg % 2 == 0 and 2 * SSD_HEAD_DIM == LANES and 12 * hpg <= CHUNK
    b_blk0 = d_inner // SSD_STATE
    c_blk0 = b_blk0 + SSD_GROUPS
    fwd = lambda b, g, k: chunk0 + b * nck + k
    bwd = lambda b, g, k: chunk0 + b * nck + (nck - 1 - k)

    def dir_specs(ch, d):
        return [pl.BlockSpec((CHUNK, gw), lambda b, g, k: (ch(b, g, k), g)),
                pl.BlockSpec((CHUNK, SSD_STATE), lambda b, g, k: (ch(b, g, k), b_blk0 + g)),
                pl.BlockSpec((CHUNK, SSD_STATE), lambda b, g, k: (ch(b, g, k), c_blk0 + g)),
                pl.BlockSpec((1, hpg, CHUNK), lambda b, g, k: (ch(b, g, k), d * SSD_GROUPS + g, 0)),
                pl.BlockSpec((1, hpg, CHUNK), lambda b, g, k: (ch(b, g, k), d * SSD_GROUPS + g, 0))]

    st_spec = pl.BlockSpec((1, 1, 2, SSD_STATE, gw), lambda b, g, k: (b, g, 0, 0, 0))
    y_shape = jax.ShapeDtypeStruct((batch * length, d_inner), BF16)
    return pl.pallas_call(
        functools.partial(_ssd_kernel, hpg=hpg),
        grid=(batch, SSD_GROUPS, nck),
        in_specs=dir_specs(fwd, 0) + dir_specs(bwd, 1) + [st_spec, pl.BlockSpec(consts.shape, lambda b, g, k: (0, 0))],
        out_specs=[pl.BlockSpec((CHUNK, gw), lambda b, g, k: (b * nck + k, g)),
                   pl.BlockSpec((CHUNK, gw), lambda b, g, k: (b * nck + (nck - 1 - k), g)),
                   st_spec],
        out_shape=[y_shape, y_shape, jax.ShapeDtypeStruct(h0.shape, F32)],
        scratch_shapes=[pltpu.VMEM((2, SSD_STATE, gw), F32)],
        compiler_params=_cparams(("arbitrary", "arbitrary", "arbitrary")),
        name="ssd",
    )(xbc, xbc, xbc, acum_t, dt_t, xbc, xbc, xbc, acum_t, dt_t, h0, consts)


def _ssd_norm_kernel(yf_ref, yb_ref, xs_ref, z_ref, ds_ref, w_ref, o_ref):
    y = yf_ref[...].astype(F32) + yb_ref[...].astype(F32) + ds_ref[...] * xs_ref[...].astype(F32)
    y = y * _silu(z_ref[...].astype(F32))
    ms = jnp.mean(y * y, axis=-1, keepdims=True)
    o_ref[...] = (y * lax.rsqrt(ms + EPS) * w_ref[...]).astype(o_ref.dtype)


def _ssd_norm(yf, yb, xbc, p3, z_blk, dskip_row, w):
    m, d = yf.shape
    tm = _tile(m, 256)
    row = lambda i: (i, 0)
    return pl.pallas_call(
        _ssd_norm_kernel,
        grid=(m // tm,),
        in_specs=[pl.BlockSpec((tm, d), row), pl.BlockSpec((tm, d), row), pl.BlockSpec((tm, d), row),
                  pl.BlockSpec((tm, d), lambda i: (i, z_blk)),
                  pl.BlockSpec((1, d), lambda i: (0, 0)), pl.BlockSpec((1, d), lambda i: (0, 0))],
        out_specs=pl.BlockSpec((tm, d), row),
        out_shape=jax.ShapeDtypeStruct((m, d), BF16),
        compiler_params=_cparams(("arbitrary",)),
        name="ssd_norm",
    )(yf, yb, xbc, p3, dskip_row, w.reshape(1, d))


def _merge(attn, ynorm, w1, w2, p3, g1_col0, g2_col0):
    m, _ = attn.shape
    n = w1.shape[1]
    tm, tn = _tile(m, 1024), _tile(n, 512)
    blk = lambda c0: pl.BlockSpec((tm, tn), lambda j, i: (i, c0 // tn + j))
    part = _mm(attn, w1, 0, n, m, tm, tn, BF16, "merge_attn", _ep_gate, (p3,), (blk(g1_col0),))
    return _mm(ynorm, w2, 0, n, m, tm, tn, BF16, "merge_ssd", _ep_gate_add, (p3, part), (blk(g2_col0), blk(0)))


def _out_proj(merged, w, x2, mod3, seq, gate_row):
    m, _ = merged.shape
    n = w.shape[1]
    tm, tn = _tile(seq, 1024), _tile(n, 512)
    tpb = seq // tm
    return _mm(merged, w, 0, n, m, tm, tn, F32, "out_proj", functools.partial(_ep_residual, gate_row=gate_row),
               (x2, mod3), (pl.BlockSpec((tm, tn), lambda j, i: (i, j)),
                            pl.BlockSpec((1, N_MOD, tn), lambda j, i: (i // tpb, 0, j))))


def _ffn_norm_route_kernel(x_ref, w_ref, m_ref, wr_ref, br_ref, h_ref, route_ref):
    h = _norm_mod(x_ref[...], w_ref[...], m_ref[0, 3:4, :], m_ref[0, 4:5, :])
    h_ref[...] = h
    h_hi = h.astype(BF16)
    h_lo = (h - h_hi.astype(F32)).astype(BF16)
    wr = wr_ref[...]
    w_hi = wr.astype(BF16)
    w_lo = (wr - w_hi.astype(F32)).astype(BF16)
    logits = (jnp.dot(h_hi, w_hi, preferred_element_type=F32) + jnp.dot(h_hi, w_lo, preferred_element_type=F32)
              + jnp.dot(h_lo, w_hi, preferred_element_type=F32)) + br_ref[...]
    lane = lax.broadcasted_iota(jnp.int32, logits.shape, 1).astype(F32)
    big = float(LANES)
    neg = -jnp.inf
    gl = jnp.where(lane < N_GROUPS, logits, neg)
    gmax = jnp.max(gl, axis=-1, keepdims=True)
    gidx = jnp.min(jnp.where(gl == gmax, lane, big), axis=-1, keepdims=True)
    g_prob = 1.0 / jnp.sum(jnp.exp(gl - gmax), axis=-1, keepdims=True)
    lo = N_GROUPS + gidx * EXPERTS_PER_GROUP
    el = jnp.where((lane >= lo) & (lane < lo + EXPERTS_PER_GROUP), logits, neg)
    m1 = jnp.max(el, axis=-1, keepdims=True)
    i1 = jnp.min(jnp.where(el == m1, lane, big), axis=-1, keepdims=True)
    el2 = jnp.where(lane == i1, neg, el)
    m2 = jnp.max(el2, axis=-1, keepdims=True)
    i2 = jnp.min(jnp.where(el2 == m2, lane, big), axis=-1, keepdims=True)
    z = jnp.sum(jnp.exp(el - m1), axis=-1, keepdims=True)
    p1 = 1.0 / z
    p2 = jnp.exp(m2 - m1) / z
    w1 = g_prob * p1 / (p1 + p2)
    w2 = g_prob * p2 / (p1 + p2)
    route = jnp.where(lane == 0, i1 - N_GROUPS, jnp.where(lane == 1, i2 - N_GROUPS,
                      jnp.where(lane == 2, w1, jnp.where(lane == 3, w2, 0.0))))
    route_ref[...] = route


def _ffn_norm_route(x1, w, mod3, wr, br, seq):
    m, d = x1.shape
    tm = _tile(seq, 256)
    tpb = seq // tm
    return pl.pallas_call(
        _ffn_norm_route_kernel,
        grid=(m // tm,),
        in_specs=[pl.BlockSpec((tm, d), lambda i: (i, 0)),
                  pl.BlockSpec((1, d), lambda i: (0, 0)),
                  pl.BlockSpec((1, N_MOD, d), lambda i: (i // tpb, 0, 0)),
                  pl.BlockSpec((d, LANES), lambda i: (0, 0)),
                  pl.BlockSpec((1, LANES), lambda i: (0, 0))],
        out_specs=[pl.BlockSpec((tm, d), lambda i: (i, 0)), pl.BlockSpec((tm, LANES), lambda i: (i, 0))],
        out_shape=[jax.ShapeDtypeStruct((m, d), F32), jax.ShapeDtypeStruct((m, LANES), F32)],
        compiler_params=_cparams(("arbitrary",)),
        name="ffn_norm_route",
    )(x1, w.reshape(1, d), mod3, wr, br)


def _gather_kernel(tok_ref, nv_ref, src_ref, o_ref, buf_ref, sem, *, blk):
    i = pl.program_id(0)
    base = i * blk
    nv = nv_ref[i]

    def issue(r, c):
        pltpu.make_async_copy(src_ref.at[pl.ds(tok_ref[base + r], 1)], buf_ref.at[pl.ds(r, 1)], sem).start()
        return c

    lax.fori_loop(0, nv, issue, 0)

    def zero(r, c):
        buf_ref[pl.ds(r, 1), :] = jnp.zeros((1, buf_ref.shape[1]), buf_ref.dtype)
        return c

    lax.fori_loop(nv, blk, zero, 0)

    def drain(r, c):
        pltpu.make_async_copy(src_ref.at[pl.ds(0, 1)], buf_ref.at[pl.ds(0, 1)], sem).wait()
        return c

    lax.fori_loop(0, nv, drain, 0)
    o_ref[...] = buf_ref[...].astype(o_ref.dtype)


def _gather_rows(row_tok, n_valid, src, blk):
    n_rows = row_tok.shape[0]
    d = src.shape[1]
    return pl.pallas_call(
        functools.partial(_gather_kernel, blk=blk),
        grid_spec=pltpu.PrefetchScalarGridSpec(
            num_scalar_prefetch=2,
            grid=(n_rows // blk,),
            in_specs=[pl.BlockSpec(memory_space=pl.ANY)],
            out_specs=pl.BlockSpec((blk, d), lambda i, tok, nv: (i, 0)),
            scratch_shapes=[pltpu.VMEM((blk, d), src.dtype), pltpu.SemaphoreType.DMA(())]),
        out_shape=jax.ShapeDtypeStruct((n_rows, d), BF16),
        compiler_params=_cparams(("arbitrary",)),
        name="moe_gather",
    )(row_tok, n_valid, src)


def _expert_mm_kernel(ie_ref, it_ref, ib_ref, if_ref, iv_ref, ne_ref, nt_ref, hn_ref, a_ref, *rest, n_w, tn, compute):
    w_hbm = rest[:n_w]
    o_ref = rest[n_w]
    stage = rest[n_w + 1:2 * n_w + 1]
    w_bf = rest[2 * n_w + 1:3 * n_w + 1]
    sem = rest[3 * n_w + 1]
    i = pl.program_id(0)

    def copies(e, t):
        col = pl.multiple_of(t * tn, LANES)
        return [pltpu.make_async_copy(w_hbm[n].at[e, :, pl.ds(col, tn)], stage[n], sem.at[n]) for n in range(n_w)]

    @pl.when(i == 0)
    def _():
        for cp in copies(ie_ref[0], it_ref[0]):
            cp.start()

    @pl.when(if_ref[i] == 1)
    def _():
        for cp in copies(ie_ref[i], it_ref[i]):
            cp.wait()
        for n in range(n_w):
            w_bf[n][...] = stage[n][...].astype(BF16)

        @pl.when(hn_ref[i] == 1)
        def _():
            for cp in copies(ne_ref[i], nt_ref[i]):
                cp.start()

    @pl.when(iv_ref[i] == 1)
    def _():
        o_ref[...] = compute(a_ref[...], [w[...] for w in w_bf]).astype(o_ref.dtype)

    @pl.when(iv_ref[i] == 0)
    def _():
        o_ref[...] = jnp.zeros(o_ref.shape, o_ref.dtype)


def _expert_mm(items, a, weights, blk, tn, compute, out_dtype, name):
    n_items = items[0].shape[0]
    n_rows, k = a.shape
    n = weights[0].shape[2]
    n_w = len(weights)
    imap_a = lambda i, ie, it, ib, *_: (ib[i], 0)
    imap_o = lambda i, ie, it, ib, *_: (ib[i], it[i])
    return pl.pallas_call(
        functools.partial(_expert_mm_kernel, n_w=n_w, tn=tn, compute=compute),
        grid_spec=pltpu.PrefetchScalarGridSpec(
            num_scalar_prefetch=len(items),
            grid=(n_items,),
            in_specs=[pl.BlockSpec((blk, k), imap_a)] + [pl.BlockSpec(memory_space=pl.ANY)] * n_w,
            out_specs=pl.BlockSpec((blk, tn), imap_o),
            scratch_shapes=([pltpu.VMEM((k, tn), F32)] * n_w + [pltpu.VMEM((k, tn), BF16)] * n_w
                            + [pltpu.SemaphoreType.DMA((n_w,))])),
        out_shape=jax.ShapeDtypeStruct((n_rows, n), out_dtype),
        compiler_params=_cparams(("arbitrary",)),
        name=name,
    )(*items, a, *weights)


def _up_compute(xb, ws):
    a = jnp.dot(xb, ws[0], preferred_element_type=F32)
    u = jnp.dot(xb, ws[1], preferred_element_type=F32)
    return _silu(a) * u


def _down_compute(hb, ws):
    return jnp.dot(hb, ws[0], preferred_element_type=F32)


def _combine_kernel(pos_ref, y_ref, x_ref, rt_ref, m_ref, o_ref, buf_ref, sem, *, tm):
    base = pl.program_id(0) * tm

    def issue(r, c):
        for kk in range(TOP_K):
            pltpu.make_async_copy(y_ref.at[pl.ds(pos_ref[TOP_K * (base + r) + kk], 1)],
                                  buf_ref.at[kk, pl.ds(r, 1)], sem).start()
        return c

    lax.fori_loop(0, tm, issue, 0)

    def drain(r, c):
        pltpu.make_async_copy(y_ref.at[pl.ds(0, 1)], buf_ref.at[0, pl.ds(0, 1)], sem).wait()
        return c

    lax.fori_loop(0, TOP_K * tm, drain, 0)
    rt = rt_ref[...]
    moe = rt[:, 2:3] * buf_ref[0] + rt[:, 3:4] * buf_ref[1]
    o_ref[...] = x_ref[...] + m_ref[0, 5:6, :] * moe


def _combine(pos, yexp, x1, route, mod3, seq):
    m, d = x1.shape
    tm = _tile(seq, 128)
    tpb = seq // tm
    return pl.pallas_call(
        functools.partial(_combine_kernel, tm=tm),
        grid_spec=pltpu.PrefetchScalarGridSpec(
            num_scalar_prefetch=1,
            grid=(m // tm,),
            in_specs=[pl.BlockSpec(memory_space=pl.ANY),
                      pl.BlockSpec((tm, d), lambda i, pos: (i, 0)),
                      pl.BlockSpec((tm, LANES), lambda i, pos: (i, 0)),
                      pl.BlockSpec((1, N_MOD, d), lambda i, pos: (i // tpb, 0, 0))],
            out_specs=pl.BlockSpec((tm, d), lambda i, pos: (i, 0)),
            scratch_shapes=[pltpu.VMEM((TOP_K, tm, d), F32), pltpu.SemaphoreType.DMA(())]),
        out_shape=jax.ShapeDtypeStruct((m, d), F32),
        compiler_params=_cparams(("arbitrary",)),
        name="moe_combine",
    )(pos, yexp, x1, route, mod3)


def _moe_plan(route, blk, n_tiles_up, n_tiles_down):
    t = route.shape[0]
    n_assign = t * TOP_K
    nb_max = n_assign // blk + N_EXPERTS
    flat_e = route[:, 0:TOP_K].astype(jnp.int32).reshape(-1)
    onehot = (flat_e[:, None] == jnp.arange(N_EXPERTS, dtype=jnp.int32)[None, :]).astype(jnp.int32)
    cum = jnp.cumsum(onehot, axis=0)
    rank = jnp.take_along_axis(cum, flat_e[:, None], axis=1)[:, 0] - 1
    counts = cum[-1]
    nblk = (counts + blk - 1) // blk
    pend = jnp.cumsum(nblk)
    pstart = pend - nblk
    dest = pstart[flat_e] * blk + rank
    row_tok = jnp.zeros((nb_max * blk,), jnp.int32).at[dest].set(jnp.arange(n_assign, dtype=jnp.int32) // TOP_K)
    total = pend[-1]
    unused = jnp.maximum(nb_max - total, 1)

    def items(n_tiles):
        j = jnp.arange(n_tiles * nb_max, dtype=jnp.int32)
        valid = j < n_tiles * total
        jj = jnp.minimum(j, n_tiles * total - 1)
        e = jnp.minimum(jnp.sum((jj[:, None] >= n_tiles * pend[None, :]).astype(jnp.int32), axis=1), N_EXPERTS - 1)
        local = jj - n_tiles * pstart[e]
        nb_e = jnp.maximum(nblk[e], 1)
        u = j - n_tiles * total
        tile = jnp.where(valid, local // nb_e, u // unused)
        b = jnp.where(valid, pstart[e] + local % nb_e, total + u % unused)
        first = valid & (local % nb_e == 0)
        nxt = j + nb_e
        has_next = first & (nxt < n_tiles * total)
        nxt = jnp.minimum(nxt, n_tiles * nb_max - 1)
        i32 = lambda v: v.astype(jnp.int32)
        return (e, i32(tile), i32(b), i32(first), i32(valid), e[nxt], i32(tile[nxt]), i32(has_next))

    blk_ids = jnp.arange(nb_max, dtype=jnp.int32)
    blk_e = jnp.minimum(jnp.sum((blk_ids[:, None] >= pend[None, :]).astype(jnp.int32), axis=1), N_EXPERTS - 1)
    n_valid = jnp.clip(counts[blk_e] - (blk_ids - pstart[blk_e]) * blk, 0, blk)
    n_valid = jnp.where(blk_ids < total, n_valid, 0).astype(jnp.int32)
    return row_tok, n_valid, dest, items(n_tiles_up), items(n_tiles_down)


def kernel(x, c, ctx, c_ctx, w_ada, b_ada, norm_mix_w, norm_ffn_w, w_in, q_norm_w, k_norm_w, conv_w, conv_b,
           a_log_f, a_log_b, dt_bias_f, dt_bias_b, d_skip, ssd_norm_w, w_attn_proj, w_ssd_proj, w_out,
           w_router_group, b_router_group, w_router_expert, b_router_expert, w_exp_gate, w_exp_up, w_exp_down):
    batch, seq, d = x.shape
    ctx_len = ctx.shape[1]
    assert w_ada.shape[0] == 1, "single layer: the context stream is read, never updated"
    d_inner = d
    n_ssd_heads = d_inner // SSD_HEAD_DIM
    hpg = n_ssd_heads // SSD_GROUPS
    kv_dim = N_KV_HEADS * HEAD_DIM
    q_dim = N_Q_HEADS * HEAD_DIM
    bc_dim = SSD_GROUPS * SSD_STATE
    xbc_dim = d_inner + 2 * bc_dim
    p1_cols = 2 * kv_dim + xbc_dim
    dt_cols = 2 * n_ssd_heads
    p3_col0 = p1_cols + dt_cols
    p3_cols = q_dim + 2 * d + d_inner
    assert dt_cols == LANES
    mx, mc = batch * seq, batch * ctx_len

    x2 = x.reshape(mx, d)
    c2 = ctx.reshape(mc, d)
    w_in0 = w_in[0]

    cond8 = jnp.zeros((8, d), F32).at[0:batch].set(c).at[batch].set(c_ctx)
    assert batch == 2
    mod3 = _adaln(cond8, w_ada[0], b_ada[0]).reshape(8, N_MOD, d)

    h_all = _norm_mix(x2, c2, norm_mix_w[0], mod3, seq)
    tm_all = 1088 if (mx + mc) % 1088 == 0 else _tile(mx + mc, 512)
    p1 = _mm(h_all, w_in0, 0, p1_cols, mx + mc, tm_all, 512, BF16, "in_proj_kvx")
    dt_raw = _mm(h_all, w_in0, p1_cols, dt_cols, mx + mc, _tile(mx + mc, 512), dt_cols, F32, "in_proj_dt")
    p3 = _mm(h_all, w_in0, p3_col0, p3_cols, mx, _tile(mx, 1024), 512, BF16, "in_proj_qgz")

    tmq = _tile(min(seq, mc), 256)
    cos_t, sin_t = _rope_tables(seq, tmq)
    k = _qk_post(p1, 0, N_KV_HEADS, mx + mc, k_norm_w[0], cos_t, sin_t, seq, mx, 1.0, "k_post")
    attn = _attention(p3, q_norm_w[0] * (HEAD_DIM ** -0.5 * LOG2_E), cos_t, sin_t, k, p1, batch, seq, ctx_len)

    xbc = _conv_silu(p1, 2 * kv_dim, xbc_dim, conv_w[0], conv_b[0], seq, ctx_len, mx)
    bias2 = jnp.concatenate([dt_bias_f[0], dt_bias_b[0]]).reshape(1, dt_cols)
    alog2 = jnp.concatenate([a_log_f[0], a_log_b[0]]).reshape(1, dt_cols)
    acum_t, dt_t = _dt_prep(dt_raw, bias2, alog2, n_ssd_heads)
    h_zero = jnp.zeros((batch, SSD_GROUPS, 2, SSD_STATE, hpg * SSD_HEAD_DIM), F32)
    ssd_consts = _ssd_consts(hpg)
    _, _, h_ctx = _ssd(xbc, acum_t, dt_t, h_zero, ssd_consts, batch, ctx_len, mx // CHUNK, hpg, d_inner)
    yf, yb, _ = _ssd(xbc, acum_t, dt_t, h_ctx, ssd_consts, batch, seq, 0, hpg, d_inner)
    dskip_row = jnp.repeat(d_skip[0], SSD_HEAD_DIM).reshape(1, d_inner)
    ynorm = _ssd_norm(yf, yb, xbc, p3, (q_dim + 2 * d) // d_inner, dskip_row, ssd_norm_w[0])

    merged = _merge(attn, ynorm, w_attn_proj[0], w_ssd_proj[0], p3, q_dim, q_dim + d)
    x1 = _out_proj(merged, w_out[0], x2, mod3, seq, 2)

    wr = jnp.zeros((d, LANES), F32).at[:, :N_GROUPS].set(w_router_group[0])
    wr = wr.at[:, N_GROUPS:N_GROUPS + N_EXPERTS].set(w_router_expert[0])
    br = jnp.zeros((1, LANES), F32).at[0, :N_GROUPS].set(b_router_group[0])
    br = br.at[0, N_GROUPS:N_GROUPS + N_EXPERTS].set(b_router_expert[0])
    h2, route = _ffn_norm_route(x1, norm_ffn_w[0], mod3, wr, br, seq)
    d_exp = w_exp_gate.shape[-1]
    tn_up, tn_down = _tile(d_exp, 512), _tile(d, 4096)
    row_tok, n_valid, dest, items_up, items_down = _moe_plan(route, MOE_BLK, d_exp // tn_up, d // tn_down)
    xg = _gather_rows(row_tok, n_valid, h2, MOE_BLK)
    hid = _expert_mm(items_up, xg, (w_exp_gate[0], w_exp_up[0]), MOE_BLK, tn_up, _up_compute, BF16, "moe_up")
    yexp = _expert_mm(items_down, hid, (w_exp_down[0],), MOE_BLK, tn_down, _down_compute, F32, "moe_down")
    out = _combine(dest, yexp, x1, route, mod3, seq)
    return out.reshape(batch, seq, d)
```

```python
import functools

import numpy as np
import jax
import jax.numpy as jnp
from jax import lax
from jax.experimental import pallas as pl
from jax.experimental.pallas import tpu as pltpu

F32 = jnp.float32
BF16 = jnp.bfloat16

N_MOD = 6
EPS = 1e-6
GRID_W = 64
N_Q_HEADS = 32
N_KV_HEADS = 8
HEAD_DIM = 128
GQA_GROUP = N_Q_HEADS // N_KV_HEADS
ROPE_THETA = 10000.0
ROPE_AXIS_FREQS = HEAD_DIM // 4
SSD_HEAD_DIM = 64
SSD_GROUPS = 8
SSD_STATE = 128
CONV_W = 5
CHUNK = 128
N_GROUPS = 4
EXPERTS_PER_GROUP = 8
N_EXPERTS = N_GROUPS * EXPERTS_PER_GROUP
TOP_K = 2

LOG2_E = 1.4426950408889634
LANES = 128
HALO = 16
SSD_GROUPS_PER_STEP = 8
MOE_BLK = 256
ATTN_TQ = 512
ATTN_ROW_SPLIT = 4
VMEM_LIMIT = 56 * 1024 * 1024


def _cparams(sem):
    return pltpu.CompilerParams(dimension_semantics=sem, vmem_limit_bytes=VMEM_LIMIT)


def _tile(n, pref):
    t = min(n, pref)
    while n % t:
        t //= 2
    return t


def _silu(v):
    return v * jax.nn.sigmoid(v)


def _split3(v):
    hi = v.astype(BF16)
    r1 = v - hi.astype(F32)
    mid = r1.astype(BF16)
    lo = (r1 - mid.astype(F32)).astype(BF16)
    return hi, mid, lo


def _dot_exact_lhs(m_bf16, v, dims=(((1,), (0,)), ((), ()))):
    hi, mid, lo = _split3(v)
    out = lax.dot_general(m_bf16, hi, dims, preferred_element_type=F32)
    out = out + lax.dot_general(m_bf16, mid, dims, preferred_element_type=F32)
    return out + lax.dot_general(m_bf16, lo, dims, preferred_element_type=F32)


def _adaln_kernel(c_ref, w_ref, b_ref, o_ref):
    s = _silu(c_ref[...]).astype(BF16)
    o_ref[...] = jnp.dot(s, w_ref[...].astype(BF16), preferred_element_type=F32) + b_ref[...]


def _adaln(cond8, w_ada, b_ada):
    d, n = w_ada.shape
    tn = _tile(n, 512)
    return pl.pallas_call(
        _adaln_kernel,
        grid=(n // tn,),
        in_specs=[pl.BlockSpec((8, d), lambda j: (0, 0)),
                  pl.BlockSpec((d, tn), lambda j: (0, j)),
                  pl.BlockSpec((1, tn), lambda j: (0, j))],
        out_specs=pl.BlockSpec((8, tn), lambda j: (0, j)),
        out_shape=jax.ShapeDtypeStruct((8, n), F32),
        compiler_params=_cparams(("arbitrary",)),
        name="adaln",
    )(cond8, w_ada, b_ada.reshape(1, n))


def _norm_mod(xv, w, shift, scale):
    ms = jnp.mean(xv * xv, axis=-1, keepdims=True)
    h = xv * lax.rsqrt(ms + EPS) * w
    return h * (1.0 + scale) + shift


def _norm_mix_kernel(x_ref, c_ref, w_ref, m_ref, o_ref, *, n_x_tiles):
    i = pl.program_id(0)

    def emit(src):
        o_ref[...] = _norm_mod(src[...], w_ref[...], m_ref[0, 0:1, :], m_ref[0, 1:2, :]).astype(o_ref.dtype)

    @pl.when(i < n_x_tiles)
    def _():
        emit(x_ref)

    @pl.when(i >= n_x_tiles)
    def _():
        emit(c_ref)


def _norm_mix(x2, c2, w, mod3, seq):
    mx, d = x2.shape
    mc = c2.shape[0]
    tm = _tile(min(seq, mc), 256)
    nx, nc = mx // tm, mc // tm
    tpb = seq // tm
    return pl.pallas_call(
        functools.partial(_norm_mix_kernel, n_x_tiles=nx),
        grid=(nx + nc,),
        in_specs=[pl.BlockSpec((tm, d), lambda i: (jnp.minimum(i, nx - 1), 0)),
                  pl.BlockSpec((tm, d), lambda i: (jnp.maximum(i - nx, 0), 0)),
                  pl.BlockSpec((1, d), lambda i: (0, 0)),
                  pl.BlockSpec((1, N_MOD, d), lambda i: (jnp.where(i < nx, i // tpb, 2), 0, 0))],
        out_specs=pl.BlockSpec((tm, d), lambda i: (i, 0)),
        out_shape=jax.ShapeDtypeStruct((mx + mc, d), BF16),
        compiler_params=_cparams(("arbitrary",)),
        name="norm_mix",
    )(x2, c2, w.reshape(1, d), mod3)


def _mm_kernel(a_ref, w_hbm, *rest, epilogue, col0, tn, n_col_tiles):
    extra, o_ref, stage_ref, wbf_ref, sem = rest[:-4], rest[-4], rest[-3], rest[-2], rest[-1]
    j, i = pl.program_id(0), pl.program_id(1)

    def w_copy(jj):
        col = pl.multiple_of(col0 + jj * tn, LANES)
        return pltpu.make_async_copy(w_hbm.at[:, pl.ds(col, tn)], stage_ref, sem)

    @pl.when((i == 0) & (j == 0))
    def _():
        w_copy(0).start()

    @pl.when(i == 0)
    def _():
        w_copy(j).wait()
        wbf_ref[...] = stage_ref[...].astype(BF16)

        @pl.when(j + 1 < n_col_tiles)
        def _():
            w_copy(j + 1).start()

    t = jnp.dot(a_ref[...], wbf_ref[...], preferred_element_type=F32)
    o_ref[...] = epilogue(t, *extra).astype(o_ref.dtype)


def _ep_plain(t):
    return t


def _ep_gate(t, g_ref):
    return jax.nn.sigmoid(g_ref[...].astype(F32)) * t


def _ep_gate_add(t, g_ref, prev_ref):
    return prev_ref[...].astype(F32) + jax.nn.sigmoid(g_ref[...].astype(F32)) * t


def _ep_residual(t, x_ref, m_ref, *, gate_row):
    return x_ref[...] + m_ref[0, gate_row:gate_row + 1, :] * t


def _mm(a, w, col0, ncols, m_rows, tm, tn, out_dtype, name, epilogue=_ep_plain, extra=(), extra_specs=()):
    k = a.shape[1]
    assert ncols % tn == 0 and m_rows % tm == 0 and col0 % LANES == 0
    return pl.pallas_call(
        functools.partial(_mm_kernel, epilogue=epilogue, col0=col0, tn=tn, n_col_tiles=ncols // tn),
        grid=(ncols // tn, m_rows // tm),
        in_specs=[pl.BlockSpec((tm, k), lambda j, i: (i, 0)), pl.BlockSpec(memory_space=pl.ANY)] + list(extra_specs),
        out_specs=pl.BlockSpec((tm, tn), lambda j, i: (i, j)),
        out_shape=jax.ShapeDtypeStruct((m_rows, ncols), out_dtype),
        scratch_shapes=[pltpu.VMEM((k, tn), F32), pltpu.VMEM((k, tn), BF16), pltpu.SemaphoreType.DMA(())],
        compiler_params=_cparams(("arbitrary", "arbitrary")),
        name=name,
    )(a, w, *extra)


def _qk_post_kernel(x_ref, w_ref, cos_ref, sin_ref, o_ref, *, n_heads, scale):
    lane = lax.broadcasted_iota(jnp.int32, (x_ref.shape[0], HEAD_DIM), 1)
    first = (lane % (HEAD_DIM // 2)) < (HEAD_DIM // 4)
    for h in range(n_heads):
        sl = slice(h * HEAD_DIM, (h + 1) * HEAD_DIM)
        xh = x_ref[:, sl].astype(F32)
        ms = jnp.mean(xh * xh, axis=-1, keepdims=True)
        y = xh * lax.rsqrt(ms + EPS) * w_ref[...]
        partner = jnp.where(first, pltpu.roll(y, HEAD_DIM - HEAD_DIM // 4, 1), pltpu.roll(y, HEAD_DIM // 4, 1))
        y = y * cos_ref[...] + partner * sin_ref[...]
        o_ref[:, sl] = (y * scale).astype(o_ref.dtype)


def _qk_post(src, col_blk0, n_heads_total, rows, w, cos_t, sin_t, seq, n_x_rows, scale, name):
    tm = _tile(min(seq, rows), 256)
    hpb = 4
    tpb = seq // tm
    nx = n_x_rows // tm
    tab_map = lambda i, j: (jnp.where(i < nx, i % tpb, tpb), 0)
    return pl.pallas_call(
        functools.partial(_qk_post_kernel, n_heads=hpb, scale=scale),
        grid=(rows // tm, n_heads_total // hpb),
        in_specs=[pl.BlockSpec((tm, hpb * HEAD_DIM), lambda i, j: (i, col_blk0 + j)),
                  pl.BlockSpec((1, HEAD_DIM), lambda i, j: (0, 0)),
                  pl.BlockSpec((tm, HEAD_DIM), tab_map),
                  pl.BlockSpec((tm, HEAD_DIM), tab_map)],
        out_specs=pl.BlockSpec((tm, hpb * HEAD_DIM), lambda i, j: (i, j)),
        out_shape=jax.ShapeDtypeStruct((rows, n_heads_total * HEAD_DIM), BF16),
        compiler_params=_cparams(("arbitrary", "arbitrary")),
        name=name,
    )(src, w.reshape(1, HEAD_DIM), cos_t, sin_t)


def _rope_tables(seq, tm):
    rows = seq // GRID_W
    row_pos = jnp.repeat(jnp.arange(rows, dtype=F32), GRID_W)
    col_pos = (jnp.arange(seq) % GRID_W).astype(F32)
    inv_freq = ROPE_THETA ** (-jnp.arange(ROPE_AXIS_FREQS, dtype=F32) / ROPE_AXIS_FREQS)
    ar = row_pos[:, None] * inv_freq
    ac = col_pos[:, None] * inv_freq
    cos_t = jnp.concatenate([jnp.cos(ar), jnp.cos(ar), jnp.cos(ac), jnp.cos(ac)], axis=-1)
    sin_t = jnp.concatenate([-jnp.sin(ar), jnp.sin(ar), -jnp.sin(ac), jnp.sin(ac)], axis=-1)
    cos_t = jnp.concatenate([cos_t, jnp.ones((tm, HEAD_DIM), F32)], axis=0)
    sin_t = jnp.concatenate([sin_t, jnp.zeros((tm, HEAD_DIM), F32)], axis=0)
    return cos_t, sin_t


def _attn_kernel(q_ref, wq_ref, cos_ref, sin_ref, kx_ref, kc_ref, vx_ref, vc_ref, o_ref):
    nt = (((1,), (1,)), ((), ()))
    kx, kc = kx_ref[...], kc_ref[...]

    def with_ones(v):
        lane = lax.broadcasted_iota(jnp.int32, v.shape, 1)
        return jnp.concatenate([v, jnp.where(lane == 0, 1.0, 0.0).astype(v.dtype)], axis=1)

    vx, vc = with_ones(vx_ref[...]), with_ones(vc_ref[...])
    tq = q_ref.shape[0]
    rows = tq // ATTN_ROW_SPLIT
    units = [(g, r) for g in range(GQA_GROUP) for r in range(ATTN_ROW_SPLIT)]

    lane = lax.broadcasted_iota(jnp.int32, (rows, HEAD_DIM), 1)
    first = (lane % (HEAD_DIM // 2)) < (HEAD_DIM // 4)

    def scores(u):
        g, r = u
        rs = slice(r * rows, (r + 1) * rows)
        xh = q_ref[rs, g * HEAD_DIM:(g + 1) * HEAD_DIM].astype(F32)
        y = xh * lax.rsqrt(jnp.mean(xh * xh, axis=-1, keepdims=True) + EPS) * wq_ref[...]
        partner = jnp.where(first, pltpu.roll(y, HEAD_DIM - HEAD_DIM // 4, 1), pltpu.roll(y, HEAD_DIM // 4, 1))
        q = (y * cos_ref[rs, :] + partner * sin_ref[rs, :]).astype(BF16)
        return (lax.dot_general(q, kx, nt, preferred_element_type=F32),
                lax.dot_general(q, kc, nt, preferred_element_type=F32))

    nxt = scores(units[0])
    for n, (g, r) in enumerate(units):
        s1, s2 = nxt
        if n + 1 < len(units):
            nxt = scores(units[n + 1])
        m = jnp.maximum(jnp.max(s1, axis=-1, keepdims=True), jnp.max(s2, axis=-1, keepdims=True))
        p1 = jnp.exp2(s1 - m).astype(BF16)
        p2 = jnp.exp2(s2 - m).astype(BF16)
        o = jnp.dot(p1, vx, preferred_element_type=F32) + jnp.dot(p2, vc, preferred_element_type=F32)
        o_ref[r * rows:(r + 1) * rows, g * HEAD_DIM:(g + 1) * HEAD_DIM] = (
            o[:, :HEAD_DIM] / o[:, HEAD_DIM:HEAD_DIM + 1]).astype(o_ref.dtype)


def _attention(p3, wq, cos_t, sin_t, k, p1, batch, seq, ctx_len):
    tq = _tile(seq, ATTN_TQ)
    qpb = seq // tq
    gw = GQA_GROUP * HEAD_DIM
    cblk0 = batch * seq // ctx_len
    return pl.pallas_call(
        _attn_kernel,
        grid=(batch, N_KV_HEADS, qpb),
        in_specs=[pl.BlockSpec((tq, gw), lambda b, h, i: (b * qpb + i, h)),
                  pl.BlockSpec((1, HEAD_DIM), lambda b, h, i: (0, 0)),
                  pl.BlockSpec((tq, HEAD_DIM), lambda b, h, i: (i, 0)),
                  pl.BlockSpec((tq, HEAD_DIM), lambda b, h, i: (i, 0)),
                  pl.BlockSpec((seq, HEAD_DIM), lambda b, h, i: (b, h)),
                  pl.BlockSpec((ctx_len, HEAD_DIM), lambda b, h, i: (cblk0 + b, h)),
                  pl.BlockSpec((seq, HEAD_DIM), lambda b, h, i: (b, N_KV_HEADS + h)),
                  pl.BlockSpec((ctx_len, HEAD_DIM), lambda b, h, i: (cblk0 + b, N_KV_HEADS + h))],
        out_specs=pl.BlockSpec((tq, gw), lambda b, h, i: (b * qpb + i, h)),
        out_shape=jax.ShapeDtypeStruct((batch * seq, N_Q_HEADS * HEAD_DIM), BF16),
        compiler_params=_cparams(("arbitrary", "arbitrary", "arbitrary")),
        name="attention",
    )(p3, wq.reshape(1, HEAD_DIM), cos_t, sin_t, k, k, p1, p1)


def _conv_kernel(prev_ref, cur_ref, next_ref, w_ref, b_ref, o_ref, buf_ref, *, tl, x_tiles, x_tpb, c_tpb):
    i = pl.program_id(0)
    j = jnp.where(i < x_tiles, i % x_tpb, (i - x_tiles) % c_tpb)
    n = jnp.where(i < x_tiles, x_tpb, c_tpb)
    buf_ref[0:HALO, :] = jnp.where(j == 0, 0.0, prev_ref[...].astype(F32))
    buf_ref[HALO:HALO + tl, :] = cur_ref[...].astype(F32)
    buf_ref[HALO + tl:, :] = jnp.where(j == n - 1, 0.0, next_ref[...].astype(F32))
    pad = CONV_W // 2
    xall = buf_ref[...]
    n_rows = xall.shape[0]
    acc = b_ref[...] + w_ref[pad:pad + 1, :] * xall[HALO:HALO + tl]
    for t in range(CONV_W):
        if t != pad:
            acc = acc + w_ref[t:t + 1, :] * pltpu.roll(xall, (pad - t) % n_rows, 0)[HALO:HALO + tl]
    o_ref[...] = _silu(acc).astype(o_ref.dtype)


def _conv_silu(p1, col0, ncols, conv_w, conv_b, seq, ctx_len, n_x_rows):
    rows = p1.shape[0]
    tl = _tile(min(seq, ctx_len), 256)
    tc = _tile(ncols, 2048)
    assert col0 % tc == 0
    cb0 = col0 // tc
    hb = tl // HALO
    last_hb = rows // HALO - 1
    return pl.pallas_call(
        functools.partial(_conv_kernel, tl=tl, x_tiles=n_x_rows // tl, x_tpb=seq // tl, c_tpb=ctx_len // tl),
        grid=(rows // tl, ncols // tc),
        in_specs=[pl.BlockSpec((HALO, tc), lambda i, j: (jnp.maximum(i * hb - 1, 0), cb0 + j)),
                  pl.BlockSpec((tl, tc), lambda i, j: (i, cb0 + j)),
                  pl.BlockSpec((HALO, tc), lambda i, j: (jnp.minimum((i + 1) * hb, last_hb), cb0 + j)),
                  pl.BlockSpec((CONV_W, tc), lambda i, j: (0, j)),
                  pl.BlockSpec((1, tc), lambda i, j: (0, j))],
        out_specs=pl.BlockSpec((tl, tc), lambda i, j: (i, j)),
        out_shape=jax.ShapeDtypeStruct((rows, ncols), BF16),
        scratch_shapes=[pltpu.VMEM((tl + 2 * HALO, tc), F32)],
        compiler_params=_cparams(("arbitrary", "arbitrary")),
        name="conv_silu",
    )(p1, p1, p1, conv_w, conv_b.reshape(1, ncols))


def _dt_prep_kernel(raw_ref, bias_ref, alog_ref, acum_ref, dt_ref, *, n_heads):
    v = raw_ref[...] + bias_ref[...]
    dt = jnp.maximum(v, 0.0) + jnp.log(1.0 + jnp.exp(-jnp.abs(v)))
    dta = dt * (-jnp.exp(alog_ref[...]))
    r = lax.broadcasted_iota(jnp.int32, (CHUNK, CHUNK), 0)
    c = lax.broadcasted_iota(jnp.int32, (CHUNK, CHUNK), 1)
    tril = jnp.where(r >= c, 1.0, 0.0).astype(BF16)
    triu = jnp.where(r <= c, 1.0, 0.0).astype(BF16)
    lane = lax.broadcasted_iota(jnp.int32, dta.shape, 1)
    acum = jnp.where(lane < n_heads, _dot_exact_lhs(tril, dta), _dot_exact_lhs(triu, dta))
    acum_ref[0] = acum.T
    dt_ref[0] = dt.T


def _dt_prep(dt_raw, bias2, alog2, n_heads):
    rows, w = dt_raw.shape
    nch = rows // CHUNK
    out = jax.ShapeDtypeStruct((nch, w, CHUNK), F32)
    return pl.pallas_call(
        functools.partial(_dt_prep_kernel, n_heads=n_heads),
        grid=(nch,),
        in_specs=[pl.BlockSpec((CHUNK, w), lambda i: (i, 0)),
                  pl.BlockSpec((1, w), lambda i: (0, 0)),
                  pl.BlockSpec((1, w), lambda i: (0, 0))],
        out_specs=[pl.BlockSpec((1, w, CHUNK), lambda i: (i, 0, 0))] * 2,
        out_shape=[out, out],
        compiler_params=_cparams(("arbitrary",)),
        name="dt_prep",
    )(dt_raw, bias2, alog2)


def _split3_f32(v):
    hi = v.astype(BF16).astype(F32)
    r1 = v - hi
    mid = r1.astype(BF16).astype(F32)
    lo = (r1 - mid).astype(BF16).astype(F32)
    return [hi, mid, lo]


def _ssd_consts(hpg):
    p = SSD_HEAD_DIM
    gw = hpg * p
    k = np.arange(CHUNK)[:, None]

    def expand(base, width, per):
        col_head = (np.arange(width) // per)[None, :]
        kk = k - base
        return ((kk >= 0) & (kk < 3 * hpg) & (kk % hpg == col_head)).astype(np.float32)

    mats = [expand(0, gw, p), expand(3 * hpg, gw, p), expand(9 * hpg, gw, p), expand(6 * hpg, hpg * CHUNK, CHUNK)]
    return jnp.asarray(np.concatenate(mats, axis=1), dtype=BF16)


def _ssd_kernel(xf_ref, bf_ref, cf_ref, af_ref, df_ref, xb_ref, bb_ref, cb_ref, ab_ref, db_ref, h0_ref, k_ref,
                yf_ref, yb_ref, hfin_ref, h_scr, *, hpg, gps):
    k = pl.program_id(2)
    nck = pl.num_programs(2)
    p = SSD_HEAD_DIM
    gw = hpg * p

    @pl.when(k == 0)
    def _():
        h_scr[...] = h0_ref[0]

    ri = lax.broadcasted_iota(jnp.int32, (CHUNK, CHUNK), 0)
    ci = lax.broadcasted_iota(jnp.int32, (CHUNK, CHUNK), 1)
    low_half = (lax.broadcasted_iota(jnp.int32, (CHUNK, gw), 1) % (2 * p)) < p
    nt = (((1,), (1,)), ((), ()))
    e_exp_a, e_to_end, e_gain = k_ref[:, 0:gw], k_ref[:, gw:2 * gw], k_ref[:, 2 * gw:3 * gw]
    e_col_a = k_ref[:, 3 * gw:]
    pad_rows = jnp.zeros((CHUNK - 12 * hpg, CHUNK), F32)

    dirs = ((xf_ref, bf_ref, cf_ref, af_ref, df_ref, yf_ref), (xb_ref, bb_ref, cb_ref, ab_ref, db_ref, yb_ref))
    for gi, d in [(gi, d) for gi in range(gps) for d in range(2)]:
        x_ref, b_ref, c_ref, a_ref, dt_ref, y_ref = dirs[d]
        gcols = slice(gi * gw, (gi + 1) * gw)
        ncols = slice(gi * SSD_STATE, (gi + 1) * SSD_STATE)
        row_a = a_ref[0, gi * hpg:(gi + 1) * hpg, :]
        row_dt = dt_ref[0, gi * hpg:(gi + 1) * hpg, :]
        if d == 0:
            mask = ri >= ci
            tot = row_a[:, CHUNK - 1:CHUNK]
        else:
            mask = ri <= ci
            tot = row_a[:, 0:1]
        exp_a = jnp.exp(row_a)
        to_end = jnp.exp(tot - row_a) * row_dt
        gain = jnp.broadcast_to(jnp.exp(tot), row_a.shape)
        table = jnp.concatenate(_split3_f32(exp_a) + _split3_f32(to_end) + _split3_f32(row_a)
                                + _split3_f32(gain) + [pad_rows], axis=0)
        tab_t = table.T.astype(BF16)
        exp_a_full = jnp.dot(tab_t, e_exp_a, preferred_element_type=F32)
        to_end_full = jnp.dot(tab_t, e_to_end, preferred_element_type=F32)
        gain_full = jnp.dot(tab_t[0:16], e_gain, preferred_element_type=F32)[0:1]
        col_a = jnp.dot(tab_t, e_col_a, preferred_element_type=F32)
        bc = b_ref[:, ncols]
        cc = c_ref[:, ncols]
        cb = lax.dot_general(cc, bc, nt, preferred_element_type=F32)
        x32 = x_ref[:, gcols].astype(F32)
        x_lo = jnp.where(low_half, x32, 0.0).astype(BF16)
        x_hi = jnp.where(low_half, 0.0, x32).astype(BF16)
        y_parts = []
        for q in range(hpg // 2):
            ws = []
            for r in (2 * q, 2 * q + 1):
                seg = col_a[:, r * CHUNK:(r + 1) * CHUNK] - row_a[r:r + 1, :]
                ws.append((cb * jnp.exp(jnp.where(mask, seg, -jnp.inf)) * row_dt[r:r + 1, :]).astype(BF16))
            sl = slice(q * 2 * p, (q + 1) * 2 * p)
            y_parts.append(jnp.dot(jnp.concatenate(ws, axis=1), jnp.concatenate([x_lo[:, sl], x_hi[:, sl]], axis=0),
                                   preferred_element_type=F32))
        y_state = jnp.dot(cc, h_scr[gi, d].astype(BF16), preferred_element_type=F32)
        y_ref[:, gcols] = (jnp.concatenate(y_parts, axis=1) + y_state * exp_a_full).astype(y_ref.dtype)
        bct = bc.astype(F32).T.astype(BF16)
        upd = jnp.dot(bct, (x32 * to_end_full).astype(BF16), preferred_element_type=F32)
        h_scr[gi, d] = h_scr[gi, d] * gain_full + upd

    @pl.when(k == nck - 1)
    def _():
        hfin_ref[0] = h_scr[...]


def _ssd(xbc, acum_t, dt_t, h0, consts, batch, length, chunk0, hpg, d_inner):
    nck = length // CHUNK
    gw = hpg * SSD_HEAD_DIM
    assert hpg % 2 == 0 and 2 * SSD_HEAD_DIM == LANES and 12 * hpg <= CHUNK
    gps = SSD_GROUPS_PER_STEP
    n_gsteps = SSD_GROUPS // gps
    b_blk0 = d_inner // (gps * SSD_STATE)
    c_blk0 = b_blk0 + n_gsteps
    assert SSD_GROUPS % gps == 0 and d_inner % (gps * SSD_STATE) == 0
    fwd = lambda b, g, k: chunk0 + b * nck + k
    bwd = lambda b, g, k: chunk0 + b * nck + (nck - 1 - k)

    def dir_specs(ch, d):
        return [pl.BlockSpec((CHUNK, gps * gw), lambda b, g, k: (ch(b, g, k), g)),
                pl.BlockSpec((CHUNK, gps * SSD_STATE), lambda b, g, k: (ch(b, g, k), b_blk0 + g)),
                pl.BlockSpec((CHUNK, gps * SSD_STATE), lambda b, g, k: (ch(b, g, k), c_blk0 + g)),
                pl.BlockSpec((1, gps * hpg, CHUNK), lambda b, g, k: (ch(b, g, k), d * n_gsteps + g, 0)),
                pl.BlockSpec((1, gps * hpg, CHUNK), lambda b, g, k: (ch(b, g, k), d * n_gsteps + g, 0))]

    st_spec = pl.BlockSpec((1, gps, 2, SSD_STATE, gw), lambda b, g, k: (b, g, 0, 0, 0))
    y_shape = jax.ShapeDtypeStruct((batch * length, d_inner), BF16)
    return pl.pallas_call(
        functools.partial(_ssd_kernel, hpg=hpg, gps=gps),
        grid=(batch, n_gsteps, nck),
        in_specs=dir_specs(fwd, 0) + dir_specs(bwd, 1) + [st_spec, pl.BlockSpec(consts.shape, lambda b, g, k: (0, 0))],
        out_specs=[pl.BlockSpec((CHUNK, gps * gw), lambda b, g, k: (b * nck + k, g)),
                   pl.BlockSpec((CHUNK, gps * gw), lambda b, g, k: (b * nck + (nck - 1 - k), g)),
                   st_spec],
        out_shape=[y_shape, y_shape, jax.ShapeDtypeStruct(h0.shape, F32)],
        scratch_shapes=[pltpu.VMEM((gps, 2, SSD_STATE, gw), F32)],
        compiler_params=_cparams(("arbitrary", "arbitrary", "arbitrary")),
        name="ssd",
    )(xbc, xbc, xbc, acum_t, dt_t, xbc, xbc, xbc, acum_t, dt_t, h0, consts)


def _ssd_norm_kernel(yf_ref, yb_ref, xs_ref, z_ref, ds_ref, w_ref, o_ref):
    y = yf_ref[...].astype(F32) + yb_ref[...].astype(F32) + ds_ref[...] * xs_ref[...].astype(F32)
    y = y * _silu(z_ref[...].astype(F32))
    ms = jnp.mean(y * y, axis=-1, keepdims=True)
    o_ref[...] = (y * lax.rsqrt(ms + EPS) * w_ref[...]).astype(o_ref.dtype)


def _ssd_norm(yf, yb, xbc, p3, z_blk, dskip_row, w):
    m, d = yf.shape
    tm = _tile(m, 256)
    row = lambda i: (i, 0)
    return pl.pallas_call(
        _ssd_norm_kernel,
        grid=(m // tm,),
        in_specs=[pl.BlockSpec((tm, d), row), pl.BlockSpec((tm, d), row), pl.BlockSpec((tm, d), row),
                  pl.BlockSpec((tm, d), lambda i: (i, z_blk)),
                  pl.BlockSpec((1, d), lambda i: (0, 0)), pl.BlockSpec((1, d), lambda i: (0, 0))],
        out_specs=pl.BlockSpec((tm, d), row),
        out_shape=jax.ShapeDtypeStruct((m, d), BF16),
        compiler_params=_cparams(("arbitrary",)),
        name="ssd_norm",
    )(yf, yb, xbc, p3, dskip_row, w.reshape(1, d))


def _merge(attn, ynorm, w1, w2, p3, g1_col0, g2_col0):
    m, _ = attn.shape
    n = w1.shape[1]
    tm, tn = _tile(m, 512), _tile(n, 1024)
    blk = lambda c0: pl.BlockSpec((tm, tn), lambda j, i: (i, c0 // tn + j))
    part = _mm(attn, w1, 0, n, m, tm, tn, BF16, "merge_attn", _ep_gate, (p3,), (blk(g1_col0),))
    return _mm(ynorm, w2, 0, n, m, tm, tn, BF16, "merge_ssd", _ep_gate_add, (p3, part), (blk(g2_col0), blk(0)))


def _out_proj(merged, w, x2, mod3, seq, gate_row):
    m, _ = merged.shape
    n = w.shape[1]
    tm, tn = _tile(seq, 512), _tile(n, 1024)
    tpb = seq // tm
    return _mm(merged, w, 0, n, m, tm, tn, F32, "out_proj", functools.partial(_ep_residual, gate_row=gate_row),
               (x2, mod3), (pl.BlockSpec((tm, tn), lambda j, i: (i, j)),
                            pl.BlockSpec((1, N_MOD, tn), lambda j, i: (i // tpb, 0, j))))


def _ffn_norm_route_kernel(x_ref, w_ref, m_ref, wr_ref, br_ref, h_ref, route_ref):
    h = _norm_mod(x_ref[...], w_ref[...], m_ref[0, 3:4, :], m_ref[0, 4:5, :])
    h_ref[...] = h
    h_hi = h.astype(BF16)
    h_lo = (h - h_hi.astype(F32)).astype(BF16)
    wr = wr_ref[...]
    w_hi = wr.astype(BF16)
    w_lo = (wr - w_hi.astype(F32)).astype(BF16)
    logits = (jnp.dot(h_hi, w_hi, preferred_element_type=F32) + jnp.dot(h_hi, w_lo, preferred_element_type=F32)
              + jnp.dot(h_lo, w_hi, preferred_element_type=F32)) + br_ref[...]
    lane = lax.broadcasted_iota(jnp.int32, logits.shape, 1).astype(F32)
    big = float(LANES)
    neg = -jnp.inf
    gl = jnp.where(lane < N_GROUPS, logits, neg)
    gmax = jnp.max(gl, axis=-1, keepdims=True)
    gidx = jnp.min(jnp.where(gl == gmax, lane, big), axis=-1, keepdims=True)
    g_prob = 1.0 / jnp.sum(jnp.exp(gl - gmax), axis=-1, keepdims=True)
    lo = N_GROUPS + gidx * EXPERTS_PER_GROUP
    el = jnp.where((lane >= lo) & (lane < lo + EXPERTS_PER_GROUP), logits, neg)
    m1 = jnp.max(el, axis=-1, keepdims=True)
    i1 = jnp.min(jnp.where(el == m1, lane, big), axis=-1, keepdims=True)
    el2 = jnp.where(lane == i1, neg, el)
    m2 = jnp.max(el2, axis=-1, keepdims=True)
    i2 = jnp.min(jnp.where(el2 == m2, lane, big), axis=-1, keepdims=True)
    z = jnp.sum(jnp.exp(el - m1), axis=-1, keepdims=True)
    p1 = 1.0 / z
    p2 = jnp.exp(m2 - m1) / z
    w1 = g_prob * p1 / (p1 + p2)
    w2 = g_prob * p2 / (p1 + p2)
    route = jnp.where(lane == 0, i1 - N_GROUPS, jnp.where(lane == 1, i2 - N_GROUPS,
                      jnp.where(lane == 2, w1, jnp.where(lane == 3, w2, 0.0))))
    route_ref[...] = route


def _ffn_norm_route(x1, w, mod3, wr, br, seq):
    m, d = x1.shape
    tm = _tile(seq, 256)
    tpb = seq // tm
    return pl.pallas_call(
        _ffn_norm_route_kernel,
        grid=(m // tm,),
        in_specs=[pl.BlockSpec((tm, d), lambda i: (i, 0)),
                  pl.BlockSpec((1, d), lambda i: (0, 0)),
                  pl.BlockSpec((1, N_MOD, d), lambda i: (i // tpb, 0, 0)),
                  pl.BlockSpec((d, LANES), lambda i: (0, 0)),
                  pl.BlockSpec((1, LANES), lambda i: (0, 0))],
        out_specs=[pl.BlockSpec((tm, d), lambda i: (i, 0)), pl.BlockSpec((tm, LANES), lambda i: (i, 0))],
        out_shape=[jax.ShapeDtypeStruct((m, d), F32), jax.ShapeDtypeStruct((m, LANES), F32)],
        compiler_params=_cparams(("arbitrary",)),
        name="ffn_norm_route",
    )(x1, w.reshape(1, d), mod3, wr, br)


def _gather_kernel(tok_ref, nv_ref, src_ref, o_ref, buf_ref, sem, *, blk):
    i = pl.program_id(0)
    base = i * blk
    nv = nv_ref[i]

    def issue(r, c):
        pltpu.make_async_copy(src_ref.at[pl.ds(tok_ref[base + r], 1)], buf_ref.at[pl.ds(r, 1)], sem).start()
        return c

    lax.fori_loop(0, nv, issue, 0)

    def zero(r, c):
        buf_ref[pl.ds(r, 1), :] = jnp.zeros((1, buf_ref.shape[1]), buf_ref.dtype)
        return c

    lax.fori_loop(nv, blk, zero, 0)

    def drain(r, c):
        pltpu.make_async_copy(src_ref.at[pl.ds(0, 1)], buf_ref.at[pl.ds(0, 1)], sem).wait()
        return c

    lax.fori_loop(0, nv, drain, 0)
    o_ref[...] = buf_ref[...].astype(o_ref.dtype)


def _gather_rows(row_tok, n_valid, src, blk):
    n_rows = row_tok.shape[0]
    d = src.shape[1]
    return pl.pallas_call(
        functools.partial(_gather_kernel, blk=blk),
        grid_spec=pltpu.PrefetchScalarGridSpec(
            num_scalar_prefetch=2,
            grid=(n_rows // blk,),
            in_specs=[pl.BlockSpec(memory_space=pl.ANY)],
            out_specs=pl.BlockSpec((blk, d), lambda i, tok, nv: (i, 0)),
            scratch_shapes=[pltpu.VMEM((blk, d), src.dtype), pltpu.SemaphoreType.DMA(())]),
        out_shape=jax.ShapeDtypeStruct((n_rows, d), BF16),
        compiler_params=_cparams(("arbitrary",)),
        name="moe_gather",
    )(row_tok, n_valid, src)


def _expert_mm_kernel(ie_ref, it_ref, ib_ref, if_ref, iv_ref, ne_ref, nt_ref, hn_ref, a_ref, *rest, n_w, tn, compute):
    w_hbm = rest[:n_w]
    o_ref = rest[n_w]
    stage = rest[n_w + 1:2 * n_w + 1]
    w_bf = rest[2 * n_w + 1:3 * n_w + 1]
    sem = rest[3 * n_w + 1]
    i = pl.program_id(0)

    def copies(e, t):
        col = pl.multiple_of(t * tn, LANES)
        return [pltpu.make_async_copy(w_hbm[n].at[e, :, pl.ds(col, tn)], stage[n], sem.at[n]) for n in range(n_w)]

    @pl.when(i == 0)
    def _():
        for cp in copies(ie_ref[0], it_ref[0]):
            cp.start()

    @pl.when(if_ref[i] == 1)
    def _():
        for cp in copies(ie_ref[i], it_ref[i]):
            cp.wait()
        for n in range(n_w):
            w_bf[n][...] = stage[n][...].astype(BF16)

        @pl.when(hn_ref[i] == 1)
        def _():
            for cp in copies(ne_ref[i], nt_ref[i]):
                cp.start()

    @pl.when(iv_ref[i] == 1)
    def _():
        o_ref[...] = compute(a_ref[...], [w[...] for w in w_bf]).astype(o_ref.dtype)

    @pl.when(iv_ref[i] == 0)
    def _():
        o_ref[...] = jnp.zeros(o_ref.shape, o_ref.dtype)


def _expert_mm(items, a, weights, blk, tn, compute, out_dtype, name):
    n_items = items[0].shape[0]
    n_rows, k = a.shape
    n = weights[0].shape[2]
    n_w = len(weights)
    imap_a = lambda i, ie, it, ib, *_: (ib[i], 0)
    imap_o = lambda i, ie, it, ib, *_: (ib[i], it[i])
    return pl.pallas_call(
        functools.partial(_expert_mm_kernel, n_w=n_w, tn=tn, compute=compute),
        grid_spec=pltpu.PrefetchScalarGridSpec(
            num_scalar_prefetch=len(items),
            grid=(n_items,),
            in_specs=[pl.BlockSpec((blk, k), imap_a)] + [pl.BlockSpec(memory_space=pl.ANY)] * n_w,
            out_specs=pl.BlockSpec((blk, tn), imap_o),
            scratch_shapes=([pltpu.VMEM((k, tn), F32)] * n_w + [pltpu.VMEM((k, tn), BF16)] * n_w
                            + [pltpu.SemaphoreType.DMA((n_w,))])),
        out_shape=jax.ShapeDtypeStruct((n_rows, n), out_dtype),
        compiler_params=_cparams(("arbitrary",)),
        name=name,
    )(*items, a, *weights)


def _up_compute(xb, ws):
    a = jnp.dot(xb, ws[0], preferred_element_type=F32)
    u = jnp.dot(xb, ws[1], preferred_element_type=F32)
    return _silu(a) * u


def _down_compute(hb, ws):
    return jnp.dot(hb, ws[0], preferred_element_type=F32)


def _combine_kernel(pos_ref, y_ref, x_ref, rt_ref, m_ref, o_ref, buf_ref, sem, *, tm):
    base = pl.program_id(0) * tm

    def issue(r, c):
        for kk in range(TOP_K):
            pltpu.make_async_copy(y_ref.at[pl.ds(pos_ref[TOP_K * (base + r) + kk], 1)],
                                  buf_ref.at[kk, pl.ds(r, 1)], sem).start()
        return c

    lax.fori_loop(0, tm, issue, 0)

    def drain(r, c):
        pltpu.make_async_copy(y_ref.at[pl.ds(0, 1)], buf_ref.at[0, pl.ds(0, 1)], sem).wait()
        return c

    lax.fori_loop(0, TOP_K * tm, drain, 0)
    rt = rt_ref[...]
    moe = rt[:, 2:3] * buf_ref[0] + rt[:, 3:4] * buf_ref[1]
    o_ref[...] = x_ref[...] + m_ref[0, 5:6, :] * moe


def _combine(pos, yexp, x1, route, mod3, seq):
    m, d = x1.shape
    tm = _tile(seq, 128)
    tpb = seq // tm
    return pl.pallas_call(
        functools.partial(_combine_kernel, tm=tm),
        grid_spec=pltpu.PrefetchScalarGridSpec(
            num_scalar_prefetch=1,
            grid=(m // tm,),
            in_specs=[pl.BlockSpec(memory_space=pl.ANY),
                      pl.BlockSpec((tm, d), lambda i, pos: (i, 0)),
                      pl.BlockSpec((tm, LANES), lambda i, pos: (i, 0)),
                      pl.BlockSpec((1, N_MOD, d), lambda i, pos: (i // tpb, 0, 0))],
            out_specs=pl.BlockSpec((tm, d), lambda i, pos: (i, 0)),
            scratch_shapes=[pltpu.VMEM((TOP_K, tm, d), F32), pltpu.SemaphoreType.DMA(())]),
        out_shape=jax.ShapeDtypeStruct((m, d), F32),
        compiler_params=_cparams(("arbitrary",)),
        name="moe_combine",
    )(pos, yexp, x1, route, mod3)


def _moe_plan(route, blk, n_tiles_up, n_tiles_down):
    t = route.shape[0]
    n_assign = t * TOP_K
    nb_max = n_assign // blk + N_EXPERTS
    flat_e = route[:, 0:TOP_K].astype(jnp.int32).reshape(-1)
    onehot = (flat_e[:, None] == jnp.arange(N_EXPERTS, dtype=jnp.int32)[None, :]).astype(jnp.int32)
    cum = jnp.cumsum(onehot, axis=0)
    rank = jnp.take_along_axis(cum, flat_e[:, None], axis=1)[:, 0] - 1
    counts = cum[-1]
    nblk = (counts + blk - 1) // blk
    pend = jnp.cumsum(nblk)
    pstart = pend - nblk
    dest = pstart[flat_e] * blk + rank
    row_tok = jnp.zeros((nb_max * blk,), jnp.int32).at[dest].set(jnp.arange(n_assign, dtype=jnp.int32) // TOP_K)
    total = pend[-1]
    unused = jnp.maximum(nb_max - total, 1)

    def items(n_tiles):
        j = jnp.arange(n_tiles * nb_max, dtype=jnp.int32)
        valid = j < n_tiles * total
        jj = jnp.minimum(j, n_tiles * total - 1)
        e = jnp.minimum(jnp.sum((jj[:, None] >= n_tiles * pend[None, :]).astype(jnp.int32), axis=1), N_EXPERTS - 1)
        local = jj - n_tiles * pstart[e]
        nb_e = jnp.maximum(nblk[e], 1)
        u = j - n_tiles * total
        tile = jnp.where(valid, local // nb_e, u // unused)
        b = jnp.where(valid, pstart[e] + local % nb_e, total + u % unused)
        first = valid & (local % nb_e == 0)
        nxt = j + nb_e
        has_next = first & (nxt < n_tiles * total)
        nxt = jnp.minimum(nxt, n_tiles * nb_max - 1)
        i32 = lambda v: v.astype(jnp.int32)
        return (e, i32(tile), i32(b), i32(first), i32(valid), e[nxt], i32(tile[nxt]), i32(has_next))

    blk_ids = jnp.arange(nb_max, dtype=jnp.int32)
    blk_e = jnp.minimum(jnp.sum((blk_ids[:, None] >= pend[None, :]).astype(jnp.int32), axis=1), N_EXPERTS - 1)
    n_valid = jnp.clip(counts[blk_e] - (blk_ids - pstart[blk_e]) * blk, 0, blk)
    n_valid = jnp.where(blk_ids < total, n_valid, 0).astype(jnp.int32)
    return row_tok, n_valid, dest, items(n_tiles_up), items(n_tiles_down)


def kernel(x, c, ctx, c_ctx, w_ada, b_ada, norm_mix_w, norm_ffn_w, w_in, q_norm_w, k_norm_w, conv_w, conv_b,
           a_log_f, a_log_b, dt_bias_f, dt_bias_b, d_skip, ssd_norm_w, w_attn_proj, w_ssd_proj, w_out,
           w_router_group, b_router_group, w_router_expert, b_router_expert, w_exp_gate, w_exp_up, w_exp_down):
    batch, seq, d = x.shape
    ctx_len = ctx.shape[1]
    assert w_ada.shape[0] == 1, "single layer: the context stream is read, never updated"
    d_inner = d
    n_ssd_heads = d_inner // SSD_HEAD_DIM
    hpg = n_ssd_heads // SSD_GROUPS
    kv_dim = N_KV_HEADS * HEAD_DIM
    q_dim = N_Q_HEADS * HEAD_DIM
    bc_dim = SSD_GROUPS * SSD_STATE
    xbc_dim = d_inner + 2 * bc_dim
    p1_cols = 2 * kv_dim + xbc_dim
    dt_cols = 2 * n_ssd_heads
    p3_col0 = p1_cols + dt_cols
    p3_cols = q_dim + 2 * d + d_inner
    assert dt_cols == LANES
    mx, mc = batch * seq, batch * ctx_len

    x2 = x.reshape(mx, d)
    c2 = ctx.reshape(mc, d)
    w_in0 = w_in[0]

    cond8 = jnp.zeros((8, d), F32).at[0:batch].set(c).at[batch].set(c_ctx)
    assert batch == 2
    mod3 = _adaln(cond8, w_ada[0], b_ada[0]).reshape(8, N_MOD, d)

    h_all = _norm_mix(x2, c2, norm_mix_w[0], mod3, seq)
    tm_all = 1088 if (mx + mc) % 1088 == 0 else _tile(mx + mc, 512)
    p1 = _mm(h_all, w_in0, 0, p1_cols, mx + mc, tm_all, 1024, BF16, "in_proj_kvx")
    dt_raw = _mm(h_all, w_in0, p1_cols, dt_cols, mx + mc, _tile(mx + mc, 512), dt_cols, F32, "in_proj_dt")
    p3 = _mm(h_all, w_in0, p3_col0, p3_cols, mx, _tile(mx, 1024), 1024, BF16, "in_proj_qgz")

    tmq = _tile(min(seq, mc), 256)
    cos_t, sin_t = _rope_tables(seq, tmq)
    k = _qk_post(p1, 0, N_KV_HEADS, mx + mc, k_norm_w[0], cos_t, sin_t, seq, mx, 1.0, "k_post")
    attn = _attention(p3, q_norm_w[0] * (HEAD_DIM ** -0.5 * LOG2_E), cos_t, sin_t, k, p1, batch, seq, ctx_len)

    xbc = _conv_silu(p1, 2 * kv_dim, xbc_dim, conv_w[0], conv_b[0], seq, ctx_len, mx)
    bias2 = jnp.concatenate([dt_bias_f[0], dt_bias_b[0]]).reshape(1, dt_cols)
    alog2 = jnp.concatenate([a_log_f[0], a_log_b[0]]).reshape(1, dt_cols)
    acum_t, dt_t = _dt_prep(dt_raw, bias2, alog2, n_ssd_heads)
    h_zero = jnp.zeros((batch, SSD_GROUPS, 2, SSD_STATE, hpg * SSD_HEAD_DIM), F32)
    ssd_consts = _ssd_consts(hpg)
    _, _, h_ctx = _ssd(xbc, acum_t, dt_t, h_zero, ssd_consts, batch, ctx_len, mx // CHUNK, hpg, d_inner)
    yf, yb, _ = _ssd(xbc, acum_t, dt_t, h_ctx, ssd_consts, batch, seq, 0, hpg, d_inner)
    dskip_row = jnp.repeat(d_skip[0], SSD_HEAD_DIM).reshape(1, d_inner)
    ynorm = _ssd_norm(yf, yb, xbc, p3, (q_dim + 2 * d) // d_inner, dskip_row, ssd_norm_w[0])

    merged = _merge(attn, ynorm, w_attn_proj[0], w_ssd_proj[0], p3, q_dim, q_dim + d)
    x1 = _out_proj(merged, w_out[0], x2, mod3, seq, 2)

    wr = jnp.zeros((d, LANES), F32).at[:, :N_GROUPS].set(w_router_group[0])
    wr = wr.at[:, N_GROUPS:N_GROUPS + N_EXPERTS].set(w_router_expert[0])
    br = jnp.zeros((1, LANES), F32).at[0, :N_GROUPS].set(b_router_group[0])
    br = br.at[0, N_GROUPS:N_GROUPS + N_EXPERTS].set(b_router_expert[0])
    h2, route = _ffn_norm_route(x1, norm_ffn_w[0], mod3, wr, br, seq)
    d_exp = w_exp_gate.shape[-1]
    tn_up, tn_down = _tile(d_exp, 512), _tile(d, 4096)
    row_tok, n_valid, dest, items_up, items_down = _moe_plan(route, MOE_BLK, d_exp // tn_up, d // tn_down)
    xg = _gather_rows(row_tok, n_valid, h2, MOE_BLK)
    hid = _expert_mm(items_up, xg, (w_exp_gate[0], w_exp_up[0]), MOE_BLK, tn_up, _up_compute, BF16, "moe_up")
    yexp = _expert_mm(items_down, hid, (w_exp_down[0],), MOE_BLK, tn_down, _down_compute, F32, "moe_down")
    out = _combine(dest, yexp, x1, route, mod3, seq)
    return out.reshape(batch, seq, d)
```

```python
import functools

import numpy as np
import jax
import jax.numpy as jnp
from jax import lax
from jax.experimental import pallas as pl
from jax.experimental.pallas import tpu as pltpu

F32 = jnp.float32
BF16 = jnp.bfloat16

N_MOD = 6
EPS = 1e-6
GRID_W = 64
N_Q_HEADS = 32
N_KV_HEADS = 8
HEAD_DIM = 128
GQA_GROUP = N_Q_HEADS // N_KV_HEADS
ROPE_THETA = 10000.0
ROPE_AXIS_FREQS = HEAD_DIM // 4
SSD_HEAD_DIM = 64
SSD_GROUPS = 8
SSD_STATE = 128
CONV_W = 5
CHUNK = 128
N_GROUPS = 4
EXPERTS_PER_GROUP = 8
N_EXPERTS = N_GROUPS * EXPERTS_PER_GROUP
TOP_K = 2

LOG2_E = 1.4426950408889634
LANES = 128
HALO = 16
SSD_GROUPS_PER_STEP = 8
MOE_BLK = 256
ATTN_TQ = 512
ATTN_ROW_SPLIT = 4
VMEM_LIMIT = 56 * 1024 * 1024


def _cparams(sem):
    return pltpu.CompilerParams(dimension_semantics=sem, vmem_limit_bytes=VMEM_LIMIT)


def _tile(n, pref):
    t = min(n, pref)
    while n % t:
        t //= 2
    return t


def _silu(v):
    return v * jax.nn.sigmoid(v)


def _pack_pairs(v):
    c = v.shape[1] // 2
    lo = lax.bitcast_convert_type(v[:, :c].astype(BF16).astype(F32), jnp.uint32)
    hi = lax.bitcast_convert_type(v[:, c:].astype(BF16).astype(F32), jnp.uint32)
    return (lo >> 16) | hi


def _unpack_pairs(w):
    lo = lax.bitcast_convert_type(w << 16, F32)
    hi = lax.bitcast_convert_type(w & jnp.uint32(0xFFFF0000), F32)
    return lo, hi


def _split3(v):
    hi = v.astype(BF16)
    r1 = v - hi.astype(F32)
    mid = r1.astype(BF16)
    lo = (r1 - mid.astype(F32)).astype(BF16)
    return hi, mid, lo


def _dot_exact_lhs(m_bf16, v, dims=(((1,), (0,)), ((), ()))):
    hi, mid, lo = _split3(v)
    out = lax.dot_general(m_bf16, hi, dims, preferred_element_type=F32)
    out = out + lax.dot_general(m_bf16, mid, dims, preferred_element_type=F32)
    return out + lax.dot_general(m_bf16, lo, dims, preferred_element_type=F32)


def _adaln_kernel(c_ref, w_ref, b_ref, o_ref):
    s = _silu(c_ref[...]).astype(BF16)
    o_ref[...] = jnp.dot(s, w_ref[...].astype(BF16), preferred_element_type=F32) + b_ref[...]


def _adaln(cond8, w_ada, b_ada):
    d, n = w_ada.shape
    tn = _tile(n, 512)
    return pl.pallas_call(
        _adaln_kernel,
        grid=(n // tn,),
        in_specs=[pl.BlockSpec((8, d), lambda j: (0, 0)),
                  pl.BlockSpec((d, tn), lambda j: (0, j)),
                  pl.BlockSpec((1, tn), lambda j: (0, j))],
        out_specs=pl.BlockSpec((8, tn), lambda j: (0, j)),
        out_shape=jax.ShapeDtypeStruct((8, n), F32),
        compiler_params=_cparams(("arbitrary",)),
        name="adaln",
    )(cond8, w_ada, b_ada.reshape(1, n))


def _norm_mod(xv, w, shift, scale):
    ms = jnp.mean(xv * xv, axis=-1, keepdims=True)
    h = xv * lax.rsqrt(ms + EPS) * w
    return h * (1.0 + scale) + shift


def _norm_mix_kernel(x_ref, c_ref, w_ref, m_ref, o_ref, *, n_x_tiles):
    i = pl.program_id(0)

    def emit(src):
        o_ref[...] = _norm_mod(src[...], w_ref[...], m_ref[0, 0:1, :], m_ref[0, 1:2, :]).astype(o_ref.dtype)

    @pl.when(i < n_x_tiles)
    def _():
        emit(x_ref)

    @pl.when(i >= n_x_tiles)
    def _():
        emit(c_ref)


def _norm_mix(x2, c2, w, mod3, seq):
    mx, d = x2.shape
    mc = c2.shape[0]
    tm = _tile(min(seq, mc), 256)
    nx, nc = mx // tm, mc // tm
    tpb = seq // tm
    return pl.pallas_call(
        functools.partial(_norm_mix_kernel, n_x_tiles=nx),
        grid=(nx + nc,),
        in_specs=[pl.BlockSpec((tm, d), lambda i: (jnp.minimum(i, nx - 1), 0)),
                  pl.BlockSpec((tm, d), lambda i: (jnp.maximum(i - nx, 0), 0)),
                  pl.BlockSpec((1, d), lambda i: (0, 0)),
                  pl.BlockSpec((1, N_MOD, d), lambda i: (jnp.where(i < nx, i // tpb, 2), 0, 0))],
        out_specs=pl.BlockSpec((tm, d), lambda i: (i, 0)),
        out_shape=jax.ShapeDtypeStruct((mx + mc, d), BF16),
        compiler_params=_cparams(("arbitrary",)),
        name="norm_mix",
    )(x2, c2, w.reshape(1, d), mod3)


def _mm_kernel(a_ref, w_hbm, *rest, epilogue, col0, tn, n_col_tiles):
    extra, o_ref, stage_ref, wbf_ref, sem = rest[:-4], rest[-4], rest[-3], rest[-2], rest[-1]
    j, i = pl.program_id(0), pl.program_id(1)

    def w_copy(jj):
        col = pl.multiple_of(col0 + jj * tn, LANES)
        return pltpu.make_async_copy(w_hbm.at[:, pl.ds(col, tn)], stage_ref, sem)

    @pl.when((i == 0) & (j == 0))
    def _():
        w_copy(0).start()

    @pl.when(i == 0)
    def _():
        w_copy(j).wait()
        wbf_ref[...] = stage_ref[...].astype(BF16)

        @pl.when(j + 1 < n_col_tiles)
        def _():
            w_copy(j + 1).start()

    t = jnp.dot(a_ref[...], wbf_ref[...], preferred_element_type=F32)
    o_ref[...] = epilogue(t, *extra).astype(o_ref.dtype)


def _ep_plain(t):
    return t


def _ep_gate(t, g_ref):
    return jax.nn.sigmoid(g_ref[...].astype(F32)) * t


def _ep_gate_add(t, g_ref, prev_ref):
    return prev_ref[...].astype(F32) + jax.nn.sigmoid(g_ref[...].astype(F32)) * t


def _ep_residual(t, x_ref, m_ref, *, gate_row):
    return x_ref[...] + m_ref[0, gate_row:gate_row + 1, :] * t


def _mm(a, w, col0, ncols, m_rows, tm, tn, out_dtype, name, epilogue=_ep_plain, extra=(), extra_specs=()):
    k = a.shape[1]
    assert ncols % tn == 0 and m_rows % tm == 0 and col0 % LANES == 0
    return pl.pallas_call(
        functools.partial(_mm_kernel, epilogue=epilogue, col0=col0, tn=tn, n_col_tiles=ncols // tn),
        grid=(ncols // tn, m_rows // tm),
        in_specs=[pl.BlockSpec((tm, k), lambda j, i: (i, 0)), pl.BlockSpec(memory_space=pl.ANY)] + list(extra_specs),
        out_specs=pl.BlockSpec((tm, tn), lambda j, i: (i, j)),
        out_shape=jax.ShapeDtypeStruct((m_rows, ncols), out_dtype),
        scratch_shapes=[pltpu.VMEM((k, tn), F32), pltpu.VMEM((k, tn), BF16), pltpu.SemaphoreType.DMA(())],
        compiler_params=_cparams(("arbitrary", "arbitrary")),
        name=name,
    )(a, w, *extra)


def _qk_post_kernel(x_ref, w_ref, cos_ref, sin_ref, o_ref, *, n_heads, scale):
    lane = lax.broadcasted_iota(jnp.int32, (x_ref.shape[0], HEAD_DIM), 1)
    first = (lane % (HEAD_DIM // 2)) < (HEAD_DIM // 4)
    for h in range(n_heads):
        sl = slice(h * HEAD_DIM, (h + 1) * HEAD_DIM)
        xh = x_ref[:, sl].astype(F32)
        ms = jnp.mean(xh * xh, axis=-1, keepdims=True)
        y = xh * lax.rsqrt(ms + EPS) * w_ref[...]
        partner = jnp.where(first, pltpu.roll(y, HEAD_DIM - HEAD_DIM // 4, 1), pltpu.roll(y, HEAD_DIM // 4, 1))
        y = y * cos_ref[...] + partner * sin_ref[...]
        o_ref[:, sl] = (y * scale).astype(o_ref.dtype)


def _qk_post(src, col_blk0, n_heads_total, rows, w, cos_t, sin_t, seq, n_x_rows, scale, name):
    tm = _tile(min(seq, rows), 256)
    hpb = 4
    tpb = seq // tm
    nx = n_x_rows // tm
    tab_map = lambda i, j: (jnp.where(i < nx, i % tpb, tpb), 0)
    return pl.pallas_call(
        functools.partial(_qk_post_kernel, n_heads=hpb, scale=scale),
        grid=(rows // tm, n_heads_total // hpb),
        in_specs=[pl.BlockSpec((tm, hpb * HEAD_DIM), lambda i, j: (i, col_blk0 + j)),
                  pl.BlockSpec((1, HEAD_DIM), lambda i, j: (0, 0)),
                  pl.BlockSpec((tm, HEAD_DIM), tab_map),
                  pl.BlockSpec((tm, HEAD_DIM), tab_map)],
        out_specs=pl.BlockSpec((tm, hpb * HEAD_DIM), lambda i, j: (i, j)),
        out_shape=jax.ShapeDtypeStruct((rows, n_heads_total * HEAD_DIM), BF16),
        compiler_params=_cparams(("arbitrary", "arbitrary")),
        name=name,
    )(src, w.reshape(1, HEAD_DIM), cos_t, sin_t)


def _rope_tables(seq, tm):
    rows = seq // GRID_W
    row_pos = jnp.repeat(jnp.arange(rows, dtype=F32), GRID_W)
    col_pos = (jnp.arange(seq) % GRID_W).astype(F32)
    inv_freq = ROPE_THETA ** (-jnp.arange(ROPE_AXIS_FREQS, dtype=F32) / ROPE_AXIS_FREQS)
    ar = row_pos[:, None] * inv_freq
    ac = col_pos[:, None] * inv_freq
    cos_t = jnp.concatenate([jnp.cos(ar), jnp.cos(ar), jnp.cos(ac), jnp.cos(ac)], axis=-1)
    sin_t = jnp.concatenate([-jnp.sin(ar), jnp.sin(ar), -jnp.sin(ac), jnp.sin(ac)], axis=-1)
    cos_t = jnp.concatenate([cos_t, jnp.ones((tm, HEAD_DIM), F32)], axis=0)
    sin_t = jnp.concatenate([sin_t, jnp.zeros((tm, HEAD_DIM), F32)], axis=0)
    return cos_t, sin_t


def _attn_kernel(q_ref, wq_ref, cos_ref, sin_ref, kx_ref, kc_ref, vx_ref, vc_ref, o_ref):
    nt = (((1,), (1,)), ((), ()))
    kx, kc = kx_ref[...], kc_ref[...]

    def with_ones(v):
        lane = lax.broadcasted_iota(jnp.int32, v.shape, 1)
        return jnp.concatenate([v, jnp.where(lane == 0, 1.0, 0.0).astype(v.dtype)], axis=1)

    vx, vc = with_ones(vx_ref[...]), with_ones(vc_ref[...])
    tq = q_ref.shape[0]
    rows = tq // ATTN_ROW_SPLIT
    units = [(g, r) for g in range(GQA_GROUP) for r in range(ATTN_ROW_SPLIT)]

    lane = lax.broadcasted_iota(jnp.int32, (rows, HEAD_DIM), 1)
    first = (lane % (HEAD_DIM // 2)) < (HEAD_DIM // 4)

    def scores(u):
        g, r = u
        rs = slice(r * rows, (r + 1) * rows)
        xh = q_ref[rs, g * HEAD_DIM:(g + 1) * HEAD_DIM].astype(F32)
        y = xh * lax.rsqrt(jnp.mean(xh * xh, axis=-1, keepdims=True) + EPS) * wq_ref[...]
        partner = jnp.where(first, pltpu.roll(y, HEAD_DIM - HEAD_DIM // 4, 1), pltpu.roll(y, HEAD_DIM // 4, 1))
        q = (y * cos_ref[rs, :] + partner * sin_ref[rs, :]).astype(BF16)
        return (lax.dot_general(q, kx, nt, preferred_element_type=F32),
                lax.dot_general(q, kc, nt, preferred_element_type=F32))

    nxt = scores(units[0])
    for n, (g, r) in enumerate(units):
        s1, s2 = nxt
        if n + 1 < len(units):
            nxt = scores(units[n + 1])
        m = jnp.maximum(jnp.max(s1, axis=-1, keepdims=True), jnp.max(s2, axis=-1, keepdims=True))
        p1 = jnp.exp2(s1 - m).astype(BF16)
        p2 = jnp.exp2(s2 - m).astype(BF16)
        o = jnp.dot(p1, vx, preferred_element_type=F32) + jnp.dot(p2, vc, preferred_element_type=F32)
        o_ref[r * rows:(r + 1) * rows, g * HEAD_DIM:(g + 1) * HEAD_DIM] = (
            o[:, :HEAD_DIM] / o[:, HEAD_DIM:HEAD_DIM + 1]).astype(o_ref.dtype)


def _attention(p3, wq, cos_t, sin_t, k, p1, batch, seq, ctx_len):
    tq = _tile(seq, ATTN_TQ)
    qpb = seq // tq
    gw = GQA_GROUP * HEAD_DIM
    cblk0 = batch * seq // ctx_len
    return pl.pallas_call(
        _attn_kernel,
        grid=(batch, N_KV_HEADS, qpb),
        in_specs=[pl.BlockSpec((tq, gw), lambda b, h, i: (b * qpb + i, h)),
                  pl.BlockSpec((1, HEAD_DIM), lambda b, h, i: (0, 0)),
                  pl.BlockSpec((tq, HEAD_DIM), lambda b, h, i: (i, 0)),
                  pl.BlockSpec((tq, HEAD_DIM), lambda b, h, i: (i, 0)),
                  pl.BlockSpec((seq, HEAD_DIM), lambda b, h, i: (b, h)),
                  pl.BlockSpec((ctx_len, HEAD_DIM), lambda b, h, i: (cblk0 + b, h)),
                  pl.BlockSpec((seq, HEAD_DIM), lambda b, h, i: (b, N_KV_HEADS + h)),
                  pl.BlockSpec((ctx_len, HEAD_DIM), lambda b, h, i: (cblk0 + b, N_KV_HEADS + h))],
        out_specs=pl.BlockSpec((tq, gw), lambda b, h, i: (b * qpb + i, h)),
        out_shape=jax.ShapeDtypeStruct((batch * seq, N_Q_HEADS * HEAD_DIM), BF16),
        compiler_params=_cparams(("arbitrary", "arbitrary", "arbitrary")),
        name="attention",
    )(p3, wq.reshape(1, HEAD_DIM), cos_t, sin_t, k, k, p1, p1)


def _conv_kernel(prev_ref, cur_ref, next_ref, w_ref, b_ref, o_ref, buf_ref, *, tl, x_tiles, x_tpb, c_tpb):
    i = pl.program_id(0)
    j = jnp.where(i < x_tiles, i % x_tpb, (i - x_tiles) % c_tpb)
    n = jnp.where(i < x_tiles, x_tpb, c_tpb)
    buf_ref[0:HALO, :] = jnp.where(j == 0, 0.0, prev_ref[...].astype(F32))
    buf_ref[HALO:HALO + tl, :] = cur_ref[...].astype(F32)
    buf_ref[HALO + tl:, :] = jnp.where(j == n - 1, 0.0, next_ref[...].astype(F32))
    pad = CONV_W // 2
    xall = buf_ref[...]
    n_rows = xall.shape[0]
    acc = b_ref[...] + w_ref[pad:pad + 1, :] * xall[HALO:HALO + tl]
    for t in range(CONV_W):
        if t != pad:
            acc = acc + w_ref[t:t + 1, :] * pltpu.roll(xall, (pad - t) % n_rows, 0)[HALO:HALO + tl]
    o_ref[...] = _silu(acc).astype(o_ref.dtype)


def _conv_silu(p1, col0, ncols, conv_w, conv_b, seq, ctx_len, n_x_rows):
    rows = p1.shape[0]
    tl = _tile(min(seq, ctx_len), 256)
    tc = _tile(ncols, 2048)
    assert col0 % tc == 0
    cb0 = col0 // tc
    hb = tl // HALO
    last_hb = rows // HALO - 1
    return pl.pallas_call(
        functools.partial(_conv_kernel, tl=tl, x_tiles=n_x_rows // tl, x_tpb=seq // tl, c_tpb=ctx_len // tl),
        grid=(rows // tl, ncols // tc),
        in_specs=[pl.BlockSpec((HALO, tc), lambda i, j: (jnp.maximum(i * hb - 1, 0), cb0 + j)),
                  pl.BlockSpec((tl, tc), lambda i, j: (i, cb0 + j)),
                  pl.BlockSpec((HALO, tc), lambda i, j: (jnp.minimum((i + 1) * hb, last_hb), cb0 + j)),
                  pl.BlockSpec((CONV_W, tc), lambda i, j: (0, j)),
                  pl.BlockSpec((1, tc), lambda i, j: (0, j))],
        out_specs=pl.BlockSpec((tl, tc), lambda i, j: (i, j)),
        out_shape=jax.ShapeDtypeStruct((rows, ncols), BF16),
        scratch_shapes=[pltpu.VMEM((tl + 2 * HALO, tc), F32)],
        compiler_params=_cparams(("arbitrary", "arbitrary")),
        name="conv_silu",
    )(p1, p1, p1, conv_w, conv_b.reshape(1, ncols))


def _dt_prep_kernel(raw_ref, bias_ref, alog_ref, acum_ref, dt_ref, *, n_heads):
    v = raw_ref[...] + bias_ref[...]
    dt = jnp.maximum(v, 0.0) + jnp.log(1.0 + jnp.exp(-jnp.abs(v)))
    dta = dt * (-jnp.exp(alog_ref[...]))
    r = lax.broadcasted_iota(jnp.int32, (CHUNK, CHUNK), 0)
    c = lax.broadcasted_iota(jnp.int32, (CHUNK, CHUNK), 1)
    tril = jnp.where(r >= c, 1.0, 0.0).astype(BF16)
    triu = jnp.where(r <= c, 1.0, 0.0).astype(BF16)
    lane = lax.broadcasted_iota(jnp.int32, dta.shape, 1)
    acum = jnp.where(lane < n_heads, _dot_exact_lhs(tril, dta), _dot_exact_lhs(triu, dta))
    acum_ref[0] = acum.T
    dt_ref[0] = dt.T


def _dt_prep(dt_raw, bias2, alog2, n_heads):
    rows, w = dt_raw.shape
    nch = rows // CHUNK
    out = jax.ShapeDtypeStruct((nch, w, CHUNK), F32)
    return pl.pallas_call(
        functools.partial(_dt_prep_kernel, n_heads=n_heads),
        grid=(nch,),
        in_specs=[pl.BlockSpec((CHUNK, w), lambda i: (i, 0)),
                  pl.BlockSpec((1, w), lambda i: (0, 0)),
                  pl.BlockSpec((1, w), lambda i: (0, 0))],
        out_specs=[pl.BlockSpec((1, w, CHUNK), lambda i: (i, 0, 0))] * 2,
        out_shape=[out, out],
        compiler_params=_cparams(("arbitrary",)),
        name="dt_prep",
    )(dt_raw, bias2, alog2)


def _split3_f32(v):
    hi = v.astype(BF16).astype(F32)
    r1 = v - hi
    mid = r1.astype(BF16).astype(F32)
    lo = (r1 - mid).astype(BF16).astype(F32)
    return [hi, mid, lo]


def _ssd_consts(hpg):
    p = SSD_HEAD_DIM
    gw = hpg * p
    k = np.arange(CHUNK)[:, None]

    def expand(base, width, per):
        col_head = (np.arange(width) // per)[None, :]
        kk = k - base
        return ((kk >= 0) & (kk < 3 * hpg) & (kk % hpg == col_head)).astype(np.float32)

    mats = [expand(0, gw, p), expand(3 * hpg, gw, p), expand(9 * hpg, gw, p), expand(6 * hpg, hpg * CHUNK, CHUNK)]
    return jnp.asarray(np.concatenate(mats, axis=1), dtype=BF16)


def _ssd_kernel(xf_ref, bf_ref, cf_ref, af_ref, df_ref, xb_ref, bb_ref, cb_ref, ab_ref, db_ref, h0_ref, k_ref,
                yf_ref, yb_ref, hfin_ref, h_scr, *, hpg, gps):
    k = pl.program_id(2)
    nck = pl.num_programs(2)
    p = SSD_HEAD_DIM
    gw = hpg * p

    @pl.when(k == 0)
    def _():
        h_scr[...] = h0_ref[0]

    ri = lax.broadcasted_iota(jnp.int32, (CHUNK, CHUNK), 0)
    ci = lax.broadcasted_iota(jnp.int32, (CHUNK, CHUNK), 1)
    low_half = (lax.broadcasted_iota(jnp.int32, (CHUNK, gw), 1) % (2 * p)) < p
    nt = (((1,), (1,)), ((), ()))
    e_exp_a, e_to_end, e_gain = k_ref[:, 0:gw], k_ref[:, gw:2 * gw], k_ref[:, 2 * gw:3 * gw]
    e_col_a = k_ref[:, 3 * gw:]
    pad_rows = jnp.zeros((CHUNK - 12 * hpg, CHUNK), F32)

    dirs = ((xf_ref, bf_ref, cf_ref, af_ref, df_ref, yf_ref), (xb_ref, bb_ref, cb_ref, ab_ref, db_ref, yb_ref))
    for gi, d in [(gi, d) for gi in range(gps) for d in range(2)]:
        x_ref, b_ref, c_ref, a_ref, dt_ref, y_ref = dirs[d]
        gcols = slice(gi * gw, (gi + 1) * gw)
        ncols = slice(gi * SSD_STATE, (gi + 1) * SSD_STATE)
        row_a = a_ref[0, gi * hpg:(gi + 1) * hpg, :]
        row_dt = dt_ref[0, gi * hpg:(gi + 1) * hpg, :]
        if d == 0:
            mask = ri >= ci
            tot = row_a[:, CHUNK - 1:CHUNK]
        else:
            mask = ri <= ci
            tot = row_a[:, 0:1]
        exp_a = jnp.exp(row_a)
        to_end = jnp.exp(tot - row_a) * row_dt
        gain = jnp.broadcast_to(jnp.exp(tot), row_a.shape)
        table = jnp.concatenate(_split3_f32(exp_a) + _split3_f32(to_end) + _split3_f32(row_a)
                                + _split3_f32(gain) + [pad_rows], axis=0)
        tab_t = table.T.astype(BF16)
        exp_a_full = jnp.dot(tab_t, e_exp_a, preferred_element_type=F32)
        to_end_full = jnp.dot(tab_t, e_to_end, preferred_element_type=F32)
        gain_full = jnp.dot(tab_t[0:16], e_gain, preferred_element_type=F32)[0:1]
        col_a = jnp.dot(tab_t, e_col_a, preferred_element_type=F32)
        bc = b_ref[:, ncols]
        cc = c_ref[:, ncols]
        cb = lax.dot_general(cc, bc, nt, preferred_element_type=F32)
        x32 = x_ref[:, gcols].astype(F32)
        x_lo = jnp.where(low_half, x32, 0.0).astype(BF16)
        x_hi = jnp.where(low_half, 0.0, x32).astype(BF16)
        y_parts = []
        for q in range(hpg // 2):
            ws = []
            for r in (2 * q, 2 * q + 1):
                seg = col_a[:, r * CHUNK:(r + 1) * CHUNK] - row_a[r:r + 1, :]
                ws.append((cb * jnp.exp(jnp.where(mask, seg, -jnp.inf)) * row_dt[r:r + 1, :]).astype(BF16))
            sl = slice(q * 2 * p, (q + 1) * 2 * p)
            y_parts.append(jnp.dot(jnp.concatenate(ws, axis=1), jnp.concatenate([x_lo[:, sl], x_hi[:, sl]], axis=0),
                                   preferred_element_type=F32))
        y_state = jnp.dot(cc, h_scr[gi, d].astype(BF16), preferred_element_type=F32)
        y_ref[:, gcols] = (jnp.concatenate(y_parts, axis=1) + y_state * exp_a_full).astype(y_ref.dtype)
        bct = bc.astype(F32).T.astype(BF16)
        upd = jnp.dot(bct, (x32 * to_end_full).astype(BF16), preferred_element_type=F32)
        h_scr[gi, d] = h_scr[gi, d] * gain_full + upd

    @pl.when(k == nck - 1)
    def _():
        hfin_ref[0] = h_scr[...]


def _ssd(xbc, acum_t, dt_t, h0, consts, batch, length, chunk0, hpg, d_inner):
    nck = length // CHUNK
    gw = hpg * SSD_HEAD_DIM
    assert hpg % 2 == 0 and 2 * SSD_HEAD_DIM == LANES and 12 * hpg <= CHUNK
    gps = SSD_GROUPS_PER_STEP
    n_gsteps = SSD_GROUPS // gps
    b_blk0 = d_inner // (gps * SSD_STATE)
    c_blk0 = b_blk0 + n_gsteps
    assert SSD_GROUPS % gps == 0 and d_inner % (gps * SSD_STATE) == 0
    fwd = lambda b, g, k: chunk0 + b * nck + k
    bwd = lambda b, g, k: chunk0 + b * nck + (nck - 1 - k)

    def dir_specs(ch, d):
        return [pl.BlockSpec((CHUNK, gps * gw), lambda b, g, k: (ch(b, g, k), g)),
                pl.BlockSpec((CHUNK, gps * SSD_STATE), lambda b, g, k: (ch(b, g, k), b_blk0 + g)),
                pl.BlockSpec((CHUNK, gps * SSD_STATE), lambda b, g, k: (ch(b, g, k), c_blk0 + g)),
                pl.BlockSpec((1, gps * hpg, CHUNK), lambda b, g, k: (ch(b, g, k), d * n_gsteps + g, 0)),
                pl.BlockSpec((1, gps * hpg, CHUNK), lambda b, g, k: (ch(b, g, k), d * n_gsteps + g, 0))]

    st_spec = pl.BlockSpec((1, gps, 2, SSD_STATE, gw), lambda b, g, k: (b, g, 0, 0, 0))
    y_shape = jax.ShapeDtypeStruct((batch * length, d_inner), BF16)
    return pl.pallas_call(
        functools.partial(_ssd_kernel, hpg=hpg, gps=gps),
        grid=(batch, n_gsteps, nck),
        in_specs=dir_specs(fwd, 0) + dir_specs(bwd, 1) + [st_spec, pl.BlockSpec(consts.shape, lambda b, g, k: (0, 0))],
        out_specs=[pl.BlockSpec((CHUNK, gps * gw), lambda b, g, k: (b * nck + k, g)),
                   pl.BlockSpec((CHUNK, gps * gw), lambda b, g, k: (b * nck + (nck - 1 - k), g)),
                   st_spec],
        out_shape=[y_shape, y_shape, jax.ShapeDtypeStruct(h0.shape, F32)],
        scratch_shapes=[pltpu.VMEM((gps, 2, SSD_STATE, gw), F32)],
        compiler_params=_cparams(("arbitrary", "arbitrary", "arbitrary")),
        name="ssd",
    )(xbc, xbc, xbc, acum_t, dt_t, xbc, xbc, xbc, acum_t, dt_t, h0, consts)


def _ssd_norm_kernel(yf_ref, yb_ref, xs_ref, z_ref, ds_ref, w_ref, o_ref):
    y = yf_ref[...].astype(F32) + yb_ref[...].astype(F32) + ds_ref[...] * xs_ref[...].astype(F32)
    y = y * _silu(z_ref[...].astype(F32))
    ms = jnp.mean(y * y, axis=-1, keepdims=True)
    o_ref[...] = (y * lax.rsqrt(ms + EPS) * w_ref[...]).astype(o_ref.dtype)


def _ssd_norm(yf, yb, xbc, p3, z_blk, dskip_row, w):
    m, d = yf.shape
    tm = _tile(m, 256)
    row = lambda i: (i, 0)
    return pl.pallas_call(
        _ssd_norm_kernel,
        grid=(m // tm,),
        in_specs=[pl.BlockSpec((tm, d), row), pl.BlockSpec((tm, d), row), pl.BlockSpec((tm, d), row),
                  pl.BlockSpec((tm, d), lambda i: (i, z_blk)),
                  pl.BlockSpec((1, d), lambda i: (0, 0)), pl.BlockSpec((1, d), lambda i: (0, 0))],
        out_specs=pl.BlockSpec((tm, d), row),
        out_shape=jax.ShapeDtypeStruct((m, d), BF16),
        compiler_params=_cparams(("arbitrary",)),
        name="ssd_norm",
    )(yf, yb, xbc, p3, dskip_row, w.reshape(1, d))


def _merge(attn, ynorm, w1, w2, p3, g1_col0, g2_col0):
    m, _ = attn.shape
    n = w1.shape[1]
    tm, tn = _tile(m, 512), _tile(n, 1024)
    blk = lambda c0: pl.BlockSpec((tm, tn), lambda j, i: (i, c0 // tn + j))
    part = _mm(attn, w1, 0, n, m, tm, tn, BF16, "merge_attn", _ep_gate, (p3,), (blk(g1_col0),))
    return _mm(ynorm, w2, 0, n, m, tm, tn, BF16, "merge_ssd", _ep_gate_add, (p3, part), (blk(g2_col0), blk(0)))


def _out_proj(merged, w, x2, mod3, seq, gate_row):
    m, _ = merged.shape
    n = w.shape[1]
    tm, tn = _tile(seq, 512), _tile(n, 1024)
    tpb = seq // tm
    return _mm(merged, w, 0, n, m, tm, tn, F32, "out_proj", functools.partial(_ep_residual, gate_row=gate_row),
               (x2, mod3), (pl.BlockSpec((tm, tn), lambda j, i: (i, j)),
                            pl.BlockSpec((1, N_MOD, tn), lambda j, i: (i // tpb, 0, j))))


def _ffn_norm_route_kernel(x_ref, w_ref, m_ref, wr_ref, br_ref, h_ref, route_ref):
    h = _norm_mod(x_ref[...], w_ref[...], m_ref[0, 3:4, :], m_ref[0, 4:5, :])
    h_ref[...] = _pack_pairs(h)
    h_hi = h.astype(BF16)
    h_lo = (h - h_hi.astype(F32)).astype(BF16)
    wr = wr_ref[...]
    w_hi = wr.astype(BF16)
    w_lo = (wr - w_hi.astype(F32)).astype(BF16)
    logits = (jnp.dot(h_hi, w_hi, preferred_element_type=F32) + jnp.dot(h_hi, w_lo, preferred_element_type=F32)
              + jnp.dot(h_lo, w_hi, preferred_element_type=F32)) + br_ref[...]
    lane = lax.broadcasted_iota(jnp.int32, logits.shape, 1).astype(F32)
    big = float(LANES)
    neg = -jnp.inf
    gl = jnp.where(lane < N_GROUPS, logits, neg)
    gmax = jnp.max(gl, axis=-1, keepdims=True)
    gidx = jnp.min(jnp.where(gl == gmax, lane, big), axis=-1, keepdims=True)
    g_prob = 1.0 / jnp.sum(jnp.exp(gl - gmax), axis=-1, keepdims=True)
    lo = N_GROUPS + gidx * EXPERTS_PER_GROUP
    el = jnp.where((lane >= lo) & (lane < lo + EXPERTS_PER_GROUP), logits, neg)
    m1 = jnp.max(el, axis=-1, keepdims=True)
    i1 = jnp.min(jnp.where(el == m1, lane, big), axis=-1, keepdims=True)
    el2 = jnp.where(lane == i1, neg, el)
    m2 = jnp.max(el2, axis=-1, keepdims=True)
    i2 = jnp.min(jnp.where(el2 == m2, lane, big), axis=-1, keepdims=True)
    z = jnp.sum(jnp.exp(el - m1), axis=-1, keepdims=True)
    p1 = 1.0 / z
    p2 = jnp.exp(m2 - m1) / z
    w1 = g_prob * p1 / (p1 + p2)
    w2 = g_prob * p2 / (p1 + p2)
    route = jnp.where(lane == 0, i1 - N_GROUPS, jnp.where(lane == 1, i2 - N_GROUPS,
                      jnp.where(lane == 2, w1, jnp.where(lane == 3, w2, 0.0))))
    route_ref[...] = route


def _ffn_norm_route(x1, w, mod3, wr, br, seq):
    m, d = x1.shape
    tm = _tile(seq, 256)
    tpb = seq // tm
    return pl.pallas_call(
        _ffn_norm_route_kernel,
        grid=(m // tm,),
        in_specs=[pl.BlockSpec((tm, d), lambda i: (i, 0)),
                  pl.BlockSpec((1, d), lambda i: (0, 0)),
                  pl.BlockSpec((1, N_MOD, d), lambda i: (i // tpb, 0, 0)),
                  pl.BlockSpec((d, LANES), lambda i: (0, 0)),
                  pl.BlockSpec((1, LANES), lambda i: (0, 0))],
        out_specs=[pl.BlockSpec((tm, d // 2), lambda i: (i, 0)), pl.BlockSpec((tm, LANES), lambda i: (i, 0))],
        out_shape=[jax.ShapeDtypeStruct((m, d // 2), jnp.uint32), jax.ShapeDtypeStruct((m, LANES), F32)],
        compiler_params=_cparams(("arbitrary",)),
        name="ffn_norm_route",
    )(x1, w.reshape(1, d), mod3, wr, br)


def _gather_kernel(tok_ref, nv_ref, src_ref, o_ref, buf_ref, sem, *, blk):
    i = pl.program_id(0)
    base = i * blk
    nv = nv_ref[i]

    def issue(r, c):
        pltpu.make_async_copy(src_ref.at[pl.ds(tok_ref[base + r], 1)], buf_ref.at[pl.ds(r, 1)], sem).start()
        return c

    lax.fori_loop(0, nv, issue, 0)

    def zero(r, c):
        buf_ref[pl.ds(r, 1), :] = jnp.zeros((1, buf_ref.shape[1]), buf_ref.dtype)
        return c

    lax.fori_loop(nv, blk, zero, 0)

    def drain(r, c):
        pltpu.make_async_copy(src_ref.at[pl.ds(0, 1)], buf_ref.at[pl.ds(0, 1)], sem).wait()
        return c

    lax.fori_loop(0, nv, drain, 0)
    lo, hi = _unpack_pairs(buf_ref[...])
    half = lo.shape[1]
    o_ref[:, :half] = lo.astype(o_ref.dtype)
    o_ref[:, half:] = hi.astype(o_ref.dtype)


def _gather_rows(row_tok, n_valid, src, blk):
    n_rows = row_tok.shape[0]
    dp = src.shape[1]
    d = 2 * dp
    return pl.pallas_call(
        functools.partial(_gather_kernel, blk=blk),
        grid_spec=pltpu.PrefetchScalarGridSpec(
            num_scalar_prefetch=2,
            grid=(n_rows // blk,),
            in_specs=[pl.BlockSpec(memory_space=pl.ANY)],
            out_specs=pl.BlockSpec((blk, d), lambda i, tok, nv: (i, 0)),
            scratch_shapes=[pltpu.VMEM((blk, dp), src.dtype), pltpu.SemaphoreType.DMA(())]),
        out_shape=jax.ShapeDtypeStruct((n_rows, d), BF16),
        compiler_params=_cparams(("arbitrary",)),
        name="moe_gather",
    )(row_tok, n_valid, src)


def _expert_mm_kernel(ie_ref, it_ref, ib_ref, if_ref, iv_ref, ne_ref, nt_ref, hn_ref, a_ref, *rest, n_w, tn, compute):
    w_hbm = rest[:n_w]
    o_ref = rest[n_w]
    stage = rest[n_w + 1:2 * n_w + 1]
    w_bf = rest[2 * n_w + 1:3 * n_w + 1]
    sem = rest[3 * n_w + 1]
    i = pl.program_id(0)

    def copies(e, t):
        col = pl.multiple_of(t * tn, LANES)
        return [pltpu.make_async_copy(w_hbm[n].at[e, :, pl.ds(col, tn)], stage[n], sem.at[n]) for n in range(n_w)]

    @pl.when(i == 0)
    def _():
        for cp in copies(ie_ref[0], it_ref[0]):
            cp.start()

    @pl.when(if_ref[i] == 1)
    def _():
        for cp in copies(ie_ref[i], it_ref[i]):
            cp.wait()
        for n in range(n_w):
            w_bf[n][...] = stage[n][...].astype(BF16)

        @pl.when(hn_ref[i] == 1)
        def _():
            for cp in copies(ne_ref[i], nt_ref[i]):
                cp.start()

    @pl.when(iv_ref[i] == 1)
    def _():
        o_ref[...] = compute(a_ref[...], [w[...] for w in w_bf]).astype(o_ref.dtype)

    @pl.when(iv_ref[i] == 0)
    def _():
        o_ref[...] = jnp.zeros(o_ref.shape, o_ref.dtype)


def _expert_mm(items, a, weights, blk, tn, compute, out_dtype, name, out_div=1):
    n_items = items[0].shape[0]
    n_rows, k = a.shape
    n = weights[0].shape[2]
    n_w = len(weights)
    assert out_div == 1 or tn == n
    imap_a = lambda i, ie, it, ib, *_: (ib[i], 0)
    imap_o = lambda i, ie, it, ib, *_: (ib[i], it[i])
    return pl.pallas_call(
        functools.partial(_expert_mm_kernel, n_w=n_w, tn=tn, compute=compute),
        grid_spec=pltpu.PrefetchScalarGridSpec(
            num_scalar_prefetch=len(items),
            grid=(n_items,),
            in_specs=[pl.BlockSpec((blk, k), imap_a)] + [pl.BlockSpec(memory_space=pl.ANY)] * n_w,
            out_specs=pl.BlockSpec((blk, tn // out_div), imap_o),
            scratch_shapes=([pltpu.VMEM((k, tn), F32)] * n_w + [pltpu.VMEM((k, tn), BF16)] * n_w
                            + [pltpu.SemaphoreType.DMA((n_w,))])),
        out_shape=jax.ShapeDtypeStruct((n_rows, n // out_div), out_dtype),
        compiler_params=_cparams(("arbitrary",)),
        name=name,
    )(*items, a, *weights)


def _up_compute(xb, ws):
    a = jnp.dot(xb, ws[0], preferred_element_type=F32)
    u = jnp.dot(xb, ws[1], preferred_element_type=F32)
    return _silu(a) * u


def _down_compute(hb, ws):
    return _pack_pairs(jnp.dot(hb, ws[0], preferred_element_type=F32))


def _combine_kernel(pos_ref, y_ref, x_ref, rt_ref, m_ref, o_ref, buf_ref, sem, *, tm):
    base = pl.program_id(0) * tm

    def issue(r, c):
        for kk in range(TOP_K):
            pltpu.make_async_copy(y_ref.at[pl.ds(pos_ref[TOP_K * (base + r) + kk], 1)],
                                  buf_ref.at[kk, pl.ds(r, 1)], sem).start()
        return c

    lax.fori_loop(0, tm, issue, 0)

    def drain(r, c):
        pltpu.make_async_copy(y_ref.at[pl.ds(0, 1)], buf_ref.at[0, pl.ds(0, 1)], sem).wait()
        return c

    lax.fori_loop(0, TOP_K * tm, drain, 0)
    rt = rt_ref[...]
    lo0, hi0 = _unpack_pairs(buf_ref[0])
    lo1, hi1 = _unpack_pairs(buf_ref[1])
    half = lo0.shape[1]
    w0, w1 = rt[:, 2:3], rt[:, 3:4]
    o_ref[:, :half] = x_ref[:, :half] + m_ref[0, 5:6, :half] * (w0 * lo0 + w1 * lo1)
    o_ref[:, half:] = x_ref[:, half:] + m_ref[0, 5:6, half:] * (w0 * hi0 + w1 * hi1)


def _combine(pos, yexp, x1, route, mod3, seq):
    m, d = x1.shape
    tm = _tile(seq, 128)
    tpb = seq // tm
    return pl.pallas_call(
        functools.partial(_combine_kernel, tm=tm),
        grid_spec=pltpu.PrefetchScalarGridSpec(
            num_scalar_prefetch=1,
            grid=(m // tm,),
            in_specs=[pl.BlockSpec(memory_space=pl.ANY),
                      pl.BlockSpec((tm, d), lambda i, pos: (i, 0)),
                      pl.BlockSpec((tm, LANES), lambda i, pos: (i, 0)),
                      pl.BlockSpec((1, N_MOD, d), lambda i, pos: (i // tpb, 0, 0))],
            out_specs=pl.BlockSpec((tm, d), lambda i, pos: (i, 0)),
            scratch_shapes=[pltpu.VMEM((TOP_K, tm, d // 2), jnp.uint32), pltpu.SemaphoreType.DMA(())]),
        out_shape=jax.ShapeDtypeStruct((m, d), F32),
        compiler_params=_cparams(("arbitrary",)),
        name="moe_combine",
    )(pos, yexp, x1, route, mod3)


def _moe_plan(route, blk, n_tiles_up, n_tiles_down):
    t = route.shape[0]
    n_assign = t * TOP_K
    nb_max = n_assign // blk + N_EXPERTS
    flat_e = route[:, 0:TOP_K].astype(jnp.int32).reshape(-1)
    onehot = (flat_e[:, None] == jnp.arange(N_EXPERTS, dtype=jnp.int32)[None, :]).astype(jnp.int32)
    cum = jnp.cumsum(onehot, axis=0)
    rank = jnp.take_along_axis(cum, flat_e[:, None], axis=1)[:, 0] - 1
    counts = cum[-1]
    nblk = (counts + blk - 1) // blk
    pend = jnp.cumsum(nblk)
    pstart = pend - nblk
    dest = pstart[flat_e] * blk + rank
    row_tok = jnp.zeros((nb_max * blk,), jnp.int32).at[dest].set(jnp.arange(n_assign, dtype=jnp.int32) // TOP_K)
    total = pend[-1]
    unused = jnp.maximum(nb_max - total, 1)

    def items(n_tiles):
        j = jnp.arange(n_tiles * nb_max, dtype=jnp.int32)
        valid = j < n_tiles * total
        jj = jnp.minimum(j, n_tiles * total - 1)
        e = jnp.minimum(jnp.sum((jj[:, None] >= n_tiles * pend[None, :]).astype(jnp.int32), axis=1), N_EXPERTS - 1)
        local = jj - n_tiles * pstart[e]
        nb_e = jnp.maximum(nblk[e], 1)
        u = j - n_tiles * total
        tile = jnp.where(valid, local // nb_e, u // unused)
        b = jnp.where(valid, pstart[e] + local % nb_e, total + u % unused)
        first = valid & (local % nb_e == 0)
        nxt = j + nb_e
        has_next = first & (nxt < n_tiles * total)
        nxt = jnp.minimum(nxt, n_tiles * nb_max - 1)
        i32 = lambda v: v.astype(jnp.int32)
        return (e, i32(tile), i32(b), i32(first), i32(valid), e[nxt], i32(tile[nxt]), i32(has_next))

    blk_ids = jnp.arange(nb_max, dtype=jnp.int32)
    blk_e = jnp.minimum(jnp.sum((blk_ids[:, None] >= pend[None, :]).astype(jnp.int32), axis=1), N_EXPERTS - 1)
    n_valid = jnp.clip(counts[blk_e] - (blk_ids - pstart[blk_e]) * blk, 0, blk)
    n_valid = jnp.where(blk_ids < total, n_valid, 0).astype(jnp.int32)
    return row_tok, n_valid, dest, items(n_tiles_up), items(n_tiles_down)


def kernel(x, c, ctx, c_ctx, w_ada, b_ada, norm_mix_w, norm_ffn_w, w_in, q_norm_w, k_norm_w, conv_w, conv_b,
           a_log_f, a_log_b, dt_bias_f, dt_bias_b, d_skip, ssd_norm_w, w_attn_proj, w_ssd_proj, w_out,
           w_router_group, b_router_group, w_router_expert, b_router_expert, w_exp_gate, w_exp_up, w_exp_down):
    batch, seq, d = x.shape
    ctx_len = ctx.shape[1]
    assert w_ada.shape[0] == 1, "single layer: the context stream is read, never updated"
    d_inner = d
    n_ssd_heads = d_inner // SSD_HEAD_DIM
    hpg = n_ssd_heads // SSD_GROUPS
    kv_dim = N_KV_HEADS * HEAD_DIM
    q_dim = N_Q_HEADS * HEAD_DIM
    bc_dim = SSD_GROUPS * SSD_STATE
    xbc_dim = d_inner + 2 * bc_dim
    p1_cols = 2 * kv_dim + xbc_dim
    dt_cols = 2 * n_ssd_heads
    p3_col0 = p1_cols + dt_cols
    p3_cols = q_dim + 2 * d + d_inner
    assert dt_cols == LANES
    mx, mc = batch * seq, batch * ctx_len

    x2 = x.reshape(mx, d)
    c2 = ctx.reshape(mc, d)
    w_in0 = w_in[0]

    cond8 = jnp.zeros((8, d), F32).at[0:batch].set(c).at[batch].set(c_ctx)
    assert batch == 2
    mod3 = _adaln(cond8, w_ada[0], b_ada[0]).reshape(8, N_MOD, d)

    h_all = _norm_mix(x2, c2, norm_mix_w[0], mod3, seq)
    tm_all = 1088 if (mx + mc) % 1088 == 0 else _tile(mx + mc, 512)
    p1 = _mm(h_all, w_in0, 0, p1_cols, mx + mc, tm_all, 1024, BF16, "in_proj_kvx")
    dt_raw = _mm(h_all, w_in0, p1_cols, dt_cols, mx + mc, _tile(mx + mc, 512), dt_cols, F32, "in_proj_dt")
    p3 = _mm(h_all, w_in0, p3_col0, p3_cols, mx, _tile(mx, 1024), 1024, BF16, "in_proj_qgz")

    tmq = _tile(min(seq, mc), 256)
    cos_t, sin_t = _rope_tables(seq, tmq)
    k = _qk_post(p1, 0, N_KV_HEADS, mx + mc, k_norm_w[0], cos_t, sin_t, seq, mx, 1.0, "k_post")
    attn = _attention(p3, q_norm_w[0] * (HEAD_DIM ** -0.5 * LOG2_E), cos_t, sin_t, k, p1, batch, seq, ctx_len)

    xbc = _conv_silu(p1, 2 * kv_dim, xbc_dim, conv_w[0], conv_b[0], seq, ctx_len, mx)
    bias2 = jnp.concatenate([dt_bias_f[0], dt_bias_b[0]]).reshape(1, dt_cols)
    alog2 = jnp.concatenate([a_log_f[0], a_log_b[0]]).reshape(1, dt_cols)
    acum_t, dt_t = _dt_prep(dt_raw, bias2, alog2, n_ssd_heads)
    h_zero = jnp.zeros((batch, SSD_GROUPS, 2, SSD_STATE, hpg * SSD_HEAD_DIM), F32)
    ssd_consts = _ssd_consts(hpg)
    _, _, h_ctx = _ssd(xbc, acum_t, dt_t, h_zero, ssd_consts, batch, ctx_len, mx // CHUNK, hpg, d_inner)
    yf, yb, _ = _ssd(xbc, acum_t, dt_t, h_ctx, ssd_consts, batch, seq, 0, hpg, d_inner)
    dskip_row = jnp.repeat(d_skip[0], SSD_HEAD_DIM).reshape(1, d_inner)
    ynorm = _ssd_norm(yf, yb, xbc, p3, (q_dim + 2 * d) // d_inner, dskip_row, ssd_norm_w[0])

    merged = _merge(attn, ynorm, w_attn_proj[0], w_ssd_proj[0], p3, q_dim, q_dim + d)
    x1 = _out_proj(merged, w_out[0], x2, mod3, seq, 2)

    wr = jnp.zeros((d, LANES), F32).at[:, :N_GROUPS].set(w_router_group[0])
    wr = wr.at[:, N_GROUPS:N_GROUPS + N_EXPERTS].set(w_router_expert[0])
    br = jnp.zeros((1, LANES), F32).at[0, :N_GROUPS].set(b_router_group[0])
    br = br.at[0, N_GROUPS:N_GROUPS + N_EXPERTS].set(b_router_expert[0])
    h2, route = _ffn_norm_route(x1, norm_ffn_w[0], mod3, wr, br, seq)
    d_exp = w_exp_gate.shape[-1]
    tn_up, tn_down = _tile(d_exp, 512), _tile(d, 4096)
    row_tok, n_valid, dest, items_up, items_down = _moe_plan(route, MOE_BLK, d_exp // tn_up, d // tn_down)
    xg = _gather_rows(row_tok, n_valid, h2, MOE_BLK)
    hid = _expert_mm(items_up, xg, (w_exp_gate[0], w_exp_up[0]), MOE_BLK, tn_up, _up_compute, BF16, "moe_up")
    yexp = _expert_mm(items_down, hid, (w_exp_down[0],), MOE_BLK, tn_down, _down_compute, jnp.uint32, "moe_down",
                      out_div=2)
    out = _combine(dest, yexp, x1, route, mod3, seq)
    return out.reshape(batch, seq, d)
```

```python
import functools

import numpy as np
import jax
import jax.numpy as jnp
from jax import lax
from jax.experimental import pallas as pl
from jax.experimental.pallas import tpu as pltpu

F32 = jnp.float32
BF16 = jnp.bfloat16

N_MOD = 6
EPS = 1e-6
GRID_W = 64
N_Q_HEADS = 32
N_KV_HEADS = 8
HEAD_DIM = 128
GQA_GROUP = N_Q_HEADS // N_KV_HEADS
ROPE_THETA = 10000.0
ROPE_AXIS_FREQS = HEAD_DIM // 4
SSD_HEAD_DIM = 64
SSD_GROUPS = 8
SSD_STATE = 128
CONV_W = 5
CHUNK = 128
N_GROUPS = 4
EXPERTS_PER_GROUP = 8
N_EXPERTS = N_GROUPS * EXPERTS_PER_GROUP
TOP_K = 2

LOG2_E = 1.4426950408889634
LANES = 128
HALO = 16
SSD_GROUPS_PER_STEP = 8
MOE_BLK = 256
ATTN_TQ = 512
ATTN_ROW_SPLIT = 4
VMEM_LIMIT = 56 * 1024 * 1024


def _cparams(sem):
    return pltpu.CompilerParams(dimension_semantics=sem, vmem_limit_bytes=VMEM_LIMIT)


def _tile(n, pref):
    t = min(n, pref)
    while n % t:
        t //= 2
    return t


def _silu(v):
    return v * jax.nn.sigmoid(v)


def _pack_pairs(v):
    c = v.shape[1] // 2
    lo = lax.bitcast_convert_type(v[:, :c].astype(BF16).astype(F32), jnp.uint32)
    hi = lax.bitcast_convert_type(v[:, c:].astype(BF16).astype(F32), jnp.uint32)
    return (lo >> 16) | hi


def _unpack_pairs(w):
    lo = lax.bitcast_convert_type(w << 16, F32)
    hi = lax.bitcast_convert_type(w & jnp.uint32(0xFFFF0000), F32)
    return lo, hi


def _split3(v):
    hi = v.astype(BF16)
    r1 = v - hi.astype(F32)
    mid = r1.astype(BF16)
    lo = (r1 - mid.astype(F32)).astype(BF16)
    return hi, mid, lo


def _dot_exact_lhs(m_bf16, v, dims=(((1,), (0,)), ((), ()))):
    hi, mid, lo = _split3(v)
    out = lax.dot_general(m_bf16, hi, dims, preferred_element_type=F32)
    out = out + lax.dot_general(m_bf16, mid, dims, preferred_element_type=F32)
    return out + lax.dot_general(m_bf16, lo, dims, preferred_element_type=F32)


def _adaln_kernel(c_ref, w_ref, b_ref, o_ref):
    s = _silu(c_ref[...]).astype(BF16)
    o_ref[...] = jnp.dot(s, w_ref[...].astype(BF16), preferred_element_type=F32) + b_ref[...]


def _adaln(cond8, w_ada, b_ada):
    d, n = w_ada.shape
    tn = _tile(n, 512)
    return pl.pallas_call(
        _adaln_kernel,
        grid=(n // tn,),
        in_specs=[pl.BlockSpec((8, d), lambda j: (0, 0)),
                  pl.BlockSpec((d, tn), lambda j: (0, j)),
                  pl.BlockSpec((1, tn), lambda j: (0, j))],
        out_specs=pl.BlockSpec((8, tn), lambda j: (0, j)),
        out_shape=jax.ShapeDtypeStruct((8, n), F32),
        compiler_params=_cparams(("arbitrary",)),
        name="adaln",
    )(cond8, w_ada, b_ada.reshape(1, n))


def _norm_mod(xv, w, shift, scale):
    ms = jnp.mean(xv * xv, axis=-1, keepdims=True)
    h = xv * lax.rsqrt(ms + EPS) * w
    return h * (1.0 + scale) + shift


def _norm_mix_kernel(x_ref, c_ref, w_ref, m_ref, o_ref, *, n_x_tiles):
    i = pl.program_id(0)

    def emit(src):
        o_ref[...] = _norm_mod(src[...], w_ref[...], m_ref[0, 0:1, :], m_ref[0, 1:2, :]).astype(o_ref.dtype)

    @pl.when(i < n_x_tiles)
    def _():
        emit(x_ref)

    @pl.when(i >= n_x_tiles)
    def _():
        emit(c_ref)


def _norm_mix(x2, c2, w, mod3, seq):
    mx, d = x2.shape
    mc = c2.shape[0]
    tm = _tile(min(seq, mc), 256)
    nx, nc = mx // tm, mc // tm
    tpb = seq // tm
    return pl.pallas_call(
        functools.partial(_norm_mix_kernel, n_x_tiles=nx),
        grid=(nx + nc,),
        in_specs=[pl.BlockSpec((tm, d), lambda i: (jnp.minimum(i, nx - 1), 0)),
                  pl.BlockSpec((tm, d), lambda i: (jnp.maximum(i - nx, 0), 0)),
                  pl.BlockSpec((1, d), lambda i: (0, 0)),
                  pl.BlockSpec((1, N_MOD, d), lambda i: (jnp.where(i < nx, i // tpb, 2), 0, 0))],
        out_specs=pl.BlockSpec((tm, d), lambda i: (i, 0)),
        out_shape=jax.ShapeDtypeStruct((mx + mc, d), BF16),
        compiler_params=_cparams(("arbitrary",)),
        name="norm_mix",
    )(x2, c2, w.reshape(1, d), mod3)


def _mm_kernel(a_ref, w_hbm, *rest, epilogue, col0, tn, n_col_tiles):
    extra, o_ref, stage_ref, wbf_ref, sem = rest[:-4], rest[-4], rest[-3], rest[-2], rest[-1]
    j, i = pl.program_id(0), pl.program_id(1)

    def w_copy(jj):
        col = pl.multiple_of(col0 + jj * tn, LANES)
        return pltpu.make_async_copy(w_hbm.at[:, pl.ds(col, tn)], stage_ref, sem)

    @pl.when((i == 0) & (j == 0))
    def _():
        w_copy(0).start()

    @pl.when(i == 0)
    def _():
        w_copy(j).wait()
        wbf_ref[...] = stage_ref[...].astype(BF16)

        @pl.when(j + 1 < n_col_tiles)
        def _():
            w_copy(j + 1).start()

    t = jnp.dot(a_ref[...], wbf_ref[...], preferred_element_type=F32)
    o_ref[...] = epilogue(t, *extra).astype(o_ref.dtype)


def _ep_plain(t):
    return t


def _ep_gate(t, g_ref):
    return jax.nn.sigmoid(g_ref[...].astype(F32)) * t


def _ep_gate_add(t, g_ref, prev_ref):
    return prev_ref[...].astype(F32) + jax.nn.sigmoid(g_ref[...].astype(F32)) * t


def _ep_residual(t, x_ref, m_ref, *, gate_row):
    return x_ref[...] + m_ref[0, gate_row:gate_row + 1, :] * t


def _mm(a, w, col0, ncols, m_rows, tm, tn, out_dtype, name, epilogue=_ep_plain, extra=(), extra_specs=()):
    k = a.shape[1]
    assert ncols % tn == 0 and m_rows % tm == 0 and col0 % LANES == 0
    return pl.pallas_call(
        functools.partial(_mm_kernel, epilogue=epilogue, col0=col0, tn=tn, n_col_tiles=ncols // tn),
        grid=(ncols // tn, m_rows // tm),
        in_specs=[pl.BlockSpec((tm, k), lambda j, i: (i, 0)), pl.BlockSpec(memory_space=pl.ANY)] + list(extra_specs),
        out_specs=pl.BlockSpec((tm, tn), lambda j, i: (i, j)),
        out_shape=jax.ShapeDtypeStruct((m_rows, ncols), out_dtype),
        scratch_shapes=[pltpu.VMEM((k, tn), F32), pltpu.VMEM((k, tn), BF16), pltpu.SemaphoreType.DMA(())],
        compiler_params=_cparams(("arbitrary", "arbitrary")),
        name=name,
    )(a, w, *extra)


def _qk_post_kernel(x_ref, w_ref, cos_ref, sin_ref, o_ref, *, n_heads, scale):
    lane = lax.broadcasted_iota(jnp.int32, (x_ref.shape[0], HEAD_DIM), 1)
    first = (lane % (HEAD_DIM // 2)) < (HEAD_DIM // 4)
    for h in range(n_heads):
        sl = slice(h * HEAD_DIM, (h + 1) * HEAD_DIM)
        xh = x_ref[:, sl].astype(F32)
        ms = jnp.mean(xh * xh, axis=-1, keepdims=True)
        y = xh * lax.rsqrt(ms + EPS) * w_ref[...]
        partner = jnp.where(first, pltpu.roll(y, HEAD_DIM - HEAD_DIM // 4, 1), pltpu.roll(y, HEAD_DIM // 4, 1))
        y = y * cos_ref[...] + partner * sin_ref[...]
        o_ref[:, sl] = (y * scale).astype(o_ref.dtype)


def _qk_post(src, col_blk0, n_heads_total, rows, w, cos_t, sin_t, seq, n_x_rows, scale, name):
    tm = _tile(min(seq, rows), 256)
    hpb = 4
    tpb = seq // tm
    nx = n_x_rows // tm
    tab_map = lambda i, j: (jnp.where(i < nx, i % tpb, tpb), 0)
    return pl.pallas_call(
        functools.partial(_qk_post_kernel, n_heads=hpb, scale=scale),
        grid=(rows // tm, n_heads_total // hpb),
        in_specs=[pl.BlockSpec((tm, hpb * HEAD_DIM), lambda i, j: (i, col_blk0 + j)),
                  pl.BlockSpec((1, HEAD_DIM), lambda i, j: (0, 0)),
                  pl.BlockSpec((tm, HEAD_DIM), tab_map),
                  pl.BlockSpec((tm, HEAD_DIM), tab_map)],
        out_specs=pl.BlockSpec((tm, hpb * HEAD_DIM), lambda i, j: (i, j)),
        out_shape=jax.ShapeDtypeStruct((rows, n_heads_total * HEAD_DIM), BF16),
        compiler_params=_cparams(("arbitrary", "arbitrary")),
        name=name,
    )(src, w.reshape(1, HEAD_DIM), cos_t, sin_t)


def _rope_tables(seq, tm):
    rows = seq // GRID_W
    row_pos = jnp.repeat(jnp.arange(rows, dtype=F32), GRID_W)
    col_pos = (jnp.arange(seq) % GRID_W).astype(F32)
    inv_freq = ROPE_THETA ** (-jnp.arange(ROPE_AXIS_FREQS, dtype=F32) / ROPE_AXIS_FREQS)
    ar = row_pos[:, None] * inv_freq
    ac = col_pos[:, None] * inv_freq
    cos_t = jnp.concatenate([jnp.cos(ar), jnp.cos(ar), jnp.cos(ac), jnp.cos(ac)], axis=-1)
    sin_t = jnp.concatenate([-jnp.sin(ar), jnp.sin(ar), -jnp.sin(ac), jnp.sin(ac)], axis=-1)
    cos_t = jnp.concatenate([cos_t, jnp.ones((tm, HEAD_DIM), F32)], axis=0)
    sin_t = jnp.concatenate([sin_t, jnp.zeros((tm, HEAD_DIM), F32)], axis=0)
    return cos_t, sin_t


def _attn_kernel(q_ref, wq_ref, cos_ref, sin_ref, kx_ref, kc_ref, vx_ref, vc_ref, o_ref):
    nt = (((1,), (1,)), ((), ()))
    kx, kc = kx_ref[...], kc_ref[...]

    def with_ones(v):
        lane = lax.broadcasted_iota(jnp.int32, v.shape, 1)
        return jnp.concatenate([v, jnp.where(lane == 0, 1.0, 0.0).astype(v.dtype)], axis=1)

    vx, vc = with_ones(vx_ref[...]), with_ones(vc_ref[...])
    tq = q_ref.shape[0]
    rows = tq // ATTN_ROW_SPLIT
    units = [(g, r) for g in range(GQA_GROUP) for r in range(ATTN_ROW_SPLIT)]

    lane = lax.broadcasted_iota(jnp.int32, (rows, HEAD_DIM), 1)
    first = (lane % (HEAD_DIM // 2)) < (HEAD_DIM // 4)

    def scores(u):
        g, r = u
        rs = slice(r * rows, (r + 1) * rows)
        xh = q_ref[rs, g * HEAD_DIM:(g + 1) * HEAD_DIM].astype(F32)
        y = xh * lax.rsqrt(jnp.mean(xh * xh, axis=-1, keepdims=True) + EPS) * wq_ref[...]
        partner = jnp.where(first, pltpu.roll(y, HEAD_DIM - HEAD_DIM // 4, 1), pltpu.roll(y, HEAD_DIM // 4, 1))
        q = (y * cos_ref[rs, :] + partner * sin_ref[rs, :]).astype(BF16)
        return (lax.dot_general(q, kx, nt, preferred_element_type=F32),
                lax.dot_general(q, kc, nt, preferred_element_type=F32))

    nxt = scores(units[0])
    for n, (g, r) in enumerate(units):
        s1, s2 = nxt
        if n + 1 < len(units):
            nxt = scores(units[n + 1])
        m = jnp.maximum(jnp.max(s1, axis=-1, keepdims=True), jnp.max(s2, axis=-1, keepdims=True))
        p1 = jnp.exp2(s1 - m).astype(BF16)
        p2 = jnp.exp2(s2 - m).astype(BF16)
        o = jnp.dot(p1, vx, preferred_element_type=F32) + jnp.dot(p2, vc, preferred_element_type=F32)
        o_ref[r * rows:(r + 1) * rows, g * HEAD_DIM:(g + 1) * HEAD_DIM] = (
            o[:, :HEAD_DIM] / o[:, HEAD_DIM:HEAD_DIM + 1]).astype(o_ref.dtype)


def _attention(p3, wq, cos_t, sin_t, k, p1, batch, seq, ctx_len):
    tq = _tile(seq, ATTN_TQ)
    qpb = seq // tq
    gw = GQA_GROUP * HEAD_DIM
    cblk0 = batch * seq // ctx_len
    return pl.pallas_call(
        _attn_kernel,
        grid=(batch, N_KV_HEADS, qpb),
        in_specs=[pl.BlockSpec((tq, gw), lambda b, h, i: (b * qpb + i, h)),
                  pl.BlockSpec((1, HEAD_DIM), lambda b, h, i: (0, 0)),
                  pl.BlockSpec((tq, HEAD_DIM), lambda b, h, i: (i, 0)),
                  pl.BlockSpec((tq, HEAD_DIM), lambda b, h, i: (i, 0)),
                  pl.BlockSpec((seq, HEAD_DIM), lambda b, h, i: (b, h)),
                  pl.BlockSpec((ctx_len, HEAD_DIM), lambda b, h, i: (cblk0 + b, h)),
                  pl.BlockSpec((seq, HEAD_DIM), lambda b, h, i: (b, N_KV_HEADS + h)),
                  pl.BlockSpec((ctx_len, HEAD_DIM), lambda b, h, i: (cblk0 + b, N_KV_HEADS + h))],
        out_specs=pl.BlockSpec((tq, gw), lambda b, h, i: (b * qpb + i, h)),
        out_shape=jax.ShapeDtypeStruct((batch * seq, N_Q_HEADS * HEAD_DIM), BF16),
        compiler_params=_cparams(("arbitrary", "arbitrary", "arbitrary")),
        name="attention",
    )(p3, wq.reshape(1, HEAD_DIM), cos_t, sin_t, k, k, p1, p1)


def _conv_kernel(prev_ref, cur_ref, next_ref, w_ref, b_ref, o_ref, buf_ref, *, tl, x_tiles, x_tpb, c_tpb):
    i = pl.program_id(0)
    j = jnp.where(i < x_tiles, i % x_tpb, (i - x_tiles) % c_tpb)
    n = jnp.where(i < x_tiles, x_tpb, c_tpb)
    buf_ref[0:HALO, :] = jnp.where(j == 0, 0.0, prev_ref[...].astype(F32))
    buf_ref[HALO:HALO + tl, :] = cur_ref[...].astype(F32)
    buf_ref[HALO + tl:, :] = jnp.where(j == n - 1, 0.0, next_ref[...].astype(F32))
    pad = CONV_W // 2
    xall = buf_ref[...]
    n_rows = xall.shape[0]
    acc = b_ref[...] + w_ref[pad:pad + 1, :] * xall[HALO:HALO + tl]
    for t in range(CONV_W):
        if t != pad:
            acc = acc + w_ref[t:t + 1, :] * pltpu.roll(xall, (pad - t) % n_rows, 0)[HALO:HALO + tl]
    o_ref[...] = _silu(acc).astype(o_ref.dtype)


def _conv_silu(p1, col0, ncols, conv_w, conv_b, seq, ctx_len, n_x_rows):
    rows = p1.shape[0]
    tl = _tile(min(seq, ctx_len), 256)
    tc = _tile(ncols, 2048)
    assert col0 % tc == 0
    cb0 = col0 // tc
    hb = tl // HALO
    last_hb = rows // HALO - 1
    return pl.pallas_call(
        functools.partial(_conv_kernel, tl=tl, x_tiles=n_x_rows // tl, x_tpb=seq // tl, c_tpb=ctx_len // tl),
        grid=(rows // tl, ncols // tc),
        in_specs=[pl.BlockSpec((HALO, tc), lambda i, j: (jnp.maximum(i * hb - 1, 0), cb0 + j)),
                  pl.BlockSpec((tl, tc), lambda i, j: (i, cb0 + j)),
                  pl.BlockSpec((HALO, tc), lambda i, j: (jnp.minimum((i + 1) * hb, last_hb), cb0 + j)),
                  pl.BlockSpec((CONV_W, tc), lambda i, j: (0, j)),
                  pl.BlockSpec((1, tc), lambda i, j: (0, j))],
        out_specs=pl.BlockSpec((tl, tc), lambda i, j: (i, j)),
        out_shape=jax.ShapeDtypeStruct((rows, ncols), BF16),
        scratch_shapes=[pltpu.VMEM((tl + 2 * HALO, tc), F32)],
        compiler_params=_cparams(("arbitrary", "arbitrary")),
        name="conv_silu",
    )(p1, p1, p1, conv_w, conv_b.reshape(1, ncols))


def _dt_prep_kernel(raw_ref, bias_ref, alog_ref, acum_ref, dt_ref, *, n_heads):
    v = raw_ref[...] + bias_ref[...]
    dt = jnp.maximum(v, 0.0) + jnp.log(1.0 + jnp.exp(-jnp.abs(v)))
    dta = dt * (-jnp.exp(alog_ref[...]))
    r = lax.broadcasted_iota(jnp.int32, (CHUNK, CHUNK), 0)
    c = lax.broadcasted_iota(jnp.int32, (CHUNK, CHUNK), 1)
    tril = jnp.where(r >= c, 1.0, 0.0).astype(BF16)
    triu = jnp.where(r <= c, 1.0, 0.0).astype(BF16)
    lane = lax.broadcasted_iota(jnp.int32, dta.shape, 1)
    acum = jnp.where(lane < n_heads, _dot_exact_lhs(tril, dta), _dot_exact_lhs(triu, dta))
    acum_ref[0] = acum.T
    dt_ref[0] = dt.T


def _dt_prep(dt_raw, bias2, alog2, n_heads):
    rows, w = dt_raw.shape
    nch = rows // CHUNK
    out = jax.ShapeDtypeStruct((nch, w, CHUNK), F32)
    return pl.pallas_call(
        functools.partial(_dt_prep_kernel, n_heads=n_heads),
        grid=(nch,),
        in_specs=[pl.BlockSpec((CHUNK, w), lambda i: (i, 0)),
                  pl.BlockSpec((1, w), lambda i: (0, 0)),
                  pl.BlockSpec((1, w), lambda i: (0, 0))],
        out_specs=[pl.BlockSpec((1, w, CHUNK), lambda i: (i, 0, 0))] * 2,
        out_shape=[out, out],
        compiler_params=_cparams(("arbitrary",)),
        name="dt_prep",
    )(dt_raw, bias2, alog2)


def _split3_f32(v):
    hi = v.astype(BF16).astype(F32)
    r1 = v - hi
    mid = r1.astype(BF16).astype(F32)
    lo = (r1 - mid).astype(BF16).astype(F32)
    return [hi, mid, lo]


def _ssd_consts(hpg):
    p = SSD_HEAD_DIM
    gw = hpg * p
    k = np.arange(CHUNK)[:, None]

    def expand(base, width, per):
        col_head = (np.arange(width) // per)[None, :]
        kk = k - base
        return ((kk >= 0) & (kk < 3 * hpg) & (kk % hpg == col_head)).astype(np.float32)

    mats = [expand(0, gw, p), expand(3 * hpg, gw, p), expand(9 * hpg, gw, p), expand(6 * hpg, hpg * CHUNK, CHUNK)]
    return jnp.asarray(np.concatenate(mats, axis=1), dtype=BF16)


def _ssd_kernel(xf_ref, bf_ref, cf_ref, af_ref, df_ref, xb_ref, bb_ref, cb_ref, ab_ref, db_ref, h0_ref, k_ref,
                yf_ref, yb_ref, hfin_ref, h_scr, *, hpg, gps):
    k = pl.program_id(2)
    nck = pl.num_programs(2)
    p = SSD_HEAD_DIM
    gw = hpg * p

    @pl.when(k == 0)
    def _():
        h_scr[...] = h0_ref[0]

    ri = lax.broadcasted_iota(jnp.int32, (CHUNK, CHUNK), 0)
    ci = lax.broadcasted_iota(jnp.int32, (CHUNK, CHUNK), 1)
    low_half = (lax.broadcasted_iota(jnp.int32, (CHUNK, gw), 1) % (2 * p)) < p
    nt = (((1,), (1,)), ((), ()))
    e_exp_a, e_to_end, e_gain = k_ref[:, 0:gw], k_ref[:, gw:2 * gw], k_ref[:, 2 * gw:3 * gw]
    e_col_a = k_ref[:, 3 * gw:]
    pad_rows = jnp.zeros((CHUNK - 12 * hpg, CHUNK), F32)

    dirs = ((xf_ref, bf_ref, cf_ref, af_ref, df_ref, yf_ref), (xb_ref, bb_ref, cb_ref, ab_ref, db_ref, yb_ref))
    for gi, d in [(gi, d) for gi in range(gps) for d in range(2)]:
        x_ref, b_ref, c_ref, a_ref, dt_ref, y_ref = dirs[d]
        gcols = slice(gi * gw, (gi + 1) * gw)
        ncols = slice(gi * SSD_STATE, (gi + 1) * SSD_STATE)
        row_a = a_ref[0, gi * hpg:(gi + 1) * hpg, :]
        row_dt = dt_ref[0, gi * hpg:(gi + 1) * hpg, :]
        if d == 0:
            mask = ri >= ci
            tot = row_a[:, CHUNK - 1:CHUNK]
        else:
            mask = ri <= ci
            tot = row_a[:, 0:1]
        exp_a = jnp.exp(row_a)
        to_end = jnp.exp(tot - row_a) * row_dt
        gain = jnp.broadcast_to(jnp.exp(tot), row_a.shape)
        table = jnp.concatenate(_split3_f32(exp_a) + _split3_f32(to_end) + _split3_f32(row_a)
                                + _split3_f32(gain) + [pad_rows], axis=0)
        tab_t = table.T.astype(BF16)
        exp_a_full = jnp.dot(tab_t, e_exp_a, preferred_element_type=F32)
        to_end_full = jnp.dot(tab_t, e_to_end, preferred_element_type=F32)
        gain_full = jnp.dot(tab_t[0:16], e_gain, preferred_element_type=F32)[0:1]
        col_a = jnp.dot(tab_t, e_col_a, preferred_element_type=F32)
        bc = b_ref[:, ncols]
        cc = c_ref[:, ncols]
        cb = lax.dot_general(cc, bc, nt, preferred_element_type=F32)
        x32 = x_ref[:, gcols].astype(F32)
        x_lo = jnp.where(low_half, x32, 0.0).astype(BF16)
        x_hi = jnp.where(low_half, 0.0, x32).astype(BF16)
        y_parts = []
        for q in range(hpg // 2):
            ws = []
            for r in (2 * q, 2 * q + 1):
                seg = col_a[:, r * CHUNK:(r + 1) * CHUNK] - row_a[r:r + 1, :]
                ws.append((cb * jnp.exp(jnp.where(mask, seg, -jnp.inf)) * row_dt[r:r + 1, :]).astype(BF16))
            sl = slice(q * 2 * p, (q + 1) * 2 * p)
            y_parts.append(jnp.dot(jnp.concatenate(ws, axis=1), jnp.concatenate([x_lo[:, sl], x_hi[:, sl]], axis=0),
                                   preferred_element_type=F32))
        y_state = jnp.dot(cc, h_scr[gi, d].astype(BF16), preferred_element_type=F32)
        y_ref[:, gcols] = (jnp.concatenate(y_parts, axis=1) + y_state * exp_a_full).astype(y_ref.dtype)
        bct = bc.astype(F32).T.astype(BF16)
        upd = jnp.dot(bct, (x32 * to_end_full).astype(BF16), preferred_element_type=F32)
        h_scr[gi, d] = h_scr[gi, d] * gain_full + upd

    @pl.when(k == nck - 1)
    def _():
        hfin_ref[0] = h_scr[...]


def _ssd(xbc, acum_t, dt_t, h0, consts, batch, length, chunk0, hpg, d_inner):
    nck = length // CHUNK
    gw = hpg * SSD_HEAD_DIM
    assert hpg % 2 == 0 and 2 * SSD_HEAD_DIM == LANES and 12 * hpg <= CHUNK
    gps = SSD_GROUPS_PER_STEP
    n_gsteps = SSD_GROUPS // gps
    b_blk0 = d_inner // (gps * SSD_STATE)
    c_blk0 = b_blk0 + n_gsteps
    assert SSD_GROUPS % gps == 0 and d_inner % (gps * SSD_STATE) == 0
    fwd = lambda b, g, k: chunk0 + b * nck + k
    bwd = lambda b, g, k: chunk0 + b * nck + (nck - 1 - k)

    def dir_specs(ch, d):
        return [pl.BlockSpec((CHUNK, gps * gw), lambda b, g, k: (ch(b, g, k), g)),
                pl.BlockSpec((CHUNK, gps * SSD_STATE), lambda b, g, k: (ch(b, g, k), b_blk0 + g)),
                pl.BlockSpec((CHUNK, gps * SSD_STATE), lambda b, g, k: (ch(b, g, k), c_blk0 + g)),
                pl.BlockSpec((1, gps * hpg, CHUNK), lambda b, g, k: (ch(b, g, k), d * n_gsteps + g, 0)),
                pl.BlockSpec((1, gps * hpg, CHUNK), lambda b, g, k: (ch(b, g, k), d * n_gsteps + g, 0))]

    st_spec = pl.BlockSpec((1, gps, 2, SSD_STATE, gw), lambda b, g, k: (b, g, 0, 0, 0))
    y_shape = jax.ShapeDtypeStruct((batch * length, d_inner), BF16)
    return pl.pallas_call(
        functools.partial(_ssd_kernel, hpg=hpg, gps=gps),
        grid=(batch, n_gsteps, nck),
        in_specs=dir_specs(fwd, 0) + dir_specs(bwd, 1) + [st_spec, pl.BlockSpec(consts.shape, lambda b, g, k: (0, 0))],
        out_specs=[pl.BlockSpec((CHUNK, gps * gw), lambda b, g, k: (b * nck + k, g)),
                   pl.BlockSpec((CHUNK, gps * gw), lambda b, g, k: (b * nck + (nck - 1 - k), g)),
                   st_spec],
        out_shape=[y_shape, y_shape, jax.ShapeDtypeStruct(h0.shape, F32)],
        scratch_shapes=[pltpu.VMEM((gps, 2, SSD_STATE, gw), F32)],
        compiler_params=_cparams(("arbitrary", "arbitrary", "arbitrary")),
        name="ssd",
    )(xbc, xbc, xbc, acum_t, dt_t, xbc, xbc, xbc, acum_t, dt_t, h0, consts)


def _ssd_norm_kernel(yf_ref, yb_ref, xs_ref, z_ref, ds_ref, w_ref, o_ref):
    y = yf_ref[...].astype(F32) + yb_ref[...].astype(F32) + ds_ref[...] * xs_ref[...].astype(F32)
    y = y * _silu(z_ref[...].astype(F32))
    ms = jnp.mean(y * y, axis=-1, keepdims=True)
    o_ref[...] = (y * lax.rsqrt(ms + EPS) * w_ref[...]).astype(o_ref.dtype)


def _ssd_norm(yf, yb, xbc, p3, z_blk, dskip_row, w):
    m, d = yf.shape
    tm = _tile(m, 256)
    row = lambda i: (i, 0)
    return pl.pallas_call(
        _ssd_norm_kernel,
        grid=(m // tm,),
        in_specs=[pl.BlockSpec((tm, d), row), pl.BlockSpec((tm, d), row), pl.BlockSpec((tm, d), row),
                  pl.BlockSpec((tm, d), lambda i: (i, z_blk)),
                  pl.BlockSpec((1, d), lambda i: (0, 0)), pl.BlockSpec((1, d), lambda i: (0, 0))],
        out_specs=pl.BlockSpec((tm, d), row),
        out_shape=jax.ShapeDtypeStruct((m, d), BF16),
        compiler_params=_cparams(("arbitrary",)),
        name="ssd_norm",
    )(yf, yb, xbc, p3, dskip_row, w.reshape(1, d))


def _merge(attn, ynorm, w1, w2, p3, g1_col0, g2_col0):
    m, _ = attn.shape
    n = w1.shape[1]
    tm, tn = _tile(m, 512), _tile(n, 1024)
    blk = lambda c0: pl.BlockSpec((tm, tn), lambda j, i: (i, c0 // tn + j))
    part = _mm(attn, w1, 0, n, m, tm, tn, BF16, "merge_attn", _ep_gate, (p3,), (blk(g1_col0),))
    return _mm(ynorm, w2, 0, n, m, tm, tn, BF16, "merge_ssd", _ep_gate_add, (p3, part), (blk(g2_col0), blk(0)))


def _out_proj(merged, w, x2, mod3, seq, gate_row):
    m, _ = merged.shape
    n = w.shape[1]
    tm, tn = _tile(seq, 512), _tile(n, 1024)
    tpb = seq // tm
    return _mm(merged, w, 0, n, m, tm, tn, F32, "out_proj", functools.partial(_ep_residual, gate_row=gate_row),
               (x2, mod3), (pl.BlockSpec((tm, tn), lambda j, i: (i, j)),
                            pl.BlockSpec((1, N_MOD, tn), lambda j, i: (i // tpb, 0, j))))


def _ffn_norm_route_kernel(x_ref, w_ref, m_ref, wr_ref, br_ref, h_ref, route_ref):
    h = _norm_mod(x_ref[...], w_ref[...], m_ref[0, 3:4, :], m_ref[0, 4:5, :])
    h_ref[...] = _pack_pairs(h)
    h_hi = h.astype(BF16)
    h_lo = (h - h_hi.astype(F32)).astype(BF16)
    wr = wr_ref[...]
    w_hi = wr.astype(BF16)
    w_lo = (wr - w_hi.astype(F32)).astype(BF16)
    logits = (jnp.dot(h_hi, w_hi, preferred_element_type=F32) + jnp.dot(h_hi, w_lo, preferred_element_type=F32)
              + jnp.dot(h_lo, w_hi, preferred_element_type=F32)) + br_ref[...]
    lane = lax.broadcasted_iota(jnp.int32, logits.shape, 1).astype(F32)
    big = float(LANES)
    neg = -jnp.inf
    gl = jnp.where(lane < N_GROUPS, logits, neg)
    gmax = jnp.max(gl, axis=-1, keepdims=True)
    gidx = jnp.min(jnp.where(gl == gmax, lane, big), axis=-1, keepdims=True)
    g_prob = 1.0 / jnp.sum(jnp.exp(gl - gmax), axis=-1, keepdims=True)
    lo = N_GROUPS + gidx * EXPERTS_PER_GROUP
    el = jnp.where((lane >= lo) & (lane < lo + EXPERTS_PER_GROUP), logits, neg)
    m1 = jnp.max(el, axis=-1, keepdims=True)
    i1 = jnp.min(jnp.where(el == m1, lane, big), axis=-1, keepdims=True)
    el2 = jnp.where(lane == i1, neg, el)
    m2 = jnp.max(el2, axis=-1, keepdims=True)
    i2 = jnp.min(jnp.where(el2 == m2, lane, big), axis=-1, keepdims=True)
    z = jnp.sum(jnp.exp(el - m1), axis=-1, keepdims=True)
    p1 = 1.0 / z
    p2 = jnp.exp(m2 - m1) / z
    w1 = g_prob * p1 / (p1 + p2)
    w2 = g_prob * p2 / (p1 + p2)
    route = jnp.where(lane == 0, i1 - N_GROUPS, jnp.where(lane == 1, i2 - N_GROUPS,
                      jnp.where(lane == 2, w1, jnp.where(lane == 3, w2, 0.0))))
    route_ref[...] = route


def _ffn_norm_route(x1, w, mod3, wr, br, seq):
    m, d = x1.shape
    tm = _tile(seq, 256)
    tpb = seq // tm
    return pl.pallas_call(
        _ffn_norm_route_kernel,
        grid=(m // tm,),
        in_specs=[pl.BlockSpec((tm, d), lambda i: (i, 0)),
                  pl.BlockSpec((1, d), lambda i: (0, 0)),
                  pl.BlockSpec((1, N_MOD, d), lambda i: (i // tpb, 0, 0)),
                  pl.BlockSpec((d, LANES), lambda i: (0, 0)),
                  pl.BlockSpec((1, LANES), lambda i: (0, 0))],
        out_specs=[pl.BlockSpec((tm, d // 2), lambda i: (i, 0)), pl.BlockSpec((tm, LANES), lambda i: (i, 0))],
        out_shape=[jax.ShapeDtypeStruct((m, d // 2), jnp.uint32), jax.ShapeDtypeStruct((m, LANES), F32)],
        compiler_params=_cparams(("arbitrary",)),
        name="ffn_norm_route",
    )(x1, w.reshape(1, d), mod3, wr, br)


def _gather_kernel(tok_ref, nv_ref, src_ref, o_ref, buf_ref, sem, *, blk):
    i = pl.program_id(0)
    n = pl.num_programs(0)

    def request(b, slot):
        def issue(r, c):
            pltpu.make_async_copy(src_ref.at[pl.ds(tok_ref[b * blk + r], 1)], buf_ref.at[slot, pl.ds(r, 1)],
                                  sem.at[slot]).start()
            return c

        lax.fori_loop(0, nv_ref[b], issue, 0)

    @pl.when(i == 0)
    def _():
        request(0, 0)

    @pl.when(i + 1 < n)
    def _():
        request(i + 1, (i + 1) % 2)

    slot = i % 2
    nv = nv_ref[i]

    def zero(r, c):
        buf_ref[slot, pl.ds(r, 1), :] = jnp.zeros((1, buf_ref.shape[2]), buf_ref.dtype)
        return c

    lax.fori_loop(nv, blk, zero, 0)

    def drain(r, c):
        pltpu.make_async_copy(src_ref.at[pl.ds(0, 1)], buf_ref.at[slot, pl.ds(0, 1)], sem.at[slot]).wait()
        return c

    lax.fori_loop(0, nv, drain, 0)
    lo, hi = _unpack_pairs(buf_ref[slot])
    half = lo.shape[1]
    o_ref[:, :half] = lo.astype(o_ref.dtype)
    o_ref[:, half:] = hi.astype(o_ref.dtype)


def _gather_rows(row_tok, n_valid, src, blk):
    n_rows = row_tok.shape[0]
    dp = src.shape[1]
    d = 2 * dp
    return pl.pallas_call(
        functools.partial(_gather_kernel, blk=blk),
        grid_spec=pltpu.PrefetchScalarGridSpec(
            num_scalar_prefetch=2,
            grid=(n_rows // blk,),
            in_specs=[pl.BlockSpec(memory_space=pl.ANY)],
            out_specs=pl.BlockSpec((blk, d), lambda i, tok, nv: (i, 0)),
            scratch_shapes=[pltpu.VMEM((2, blk, dp), src.dtype), pltpu.SemaphoreType.DMA((2,))]),
        out_shape=jax.ShapeDtypeStruct((n_rows, d), BF16),
        compiler_params=_cparams(("arbitrary",)),
        name="moe_gather",
    )(row_tok, n_valid, src)


def _expert_mm_kernel(ie_ref, it_ref, ib_ref, if_ref, iv_ref, ne_ref, nt_ref, hn_ref, a_ref, *rest, n_w, tn, compute):
    w_hbm = rest[:n_w]
    o_ref = rest[n_w]
    stage = rest[n_w + 1:2 * n_w + 1]
    w_bf = rest[2 * n_w + 1:3 * n_w + 1]
    sem = rest[3 * n_w + 1]
    i = pl.program_id(0)

    def copies(e, t):
        col = pl.multiple_of(t * tn, LANES)
        return [pltpu.make_async_copy(w_hbm[n].at[e, :, pl.ds(col, tn)], stage[n], sem.at[n]) for n in range(n_w)]

    @pl.when(i == 0)
    def _():
        for cp in copies(ie_ref[0], it_ref[0]):
            cp.start()

    @pl.when(if_ref[i] == 1)
    def _():
        for cp in copies(ie_ref[i], it_ref[i]):
            cp.wait()
        for n in range(n_w):
            w_bf[n][...] = stage[n][...].astype(BF16)

        @pl.when(hn_ref[i] == 1)
        def _():
            for cp in copies(ne_ref[i], nt_ref[i]):
                cp.start()

    @pl.when(iv_ref[i] == 1)
    def _():
        o_ref[...] = compute(a_ref[...], [w[...] for w in w_bf]).astype(o_ref.dtype)

    @pl.when(iv_ref[i] == 0)
    def _():
        o_ref[...] = jnp.zeros(o_ref.shape, o_ref.dtype)


def _expert_mm(items, a, weights, blk, tn, compute, out_dtype, name, out_div=1):
    n_items = items[0].shape[0]
    n_rows, k = a.shape
    n = weights[0].shape[2]
    n_w = len(weights)
    assert out_div == 1 or tn == n
    imap_a = lambda i, ie, it, ib, *_: (ib[i], 0)
    imap_o = lambda i, ie, it, ib, *_: (ib[i], it[i])
    return pl.pallas_call(
        functools.partial(_expert_mm_kernel, n_w=n_w, tn=tn, compute=compute),
        grid_spec=pltpu.PrefetchScalarGridSpec(
            num_scalar_prefetch=len(items),
            grid=(n_items,),
            in_specs=[pl.BlockSpec((blk, k), imap_a)] + [pl.BlockSpec(memory_space=pl.ANY)] * n_w,
            out_specs=pl.BlockSpec((blk, tn // out_div), imap_o),
            scratch_shapes=([pltpu.VMEM((k, tn), F32)] * n_w + [pltpu.VMEM((k, tn), BF16)] * n_w
                            + [pltpu.SemaphoreType.DMA((n_w,))])),
        out_shape=jax.ShapeDtypeStruct((n_rows, n // out_div), out_dtype),
        compiler_params=_cparams(("arbitrary",)),
        name=name,
    )(*items, a, *weights)


def _up_compute(xb, ws):
    a = jnp.dot(xb, ws[0], preferred_element_type=F32)
    u = jnp.dot(xb, ws[1], preferred_element_type=F32)
    return _silu(a) * u


def _down_compute(hb, ws):
    return _pack_pairs(jnp.dot(hb, ws[0], preferred_element_type=F32))


def _combine_kernel(pos_ref, y_ref, x_ref, rt_ref, m_ref, o_ref, buf_ref, sem, *, tm):
    i = pl.program_id(0)
    n = pl.num_programs(0)

    def request(t, slot):
        def issue(r, c):
            for kk in range(TOP_K):
                pltpu.make_async_copy(y_ref.at[pl.ds(pos_ref[TOP_K * (t * tm + r) + kk], 1)],
                                      buf_ref.at[slot, kk, pl.ds(r, 1)], sem.at[slot]).start()
            return c

        lax.fori_loop(0, tm, issue, 0)

    @pl.when(i == 0)
    def _():
        request(0, 0)

    @pl.when(i + 1 < n)
    def _():
        request(i + 1, (i + 1) % 2)

    slot = i % 2

    def drain(r, c):
        pltpu.make_async_copy(y_ref.at[pl.ds(0, 1)], buf_ref.at[slot, 0, pl.ds(0, 1)], sem.at[slot]).wait()
        return c

    lax.fori_loop(0, TOP_K * tm, drain, 0)
    rt = rt_ref[...]
    lo0, hi0 = _unpack_pairs(buf_ref[slot, 0])
    lo1, hi1 = _unpack_pairs(buf_ref[slot, 1])
    half = lo0.shape[1]
    w0, w1 = rt[:, 2:3], rt[:, 3:4]
    o_ref[:, :half] = x_ref[:, :half] + m_ref[0, 5:6, :half] * (w0 * lo0 + w1 * lo1)
    o_ref[:, half:] = x_ref[:, half:] + m_ref[0, 5:6, half:] * (w0 * hi0 + w1 * hi1)


def _combine(pos, yexp, x1, route, mod3, seq):
    m, d = x1.shape
    tm = _tile(seq, 128)
    tpb = seq // tm
    return pl.pallas_call(
        functools.partial(_combine_kernel, tm=tm),
        grid_spec=pltpu.PrefetchScalarGridSpec(
            num_scalar_prefetch=1,
            grid=(m // tm,),
            in_specs=[pl.BlockSpec(memory_space=pl.ANY),
                      pl.BlockSpec((tm, d), lambda i, pos: (i, 0)),
                      pl.BlockSpec((tm, LANES), lambda i, pos: (i, 0)),
                      pl.BlockSpec((1, N_MOD, d), lambda i, pos: (i // tpb, 0, 0))],
            out_specs=pl.BlockSpec((tm, d), lambda i, pos: (i, 0)),
            scratch_shapes=[pltpu.VMEM((2, TOP_K, tm, d // 2), jnp.uint32), pltpu.SemaphoreType.DMA((2,))]),
        out_shape=jax.ShapeDtypeStruct((m, d), F32),
        compiler_params=_cparams(("arbitrary",)),
        name="moe_combine",
    )(pos, yexp, x1, route, mod3)


def _moe_plan(route, blk, n_tiles_up, n_tiles_down):
    t = route.shape[0]
    n_assign = t * TOP_K
    nb_max = n_assign // blk + N_EXPERTS
    flat_e = route[:, 0:TOP_K].astype(jnp.int32).reshape(-1)
    onehot = (flat_e[:, None] == jnp.arange(N_EXPERTS, dtype=jnp.int32)[None, :]).astype(jnp.int32)
    cum = jnp.cumsum(onehot, axis=0)
    rank = jnp.take_along_axis(cum, flat_e[:, None], axis=1)[:, 0] - 1
    counts = cum[-1]
    nblk = (counts + blk - 1) // blk
    pend = jnp.cumsum(nblk)
    pstart = pend - nblk
    dest = pstart[flat_e] * blk + rank
    row_tok = jnp.zeros((nb_max * blk,), jnp.int32).at[dest].set(jnp.arange(n_assign, dtype=jnp.int32) // TOP_K)
    total = pend[-1]
    unused = jnp.maximum(nb_max - total, 1)

    def items(n_tiles):
        j = jnp.arange(n_tiles * nb_max, dtype=jnp.int32)
        valid = j < n_tiles * total
        jj = jnp.minimum(j, n_tiles * total - 1)
        e = jnp.minimum(jnp.sum((jj[:, None] >= n_tiles * pend[None, :]).astype(jnp.int32), axis=1), N_EXPERTS - 1)
        local = jj - n_tiles * pstart[e]
        nb_e = jnp.maximum(nblk[e], 1)
        u = j - n_tiles * total
        tile = jnp.where(valid, local // nb_e, u // unused)
        b = jnp.where(valid, pstart[e] + local % nb_e, total + u % unused)
        first = valid & (local % nb_e == 0)
        nxt = j + nb_e
        has_next = first & (nxt < n_tiles * total)
        nxt = jnp.minimum(nxt, n_tiles * nb_max - 1)
        i32 = lambda v: v.astype(jnp.int32)
        return (e, i32(tile), i32(b), i32(first), i32(valid), e[nxt], i32(tile[nxt]), i32(has_next))

    blk_ids = jnp.arange(nb_max, dtype=jnp.int32)
    blk_e = jnp.minimum(jnp.sum((blk_ids[:, None] >= pend[None, :]).astype(jnp.int32), axis=1), N_EXPERTS - 1)
    n_valid = jnp.clip(counts[blk_e] - (blk_ids - pstart[blk_e]) * blk, 0, blk)
    n_valid = jnp.where(blk_ids < total, n_valid, 0).astype(jnp.int32)
    return row_tok, n_valid, dest, items(n_tiles_up), items(n_tiles_down)


def kernel(x, c, ctx, c_ctx, w_ada, b_ada, norm_mix_w, norm_ffn_w, w_in, q_norm_w, k_norm_w, conv_w, conv_b,
           a_log_f, a_log_b, dt_bias_f, dt_bias_b, d_skip, ssd_norm_w, w_attn_proj, w_ssd_proj, w_out,
           w_router_group, b_router_group, w_router_expert, b_router_expert, w_exp_gate, w_exp_up, w_exp_down):
    batch, seq, d = x.shape
    ctx_len = ctx.shape[1]
    assert w_ada.shape[0] == 1, "single layer: the context stream is read, never updated"
    d_inner = d
    n_ssd_heads = d_inner // SSD_HEAD_DIM
    hpg = n_ssd_heads // SSD_GROUPS
    kv_dim = N_KV_HEADS * HEAD_DIM
    q_dim = N_Q_HEADS * HEAD_DIM
    bc_dim = SSD_GROUPS * SSD_STATE
    xbc_dim = d_inner + 2 * bc_dim
    p1_cols = 2 * kv_dim + xbc_dim
    dt_cols = 2 * n_ssd_heads
    p3_col0 = p1_cols + dt_cols
    p3_cols = q_dim + 2 * d + d_inner
    assert dt_cols == LANES
    mx, mc = batch * seq, batch * ctx_len

    x2 = x.reshape(mx, d)
    c2 = ctx.reshape(mc, d)
    w_in0 = w_in[0]

    cond8 = jnp.zeros((8, d), F32).at[0:batch].set(c).at[batch].set(c_ctx)
    assert batch == 2
    mod3 = _adaln(cond8, w_ada[0], b_ada[0]).reshape(8, N_MOD, d)

    h_all = _norm_mix(x2, c2, norm_mix_w[0], mod3, seq)
    tm_all = 1088 if (mx + mc) % 1088 == 0 else _tile(mx + mc, 512)
    p1 = _mm(h_all, w_in0, 0, p1_cols, mx + mc, tm_all, 1024, BF16, "in_proj_kvx")
    dt_raw = _mm(h_all, w_in0, p1_cols, dt_cols, mx + mc, _tile(mx + mc, 512), dt_cols, F32, "in_proj_dt")
    p3 = _mm(h_all, w_in0, p3_col0, p3_cols, mx, _tile(mx, 1024), 1024, BF16, "in_proj_qgz")

    tmq = _tile(min(seq, mc), 256)
    cos_t, sin_t = _rope_tables(seq, tmq)
    k = _qk_post(p1, 0, N_KV_HEADS, mx + mc, k_norm_w[0], cos_t, sin_t, seq, mx, 1.0, "k_post")
    attn = _attention(p3, q_norm_w[0] * (HEAD_DIM ** -0.5 * LOG2_E), cos_t, sin_t, k, p1, batch, seq, ctx_len)

    xbc = _conv_silu(p1, 2 * kv_dim, xbc_dim, conv_w[0], conv_b[0], seq, ctx_len, mx)
    bias2 = jnp.concatenate([dt_bias_f[0], dt_bias_b[0]]).reshape(1, dt_cols)
    alog2 = jnp.concatenate([a_log_f[0], a_log_b[0]]).reshape(1, dt_cols)
    acum_t, dt_t = _dt_prep(dt_raw, bias2, alog2, n_ssd_heads)
    h_zero = jnp.zeros((batch, SSD_GROUPS, 2, SSD_STATE, hpg * SSD_HEAD_DIM), F32)
    ssd_consts = _ssd_consts(hpg)
    _, _, h_ctx = _ssd(xbc, acum_t, dt_t, h_zero, ssd_consts, batch, ctx_len, mx // CHUNK, hpg, d_inner)
    yf, yb, _ = _ssd(xbc, acum_t, dt_t, h_ctx, ssd_consts, batch, seq, 0, hpg, d_inner)
    dskip_row = jnp.repeat(d_skip[0], SSD_HEAD_DIM).reshape(1, d_inner)
    ynorm = _ssd_norm(yf, yb, xbc, p3, (q_dim + 2 * d) // d_inner, dskip_row, ssd_norm_w[0])

    merged = _merge(attn, ynorm, w_attn_proj[0], w_ssd_proj[0], p3, q_dim, q_dim + d)
    x1 = _out_proj(merged, w_out[0], x2, mod3, seq, 2)

    wr = jnp.zeros((d, LANES), F32).at[:, :N_GROUPS].set(w_router_group[0])
    wr = wr.at[:, N_GROUPS:N_GROUPS + N_EXPERTS].set(w_router_expert[0])
    br = jnp.zeros((1, LANES), F32).at[0, :N_GROUPS].set(b_router_group[0])
    br = br.at[0, N_GROUPS:N_GROUPS + N_EXPERTS].set(b_router_expert[0])
    h2, route = _ffn_norm_route(x1, norm_ffn_w[0], mod3, wr, br, seq)
    d_exp = w_exp_gate.shape[-1]
    tn_up, tn_down = _tile(d_exp, 512), _tile(d, 4096)
    row_tok, n_valid, dest, items_up, items_down = _moe_plan(route, MOE_BLK, d_exp // tn_up, d // tn_down)
    xg = _gather_rows(row_tok, n_valid, h2, MOE_BLK)
    hid = _expert_mm(items_up, xg, (w_exp_gate[0], w_exp_up[0]), MOE_BLK, tn_up, _up_compute, BF16, "moe_up")
    yexp = _expert_mm(items_down, hid, (w_exp_down[0],), MOE_BLK, tn_down, _down_compute, jnp.uint32, "moe_down",
                      out_div=2)
    out = _combine(dest, yexp, x1, route, mod3, seq)
    return out.reshape(batch, seq, d)
```

```python
import functools

import numpy as np
import jax
import jax.numpy as jnp
from jax import lax
from jax.experimental import pallas as pl
from jax.experimental.pallas import tpu as pltpu

F32 = jnp.float32
BF16 = jnp.bfloat16

N_MOD = 6
EPS = 1e-6
GRID_W = 64
N_Q_HEADS = 32
N_KV_HEADS = 8
HEAD_DIM = 128
GQA_GROUP = N_Q_HEADS // N_KV_HEADS
ROPE_THETA = 10000.0
ROPE_AXIS_FREQS = HEAD_DIM // 4
SSD_HEAD_DIM = 64
SSD_GROUPS = 8
SSD_STATE = 128
CONV_W = 5
CHUNK = 128
N_GROUPS = 4
EXPERTS_PER_GROUP = 8
N_EXPERTS = N_GROUPS * EXPERTS_PER_GROUP
TOP_K = 2

LOG2_E = 1.4426950408889634
LANES = 128
HALO = 16
SSD_GROUPS_PER_STEP = 8
MOE_BLK = 256
ROW_GROUP = 8
ATTN_TQ = 512
ATTN_ROW_SPLIT = 4
VMEM_LIMIT = 56 * 1024 * 1024


def _cparams(sem):
    return pltpu.CompilerParams(dimension_semantics=sem, vmem_limit_bytes=VMEM_LIMIT)


def _tile(n, pref):
    t = min(n, pref)
    while n % t:
        t //= 2
    return t


def _silu(v):
    return v * jax.nn.sigmoid(v)


def _pack_pairs(v):
    c = v.shape[1] // 2
    lo = lax.bitcast_convert_type(v[:, :c].astype(BF16).astype(F32), jnp.uint32)
    hi = lax.bitcast_convert_type(v[:, c:].astype(BF16).astype(F32), jnp.uint32)
    return (lo >> 16) | hi


def _unpack_pairs(w):
    lo = lax.bitcast_convert_type(w << 16, F32)
    hi = lax.bitcast_convert_type(w & jnp.uint32(0xFFFF0000), F32)
    return lo, hi


def _split3(v):
    hi = v.astype(BF16)
    r1 = v - hi.astype(F32)
    mid = r1.astype(BF16)
    lo = (r1 - mid.astype(F32)).astype(BF16)
    return hi, mid, lo


def _dot_exact_lhs(m_bf16, v, dims=(((1,), (0,)), ((), ()))):
    hi, mid, lo = _split3(v)
    out = lax.dot_general(m_bf16, hi, dims, preferred_element_type=F32)
    out = out + lax.dot_general(m_bf16, mid, dims, preferred_element_type=F32)
    return out + lax.dot_general(m_bf16, lo, dims, preferred_element_type=F32)


def _adaln_kernel(c_ref, w_ref, b_ref, o_ref):
    s = _silu(c_ref[...]).astype(BF16)
    o_ref[...] = jnp.dot(s, w_ref[...].astype(BF16), preferred_element_type=F32) + b_ref[...]


def _adaln(cond8, w_ada, b_ada):
    d, n = w_ada.shape
    tn = _tile(n, 512)
    return pl.pallas_call(
        _adaln_kernel,
        grid=(n // tn,),
        in_specs=[pl.BlockSpec((8, d), lambda j: (0, 0)),
                  pl.BlockSpec((d, tn), lambda j: (0, j)),
                  pl.BlockSpec((1, tn), lambda j: (0, j))],
        out_specs=pl.BlockSpec((8, tn), lambda j: (0, j)),
        out_shape=jax.ShapeDtypeStruct((8, n), F32),
        compiler_params=_cparams(("arbitrary",)),
        name="adaln",
    )(cond8, w_ada, b_ada.reshape(1, n))


def _norm_mod(xv, w, shift, scale):
    ms = jnp.mean(xv * xv, axis=-1, keepdims=True)
    h = xv * lax.rsqrt(ms + EPS) * w
    return h * (1.0 + scale) + shift


def _norm_mix_kernel(x_ref, c_ref, w_ref, m_ref, o_ref, *, n_x_tiles):
    i = pl.program_id(0)

    def emit(src):
        o_ref[...] = _norm_mod(src[...], w_ref[...], m_ref[0, 0:1, :], m_ref[0, 1:2, :]).astype(o_ref.dtype)

    @pl.when(i < n_x_tiles)
    def _():
        emit(x_ref)

    @pl.when(i >= n_x_tiles)
    def _():
        emit(c_ref)


def _norm_mix(x2, c2, w, mod3, seq):
    mx, d = x2.shape
    mc = c2.shape[0]
    tm = _tile(min(seq, mc), 256)
    nx, nc = mx // tm, mc // tm
    tpb = seq // tm
    return pl.pallas_call(
        functools.partial(_norm_mix_kernel, n_x_tiles=nx),
        grid=(nx + nc,),
        in_specs=[pl.BlockSpec((tm, d), lambda i: (jnp.minimum(i, nx - 1), 0)),
                  pl.BlockSpec((tm, d), lambda i: (jnp.maximum(i - nx, 0), 0)),
                  pl.BlockSpec((1, d), lambda i: (0, 0)),
                  pl.BlockSpec((1, N_MOD, d), lambda i: (jnp.where(i < nx, i // tpb, 2), 0, 0))],
        out_specs=pl.BlockSpec((tm, d), lambda i: (i, 0)),
        out_shape=jax.ShapeDtypeStruct((mx + mc, d), BF16),
        compiler_params=_cparams(("arbitrary",)),
        name="norm_mix",
    )(x2, c2, w.reshape(1, d), mod3)


def _mm_kernel(a_ref, w_hbm, *rest, epilogue, col0, tn, n_col_tiles):
    extra, o_ref, stage_ref, wbf_ref, sem = rest[:-4], rest[-4], rest[-3], rest[-2], rest[-1]
    j, i = pl.program_id(0), pl.program_id(1)

    def w_copy(jj):
        col = pl.multiple_of(col0 + jj * tn, LANES)
        return pltpu.make_async_copy(w_hbm.at[:, pl.ds(col, tn)], stage_ref, sem)

    @pl.when((i == 0) & (j == 0))
    def _():
        w_copy(0).start()

    @pl.when(i == 0)
    def _():
        w_copy(j).wait()
        wbf_ref[...] = stage_ref[...].astype(BF16)

        @pl.when(j + 1 < n_col_tiles)
        def _():
            w_copy(j + 1).start()

    t = jnp.dot(a_ref[...], wbf_ref[...], preferred_element_type=F32)
    o_ref[...] = epilogue(t, *extra).astype(o_ref.dtype)


def _ep_plain(t):
    return t


def _ep_gate(t, g_ref):
    return jax.nn.sigmoid(g_ref[...].astype(F32)) * t


def _ep_gate_add(t, g_ref, prev_ref):
    return prev_ref[...].astype(F32) + jax.nn.sigmoid(g_ref[...].astype(F32)) * t


def _ep_residual(t, x_ref, m_ref, *, gate_row):
    return x_ref[...] + m_ref[0, gate_row:gate_row + 1, :] * t


def _mm(a, w, col0, ncols, m_rows, tm, tn, out_dtype, name, epilogue=_ep_plain, extra=(), extra_specs=()):
    k = a.shape[1]
    assert ncols % tn == 0 and m_rows % tm == 0 and col0 % LANES == 0
    return pl.pallas_call(
        functools.partial(_mm_kernel, epilogue=epilogue, col0=col0, tn=tn, n_col_tiles=ncols // tn),
        grid=(ncols // tn, m_rows // tm),
        in_specs=[pl.BlockSpec((tm, k), lambda j, i: (i, 0)), pl.BlockSpec(memory_space=pl.ANY)] + list(extra_specs),
        out_specs=pl.BlockSpec((tm, tn), lambda j, i: (i, j)),
        out_shape=jax.ShapeDtypeStruct((m_rows, ncols), out_dtype),
        scratch_shapes=[pltpu.VMEM((k, tn), F32), pltpu.VMEM((k, tn), BF16), pltpu.SemaphoreType.DMA(())],
        compiler_params=_cparams(("arbitrary", "arbitrary")),
        name=name,
    )(a, w, *extra)


def _qk_post_kernel(x_ref, w_ref, cos_ref, sin_ref, o_ref, *, n_heads, scale):
    lane = lax.broadcasted_iota(jnp.int32, (x_ref.shape[0], HEAD_DIM), 1)
    first = (lane % (HEAD_DIM // 2)) < (HEAD_DIM // 4)
    for h in range(n_heads):
        sl = slice(h * HEAD_DIM, (h + 1) * HEAD_DIM)
        xh = x_ref[:, sl].astype(F32)
        ms = jnp.mean(xh * xh, axis=-1, keepdims=True)
        y = xh * lax.rsqrt(ms + EPS) * w_ref[...]
        partner = jnp.where(first, pltpu.roll(y, HEAD_DIM - HEAD_DIM // 4, 1), pltpu.roll(y, HEAD_DIM // 4, 1))
        y = y * cos_ref[...] + partner * sin_ref[...]
        o_ref[:, sl] = (y * scale).astype(o_ref.dtype)


def _qk_post(src, col_blk0, n_heads_total, rows, w, cos_t, sin_t, seq, n_x_rows, scale, name):
    tm = _tile(min(seq, rows), 256)
    hpb = 4
    tpb = seq // tm
    nx = n_x_rows // tm
    tab_map = lambda i, j: (jnp.where(i < nx, i % tpb, tpb), 0)
    return pl.pallas_call(
        functools.partial(_qk_post_kernel, n_heads=hpb, scale=scale),
        grid=(rows // tm, n_heads_total // hpb),
        in_specs=[pl.BlockSpec((tm, hpb * HEAD_DIM), lambda i, j: (i, col_blk0 + j)),
                  pl.BlockSpec((1, HEAD_DIM), lambda i, j: (0, 0)),
                  pl.BlockSpec((tm, HEAD_DIM), tab_map),
                  pl.BlockSpec((tm, HEAD_DIM), tab_map)],
        out_specs=pl.BlockSpec((tm, hpb * HEAD_DIM), lambda i, j: (i, j)),
        out_shape=jax.ShapeDtypeStruct((rows, n_heads_total * HEAD_DIM), BF16),
        compiler_params=_cparams(("arbitrary", "arbitrary")),
        name=name,
    )(src, w.reshape(1, HEAD_DIM), cos_t, sin_t)


def _rope_tables(seq, tm):
    rows = seq // GRID_W
    row_pos = jnp.repeat(jnp.arange(rows, dtype=F32), GRID_W)
    col_pos = (jnp.arange(seq) % GRID_W).astype(F32)
    inv_freq = ROPE_THETA ** (-jnp.arange(ROPE_AXIS_FREQS, dtype=F32) / ROPE_AXIS_FREQS)
    ar = row_pos[:, None] * inv_freq
    ac = col_pos[:, None] * inv_freq
    cos_t = jnp.concatenate([jnp.cos(ar), jnp.cos(ar), jnp.cos(ac), jnp.cos(ac)], axis=-1)
    sin_t = jnp.concatenate([-jnp.sin(ar), jnp.sin(ar), -jnp.sin(ac), jnp.sin(ac)], axis=-1)
    cos_t = jnp.concatenate([cos_t, jnp.ones((tm, HEAD_DIM), F32)], axis=0)
    sin_t = jnp.concatenate([sin_t, jnp.zeros((tm, HEAD_DIM), F32)], axis=0)
    return cos_t, sin_t


def _attn_kernel(q_ref, wq_ref, cos_ref, sin_ref, kx_ref, kc_ref, vx_ref, vc_ref, o_ref):
    nt = (((1,), (1,)), ((), ()))
    kx, kc = kx_ref[...], kc_ref[...]

    def with_ones(v):
        lane = lax.broadcasted_iota(jnp.int32, v.shape, 1)
        return jnp.concatenate([v, jnp.where(lane == 0, 1.0, 0.0).astype(v.dtype)], axis=1)

    vx, vc = with_ones(vx_ref[...]), with_ones(vc_ref[...])
    tq = q_ref.shape[0]
    rows = tq // ATTN_ROW_SPLIT
    units = [(g, r) for g in range(GQA_GROUP) for r in range(ATTN_ROW_SPLIT)]

    lane = lax.broadcasted_iota(jnp.int32, (rows, HEAD_DIM), 1)
    first = (lane % (HEAD_DIM // 2)) < (HEAD_DIM // 4)

    def scores(u):
        g, r = u
        rs = slice(r * rows, (r + 1) * rows)
        xh = q_ref[rs, g * HEAD_DIM:(g + 1) * HEAD_DIM].astype(F32)
        y = xh * lax.rsqrt(jnp.mean(xh * xh, axis=-1, keepdims=True) + EPS) * wq_ref[...]
        partner = jnp.where(first, pltpu.roll(y, HEAD_DIM - HEAD_DIM // 4, 1), pltpu.roll(y, HEAD_DIM // 4, 1))
        q = (y * cos_ref[rs, :] + partner * sin_ref[rs, :]).astype(BF16)
        return (lax.dot_general(q, kx, nt, preferred_element_type=F32),
                lax.dot_general(q, kc, nt, preferred_element_type=F32))

    nxt = scores(units[0])
    for n, (g, r) in enumerate(units):
        s1, s2 = nxt
        if n + 1 < len(units):
            nxt = scores(units[n + 1])
        m = jnp.maximum(jnp.max(s1, axis=-1, keepdims=True), jnp.max(s2, axis=-1, keepdims=True))
        p1 = jnp.exp2(s1 - m).astype(BF16)
        p2 = jnp.exp2(s2 - m).astype(BF16)
        o = jnp.dot(p1, vx, preferred_element_type=F32) + jnp.dot(p2, vc, preferred_element_type=F32)
        o_ref[r * rows:(r + 1) * rows, g * HEAD_DIM:(g + 1) * HEAD_DIM] = (
            o[:, :HEAD_DIM] / o[:, HEAD_DIM:HEAD_DIM + 1]).astype(o_ref.dtype)


def _attention(p3, wq, cos_t, sin_t, k, p1, batch, seq, ctx_len):
    tq = _tile(seq, ATTN_TQ)
    qpb = seq // tq
    gw = GQA_GROUP * HEAD_DIM
    cblk0 = batch * seq // ctx_len
    return pl.pallas_call(
        _attn_kernel,
        grid=(batch, N_KV_HEADS, qpb),
        in_specs=[pl.BlockSpec((tq, gw), lambda b, h, i: (b * qpb + i, h)),
                  pl.BlockSpec((1, HEAD_DIM), lambda b, h, i: (0, 0)),
                  pl.BlockSpec((tq, HEAD_DIM), lambda b, h, i: (i, 0)),
                  pl.BlockSpec((tq, HEAD_DIM), lambda b, h, i: (i, 0)),
                  pl.BlockSpec((seq, HEAD_DIM), lambda b, h, i: (b, h)),
                  pl.BlockSpec((ctx_len, HEAD_DIM), lambda b, h, i: (cblk0 + b, h)),
                  pl.BlockSpec((seq, HEAD_DIM), lambda b, h, i: (b, N_KV_HEADS + h)),
                  pl.BlockSpec((ctx_len, HEAD_DIM), lambda b, h, i: (cblk0 + b, N_KV_HEADS + h))],
        out_specs=pl.BlockSpec((tq, gw), lambda b, h, i: (b * qpb + i, h)),
        out_shape=jax.ShapeDtypeStruct((batch * seq, N_Q_HEADS * HEAD_DIM), BF16),
        compiler_params=_cparams(("arbitrary", "arbitrary", "arbitrary")),
        name="attention",
    )(p3, wq.reshape(1, HEAD_DIM), cos_t, sin_t, k, k, p1, p1)


def _conv_kernel(prev_ref, cur_ref, next_ref, w_ref, b_ref, o_ref, buf_ref, *, tl, x_tiles, x_tpb, c_tpb):
    i = pl.program_id(0)
    j = jnp.where(i < x_tiles, i % x_tpb, (i - x_tiles) % c_tpb)
    n = jnp.where(i < x_tiles, x_tpb, c_tpb)
    buf_ref[0:HALO, :] = jnp.where(j == 0, 0.0, prev_ref[...].astype(F32))
    buf_ref[HALO:HALO + tl, :] = cur_ref[...].astype(F32)
    buf_ref[HALO + tl:, :] = jnp.where(j == n - 1, 0.0, next_ref[...].astype(F32))
    pad = CONV_W // 2
    xall = buf_ref[...]
    n_rows = xall.shape[0]
    acc = b_ref[...] + w_ref[pad:pad + 1, :] * xall[HALO:HALO + tl]
    for t in range(CONV_W):
        if t != pad:
            acc = acc + w_ref[t:t + 1, :] * pltpu.roll(xall, (pad - t) % n_rows, 0)[HALO:HALO + tl]
    o_ref[...] = _silu(acc).astype(o_ref.dtype)


def _conv_silu(p1, col0, ncols, conv_w, conv_b, seq, ctx_len, n_x_rows):
    rows = p1.shape[0]
    tl = _tile(min(seq, ctx_len), 256)
    tc = _tile(ncols, 2048)
    assert col0 % tc == 0
    cb0 = col0 // tc
    hb = tl // HALO
    last_hb = rows // HALO - 1
    return pl.pallas_call(
        functools.partial(_conv_kernel, tl=tl, x_tiles=n_x_rows // tl, x_tpb=seq // tl, c_tpb=ctx_len // tl),
        grid=(rows // tl, ncols // tc),
        in_specs=[pl.BlockSpec((HALO, tc), lambda i, j: (jnp.maximum(i * hb - 1, 0), cb0 + j)),
                  pl.BlockSpec((tl, tc), lambda i, j: (i, cb0 + j)),
                  pl.BlockSpec((HALO, tc), lambda i, j: (jnp.minimum((i + 1) * hb, last_hb), cb0 + j)),
                  pl.BlockSpec((CONV_W, tc), lambda i, j: (0, j)),
                  pl.BlockSpec((1, tc), lambda i, j: (0, j))],
        out_specs=pl.BlockSpec((tl, tc), lambda i, j: (i, j)),
        out_shape=jax.ShapeDtypeStruct((rows, ncols), BF16),
        scratch_shapes=[pltpu.VMEM((tl + 2 * HALO, tc), F32)],
        compiler_params=_cparams(("arbitrary", "arbitrary")),
        name="conv_silu",
    )(p1, p1, p1, conv_w, conv_b.reshape(1, ncols))


def _dt_prep_kernel(raw_ref, bias_ref, alog_ref, acum_ref, dt_ref, *, n_heads):
    v = raw_ref[...] + bias_ref[...]
    dt = jnp.maximum(v, 0.0) + jnp.log(1.0 + jnp.exp(-jnp.abs(v)))
    dta = dt * (-jnp.exp(alog_ref[...]))
    r = lax.broadcasted_iota(jnp.int32, (CHUNK, CHUNK), 0)
    c = lax.broadcasted_iota(jnp.int32, (CHUNK, CHUNK), 1)
    tril = jnp.where(r >= c, 1.0, 0.0).astype(BF16)
    triu = jnp.where(r <= c, 1.0, 0.0).astype(BF16)
    lane = lax.broadcasted_iota(jnp.int32, dta.shape, 1)
    acum = jnp.where(lane < n_heads, _dot_exact_lhs(tril, dta), _dot_exact_lhs(triu, dta))
    acum_ref[0] = acum.T
    dt_ref[0] = dt.T


def _dt_prep(dt_raw, bias2, alog2, n_heads):
    rows, w = dt_raw.shape
    nch = rows // CHUNK
    out = jax.ShapeDtypeStruct((nch, w, CHUNK), F32)
    return pl.pallas_call(
        functools.partial(_dt_prep_kernel, n_heads=n_heads),
        grid=(nch,),
        in_specs=[pl.BlockSpec((CHUNK, w), lambda i: (i, 0)),
                  pl.BlockSpec((1, w), lambda i: (0, 0)),
                  pl.BlockSpec((1, w), lambda i: (0, 0))],
        out_specs=[pl.BlockSpec((1, w, CHUNK), lambda i: (i, 0, 0))] * 2,
        out_shape=[out, out],
        compiler_params=_cparams(("arbitrary",)),
        name="dt_prep",
    )(dt_raw, bias2, alog2)


def _split3_f32(v):
    hi = v.astype(BF16).astype(F32)
    r1 = v - hi
    mid = r1.astype(BF16).astype(F32)
    lo = (r1 - mid).astype(BF16).astype(F32)
    return [hi, mid, lo]


def _ssd_consts(hpg):
    p = SSD_HEAD_DIM
    gw = hpg * p
    k = np.arange(CHUNK)[:, None]

    def expand(base, width, per):
        col_head = (np.arange(width) // per)[None, :]
        kk = k - base
        return ((kk >= 0) & (kk < 3 * hpg) & (kk % hpg == col_head)).astype(np.float32)

    mats = [expand(0, gw, p), expand(3 * hpg, gw, p), expand(9 * hpg, gw, p), expand(6 * hpg, hpg * CHUNK, CHUNK)]
    return jnp.asarray(np.concatenate(mats, axis=1), dtype=BF16)


def _ssd_kernel(xf_ref, bf_ref, cf_ref, af_ref, df_ref, xb_ref, bb_ref, cb_ref, ab_ref, db_ref, h0_ref, k_ref,
                yf_ref, yb_ref, hfin_ref, h_scr, *, hpg, gps):
    k = pl.program_id(2)
    nck = pl.num_programs(2)
    p = SSD_HEAD_DIM
    gw = hpg * p

    @pl.when(k == 0)
    def _():
        h_scr[...] = h0_ref[0]

    ri = lax.broadcasted_iota(jnp.int32, (CHUNK, CHUNK), 0)
    ci = lax.broadcasted_iota(jnp.int32, (CHUNK, CHUNK), 1)
    low_half = (lax.broadcasted_iota(jnp.int32, (CHUNK, gw), 1) % (2 * p)) < p
    nt = (((1,), (1,)), ((), ()))
    e_exp_a, e_to_end, e_gain = k_ref[:, 0:gw], k_ref[:, gw:2 * gw], k_ref[:, 2 * gw:3 * gw]
    e_col_a = k_ref[:, 3 * gw:]
    pad_rows = jnp.zeros((CHUNK - 12 * hpg, CHUNK), F32)

    dirs = ((xf_ref, bf_ref, cf_ref, af_ref, df_ref, yf_ref), (xb_ref, bb_ref, cb_ref, ab_ref, db_ref, yb_ref))
    for gi, d in [(gi, d) for gi in range(gps) for d in range(2)]:
        x_ref, b_ref, c_ref, a_ref, dt_ref, y_ref = dirs[d]
        gcols = slice(gi * gw, (gi + 1) * gw)
        ncols = slice(gi * SSD_STATE, (gi + 1) * SSD_STATE)
        row_a = a_ref[0, gi * hpg:(gi + 1) * hpg, :]
        row_dt = dt_ref[0, gi * hpg:(gi + 1) * hpg, :]
        if d == 0:
            mask = ri >= ci
            tot = row_a[:, CHUNK - 1:CHUNK]
        else:
            mask = ri <= ci
            tot = row_a[:, 0:1]
        exp_a = jnp.exp(row_a)
        to_end = jnp.exp(tot - row_a) * row_dt
        gain = jnp.broadcast_to(jnp.exp(tot), row_a.shape)
        table = jnp.concatenate(_split3_f32(exp_a) + _split3_f32(to_end) + _split3_f32(row_a)
                                + _split3_f32(gain) + [pad_rows], axis=0)
        tab_t = table.T.astype(BF16)
        exp_a_full = jnp.dot(tab_t, e_exp_a, preferred_element_type=F32)
        to_end_full = jnp.dot(tab_t, e_to_end, preferred_element_type=F32)
        gain_full = jnp.dot(tab_t[0:16], e_gain, preferred_element_type=F32)[0:1]
        col_a = jnp.dot(tab_t, e_col_a, preferred_element_type=F32)
        bc = b_ref[:, ncols]
        cc = c_ref[:, ncols]
        cb = lax.dot_general(cc, bc, nt, preferred_element_type=F32)
        x32 = x_ref[:, gcols].astype(F32)
        x_lo = jnp.where(low_half, x32, 0.0).astype(BF16)
        x_hi = jnp.where(low_half, 0.0, x32).astype(BF16)
        y_parts = []
        for q in range(hpg // 2):
            ws = []
            for r in (2 * q, 2 * q + 1):
                seg = col_a[:, r * CHUNK:(r + 1) * CHUNK] - row_a[r:r + 1, :]
                ws.append((cb * jnp.exp(jnp.where(mask, seg, -jnp.inf)) * row_dt[r:r + 1, :]).astype(BF16))
            sl = slice(q * 2 * p, (q + 1) * 2 * p)
            y_parts.append(jnp.dot(jnp.concatenate(ws, axis=1), jnp.concatenate([x_lo[:, sl], x_hi[:, sl]], axis=0),
                                   preferred_element_type=F32))
        y_state = jnp.dot(cc, h_scr[gi, d].astype(BF16), preferred_element_type=F32)
        y_ref[:, gcols] = (jnp.concatenate(y_parts, axis=1) + y_state * exp_a_full).astype(y_ref.dtype)
        bct = bc.astype(F32).T.astype(BF16)
        upd = jnp.dot(bct, (x32 * to_end_full).astype(BF16), preferred_element_type=F32)
        h_scr[gi, d] = h_scr[gi, d] * gain_full + upd

    @pl.when(k == nck - 1)
    def _():
        hfin_ref[0] = h_scr[...]


def _ssd(xbc, acum_t, dt_t, h0, consts, batch, length, chunk0, hpg, d_inner):
    nck = length // CHUNK
    gw = hpg * SSD_HEAD_DIM
    assert hpg % 2 == 0 and 2 * SSD_HEAD_DIM == LANES and 12 * hpg <= CHUNK
    gps = SSD_GROUPS_PER_STEP
    n_gsteps = SSD_GROUPS // gps
    b_blk0 = d_inner // (gps * SSD_STATE)
    c_blk0 = b_blk0 + n_gsteps
    assert SSD_GROUPS % gps == 0 and d_inner % (gps * SSD_STATE) == 0
    fwd = lambda b, g, k: chunk0 + b * nck + k
    bwd = lambda b, g, k: chunk0 + b * nck + (nck - 1 - k)

    def dir_specs(ch, d):
        return [pl.BlockSpec((CHUNK, gps * gw), lambda b, g, k: (ch(b, g, k), g)),
                pl.BlockSpec((CHUNK, gps * SSD_STATE), lambda b, g, k: (ch(b, g, k), b_blk0 + g)),
                pl.BlockSpec((CHUNK, gps * SSD_STATE), lambda b, g, k: (ch(b, g, k), c_blk0 + g)),
                pl.BlockSpec((1, gps * hpg, CHUNK), lambda b, g, k: (ch(b, g, k), d * n_gsteps + g, 0)),
                pl.BlockSpec((1, gps * hpg, CHUNK), lambda b, g, k: (ch(b, g, k), d * n_gsteps + g, 0))]

    st_spec = pl.BlockSpec((1, gps, 2, SSD_STATE, gw), lambda b, g, k: (b, g, 0, 0, 0))
    y_shape = jax.ShapeDtypeStruct((batch * length, d_inner), BF16)
    return pl.pallas_call(
        functools.partial(_ssd_kernel, hpg=hpg, gps=gps),
        grid=(batch, n_gsteps, nck),
        in_specs=dir_specs(fwd, 0) + dir_specs(bwd, 1) + [st_spec, pl.BlockSpec(consts.shape, lambda b, g, k: (0, 0))],
        out_specs=[pl.BlockSpec((CHUNK, gps * gw), lambda b, g, k: (b * nck + k, g)),
                   pl.BlockSpec((CHUNK, gps * gw), lambda b, g, k: (b * nck + (nck - 1 - k), g)),
                   st_spec],
        out_shape=[y_shape, y_shape, jax.ShapeDtypeStruct(h0.shape, F32)],
        scratch_shapes=[pltpu.VMEM((gps, 2, SSD_STATE, gw), F32)],
        compiler_params=_cparams(("arbitrary", "arbitrary", "arbitrary")),
        name="ssd",
    )(xbc, xbc, xbc, acum_t, dt_t, xbc, xbc, xbc, acum_t, dt_t, h0, consts)


def _ssd_norm_kernel(yf_ref, yb_ref, xs_ref, z_ref, ds_ref, w_ref, o_ref):
    y = yf_ref[...].astype(F32) + yb_ref[...].astype(F32) + ds_ref[...] * xs_ref[...].astype(F32)
    y = y * _silu(z_ref[...].astype(F32))
    ms = jnp.mean(y * y, axis=-1, keepdims=True)
    o_ref[...] = (y * lax.rsqrt(ms + EPS) * w_ref[...]).astype(o_ref.dtype)


def _ssd_norm(yf, yb, xbc, p3, z_blk, dskip_row, w):
    m, d = yf.shape
    tm = _tile(m, 256)
    row = lambda i: (i, 0)
    return pl.pallas_call(
        _ssd_norm_kernel,
        grid=(m // tm,),
        in_specs=[pl.BlockSpec((tm, d), row), pl.BlockSpec((tm, d), row), pl.BlockSpec((tm, d), row),
                  pl.BlockSpec((tm, d), lambda i: (i, z_blk)),
                  pl.BlockSpec((1, d), lambda i: (0, 0)), pl.BlockSpec((1, d), lambda i: (0, 0))],
        out_specs=pl.BlockSpec((tm, d), row),
        out_shape=jax.ShapeDtypeStruct((m, d), BF16),
        compiler_params=_cparams(("arbitrary",)),
        name="ssd_norm",
    )(yf, yb, xbc, p3, dskip_row, w.reshape(1, d))


def _merge(attn, ynorm, w1, w2, p3, g1_col0, g2_col0):
    m, _ = attn.shape
    n = w1.shape[1]
    tm, tn = _tile(m, 512), _tile(n, 1024)
    blk = lambda c0: pl.BlockSpec((tm, tn), lambda j, i: (i, c0 // tn + j))
    part = _mm(attn, w1, 0, n, m, tm, tn, BF16, "merge_attn", _ep_gate, (p3,), (blk(g1_col0),))
    return _mm(ynorm, w2, 0, n, m, tm, tn, BF16, "merge_ssd", _ep_gate_add, (p3, part), (blk(g2_col0), blk(0)))


def _out_proj(merged, w, x2, mod3, seq, gate_row):
    m, _ = merged.shape
    n = w.shape[1]
    tm, tn = _tile(seq, 512), _tile(n, 1024)
    tpb = seq // tm
    return _mm(merged, w, 0, n, m, tm, tn, F32, "out_proj", functools.partial(_ep_residual, gate_row=gate_row),
               (x2, mod3), (pl.BlockSpec((tm, tn), lambda j, i: (i, j)),
                            pl.BlockSpec((1, N_MOD, tn), lambda j, i: (i // tpb, 0, j))))


def _ffn_norm_route_kernel(x_ref, w_ref, m_ref, wr_ref, br_ref, h_ref, route_ref):
    h = _norm_mod(x_ref[...], w_ref[...], m_ref[0, 3:4, :], m_ref[0, 4:5, :])
    h_ref[...] = _pack_pairs(h)
    h_hi = h.astype(BF16)
    h_lo = (h - h_hi.astype(F32)).astype(BF16)
    wr = wr_ref[...]
    w_hi = wr.astype(BF16)
    w_lo = (wr - w_hi.astype(F32)).astype(BF16)
    logits = (jnp.dot(h_hi, w_hi, preferred_element_type=F32) + jnp.dot(h_hi, w_lo, preferred_element_type=F32)
              + jnp.dot(h_lo, w_hi, preferred_element_type=F32)) + br_ref[...]
    lane = lax.broadcasted_iota(jnp.int32, logits.shape, 1).astype(F32)
    big = float(LANES)
    neg = -jnp.inf
    gl = jnp.where(lane < N_GROUPS, logits, neg)
    gmax = jnp.max(gl, axis=-1, keepdims=True)
    gidx = jnp.min(jnp.where(gl == gmax, lane, big), axis=-1, keepdims=True)
    g_prob = 1.0 / jnp.sum(jnp.exp(gl - gmax), axis=-1, keepdims=True)
    lo = N_GROUPS + gidx * EXPERTS_PER_GROUP
    el = jnp.where((lane >= lo) & (lane < lo + EXPERTS_PER_GROUP), logits, neg)
    m1 = jnp.max(el, axis=-1, keepdims=True)
    i1 = jnp.min(jnp.where(el == m1, lane, big), axis=-1, keepdims=True)
    el2 = jnp.where(lane == i1, neg, el)
    m2 = jnp.max(el2, axis=-1, keepdims=True)
    i2 = jnp.min(jnp.where(el2 == m2, lane, big), axis=-1, keepdims=True)
    z = jnp.sum(jnp.exp(el - m1), axis=-1, keepdims=True)
    p1 = 1.0 / z
    p2 = jnp.exp(m2 - m1) / z
    w1 = g_prob * p1 / (p1 + p2)
    w2 = g_prob * p2 / (p1 + p2)
    route = jnp.where(lane == 0, i1 - N_GROUPS, jnp.where(lane == 1, i2 - N_GROUPS,
                      jnp.where(lane == 2, w1, jnp.where(lane == 3, w2, 0.0))))
    route_ref[...] = route


def _ffn_norm_route(x1, w, mod3, wr, br, seq):
    m, d = x1.shape
    tm = _tile(seq, 256)
    tpb = seq // tm
    return pl.pallas_call(
        _ffn_norm_route_kernel,
        grid=(m // tm,),
        in_specs=[pl.BlockSpec((tm, d), lambda i: (i, 0)),
                  pl.BlockSpec((1, d), lambda i: (0, 0)),
                  pl.BlockSpec((1, N_MOD, d), lambda i: (i // tpb, 0, 0)),
                  pl.BlockSpec((d, LANES), lambda i: (0, 0)),
                  pl.BlockSpec((1, LANES), lambda i: (0, 0))],
        out_specs=[pl.BlockSpec((tm, d // 2), lambda i: (i, 0)), pl.BlockSpec((tm, LANES), lambda i: (i, 0))],
        out_shape=[jax.ShapeDtypeStruct((m, d // 2), jnp.uint32), jax.ShapeDtypeStruct((m, LANES), F32)],
        compiler_params=_cparams(("arbitrary",)),
        name="ffn_norm_route",
    )(x1, w.reshape(1, d), mod3, wr, br)


def _gather_kernel(tok_ref, nv_ref, src_ref, o_ref, buf_ref, sem, *, blk):
    i = pl.program_id(0)
    base = i * blk
    n_groups = (nv_ref[i] + ROW_GROUP - 1) // ROW_GROUP

    def issue(g, c):
        for u in range(ROW_GROUP):
            r = g * ROW_GROUP + u
            pltpu.make_async_copy(src_ref.at[pl.ds(tok_ref[base + r], 1)], buf_ref.at[pl.ds(r, 1)], sem).start()
        return c

    lax.fori_loop(0, n_groups, issue, 0)

    def zero(g, c):
        r0 = pl.multiple_of(g * ROW_GROUP, ROW_GROUP)
        buf_ref[pl.ds(r0, ROW_GROUP), :] = jnp.zeros((ROW_GROUP, buf_ref.shape[1]), buf_ref.dtype)
        return c

    lax.fori_loop(n_groups, blk // ROW_GROUP, zero, 0)

    def drain(g, c):
        pltpu.make_async_copy(src_ref.at[pl.ds(0, ROW_GROUP)], buf_ref.at[pl.ds(0, ROW_GROUP)], sem).wait()
        return c

    lax.fori_loop(0, n_groups, drain, 0)
    lo, hi = _unpack_pairs(buf_ref[...])
    half = lo.shape[1]
    o_ref[:, :half] = lo.astype(o_ref.dtype)
    o_ref[:, half:] = hi.astype(o_ref.dtype)


def _gather_rows(row_tok, n_valid, src, blk):
    n_rows = row_tok.shape[0]
    dp = src.shape[1]
    d = 2 * dp
    return pl.pallas_call(
        functools.partial(_gather_kernel, blk=blk),
        grid_spec=pltpu.PrefetchScalarGridSpec(
            num_scalar_prefetch=2,
            grid=(n_rows // blk,),
            in_specs=[pl.BlockSpec(memory_space=pl.ANY)],
            out_specs=pl.BlockSpec((blk, d), lambda i, tok, nv: (i, 0)),
            scratch_shapes=[pltpu.VMEM((blk, dp), src.dtype), pltpu.SemaphoreType.DMA(())]),
        out_shape=jax.ShapeDtypeStruct((n_rows, d), BF16),
        compiler_params=_cparams(("arbitrary",)),
        name="moe_gather",
    )(row_tok, n_valid, src)


def _expert_mm_kernel(ie_ref, it_ref, ib_ref, if_ref, iv_ref, ne_ref, nt_ref, hn_ref, a_ref, *rest, n_w, tn, compute):
    w_hbm = rest[:n_w]
    o_ref = rest[n_w]
    stage = rest[n_w + 1:2 * n_w + 1]
    w_bf = rest[2 * n_w + 1:3 * n_w + 1]
    sem = rest[3 * n_w + 1]
    i = pl.program_id(0)

    def copies(e, t):
        col = pl.multiple_of(t * tn, LANES)
        return [pltpu.make_async_copy(w_hbm[n].at[e, :, pl.ds(col, tn)], stage[n], sem.at[n]) for n in range(n_w)]

    @pl.when(i == 0)
    def _():
        for cp in copies(ie_ref[0], it_ref[0]):
            cp.start()

    @pl.when(if_ref[i] == 1)
    def _():
        for cp in copies(ie_ref[i], it_ref[i]):
            cp.wait()
        for n in range(n_w):
            w_bf[n][...] = stage[n][...].astype(BF16)

        @pl.when(hn_ref[i] == 1)
        def _():
            for cp in copies(ne_ref[i], nt_ref[i]):
                cp.start()

    @pl.when(iv_ref[i] == 1)
    def _():
        o_ref[...] = compute(a_ref[...], [w[...] for w in w_bf]).astype(o_ref.dtype)

    @pl.when(iv_ref[i] == 0)
    def _():
        o_ref[...] = jnp.zeros(o_ref.shape, o_ref.dtype)


def _expert_mm(items, a, weights, blk, tn, compute, out_dtype, name, out_div=1):
    n_items = items[0].shape[0]
    n_rows, k = a.shape
    n = weights[0].shape[2]
    n_w = len(weights)
    assert out_div == 1 or tn == n
    imap_a = lambda i, ie, it, ib, *_: (ib[i], 0)
    imap_o = lambda i, ie, it, ib, *_: (ib[i], it[i])
    return pl.pallas_call(
        functools.partial(_expert_mm_kernel, n_w=n_w, tn=tn, compute=compute),
        grid_spec=pltpu.PrefetchScalarGridSpec(
            num_scalar_prefetch=len(items),
            grid=(n_items,),
            in_specs=[pl.BlockSpec((blk, k), imap_a)] + [pl.BlockSpec(memory_space=pl.ANY)] * n_w,
            out_specs=pl.BlockSpec((blk, tn // out_div), imap_o),
            scratch_shapes=([pltpu.VMEM((k, tn), F32)] * n_w + [pltpu.VMEM((k, tn), BF16)] * n_w
                            + [pltpu.SemaphoreType.DMA((n_w,))])),
        out_shape=jax.ShapeDtypeStruct((n_rows, n // out_div), out_dtype),
        compiler_params=_cparams(("arbitrary",)),
        name=name,
    )(*items, a, *weights)


def _up_compute(xb, ws):
    a = jnp.dot(xb, ws[0], preferred_element_type=F32)
    u = jnp.dot(xb, ws[1], preferred_element_type=F32)
    return _silu(a) * u


def _down_compute(hb, ws):
    return _pack_pairs(jnp.dot(hb, ws[0], preferred_element_type=F32))


def _combine_kernel(pos_ref, y_ref, x_ref, rt_ref, m_ref, o_ref, buf_ref, sem, *, tm):
    base = pl.program_id(0) * tm

    def issue(g, c):
        for u in range(ROW_GROUP):
            r = g * ROW_GROUP + u
            for kk in range(TOP_K):
                pltpu.make_async_copy(y_ref.at[pl.ds(pos_ref[TOP_K * (base + r) + kk], 1)],
                                      buf_ref.at[kk, pl.ds(r, 1)], sem).start()
        return c

    lax.fori_loop(0, tm // ROW_GROUP, issue, 0)
    for kk in range(TOP_K):
        pltpu.make_async_copy(y_ref.at[pl.ds(0, tm)], buf_ref.at[kk], sem).wait()
    rt = rt_ref[...]
    lo0, hi0 = _unpack_pairs(buf_ref[0])
    lo1, hi1 = _unpack_pairs(buf_ref[1])
    half = lo0.shape[1]
    w0, w1 = rt[:, 2:3], rt[:, 3:4]
    o_ref[:, :half] = x_ref[:, :half] + m_ref[0, 5:6, :half] * (w0 * lo0 + w1 * lo1)
    o_ref[:, half:] = x_ref[:, half:] + m_ref[0, 5:6, half:] * (w0 * hi0 + w1 * hi1)


def _combine(pos, yexp, x1, route, mod3, seq):
    m, d = x1.shape
    tm = _tile(seq, 128)
    tpb = seq // tm
    return pl.pallas_call(
        functools.partial(_combine_kernel, tm=tm),
        grid_spec=pltpu.PrefetchScalarGridSpec(
            num_scalar_prefetch=1,
            grid=(m // tm,),
            in_specs=[pl.BlockSpec(memory_space=pl.ANY),
                      pl.BlockSpec((tm, d), lambda i, pos: (i, 0)),
                      pl.BlockSpec((tm, LANES), lambda i, pos: (i, 0)),
                      pl.BlockSpec((1, N_MOD, d), lambda i, pos: (i // tpb, 0, 0))],
            out_specs=pl.BlockSpec((tm, d), lambda i, pos: (i, 0)),
            scratch_shapes=[pltpu.VMEM((TOP_K, tm, d // 2), jnp.uint32), pltpu.SemaphoreType.DMA(())]),
        out_shape=jax.ShapeDtypeStruct((m, d), F32),
        compiler_params=_cparams(("arbitrary",)),
        name="moe_combine",
    )(pos, yexp, x1, route, mod3)


def _moe_plan(route, blk, n_tiles_up, n_tiles_down):
    t = route.shape[0]
    n_assign = t * TOP_K
    nb_max = n_assign // blk + N_EXPERTS
    flat_e = route[:, 0:TOP_K].astype(jnp.int32).reshape(-1)
    onehot = (flat_e[:, None] == jnp.arange(N_EXPERTS, dtype=jnp.int32)[None, :]).astype(jnp.int32)
    cum = jnp.cumsum(onehot, axis=0)
    rank = jnp.take_along_axis(cum, flat_e[:, None], axis=1)[:, 0] - 1
    counts = cum[-1]
    nblk = (counts + blk - 1) // blk
    pend = jnp.cumsum(nblk)
    pstart = pend - nblk
    dest = pstart[flat_e] * blk + rank
    row_tok = jnp.zeros((nb_max * blk,), jnp.int32).at[dest].set(jnp.arange(n_assign, dtype=jnp.int32) // TOP_K)
    total = pend[-1]
    unused = jnp.maximum(nb_max - total, 1)

    def items(n_tiles):
        j = jnp.arange(n_tiles * nb_max, dtype=jnp.int32)
        valid = j < n_tiles * total
        jj = jnp.minimum(j, n_tiles * total - 1)
        e = jnp.minimum(jnp.sum((jj[:, None] >= n_tiles * pend[None, :]).astype(jnp.int32), axis=1), N_EXPERTS - 1)
        local = jj - n_tiles * pstart[e]
        nb_e = jnp.maximum(nblk[e], 1)
        u = j - n_tiles * total
        tile = jnp.where(valid, local // nb_e, u // unused)
        b = jnp.where(valid, pstart[e] + local % nb_e, total + u % unused)
        first = valid & (local % nb_e == 0)
        nxt = j + nb_e
        has_next = first & (nxt < n_tiles * total)
        nxt = jnp.minimum(nxt, n_tiles * nb_max - 1)
        i32 = lambda v: v.astype(jnp.int32)
        return (e, i32(tile), i32(b), i32(first), i32(valid), e[nxt], i32(tile[nxt]), i32(has_next))

    blk_ids = jnp.arange(nb_max, dtype=jnp.int32)
    blk_e = jnp.minimum(jnp.sum((blk_ids[:, None] >= pend[None, :]).astype(jnp.int32), axis=1), N_EXPERTS - 1)
    n_valid = jnp.clip(counts[blk_e] - (blk_ids - pstart[blk_e]) * blk, 0, blk)
    n_valid = jnp.where(blk_ids < total, n_valid, 0).astype(jnp.int32)
    return row_tok, n_valid, dest, items(n_tiles_up), items(n_tiles_down)


def kernel(x, c, ctx, c_ctx, w_ada, b_ada, norm_mix_w, norm_ffn_w, w_in, q_norm_w, k_norm_w, conv_w, conv_b,
           a_log_f, a_log_b, dt_bias_f, dt_bias_b, d_skip, ssd_norm_w, w_attn_proj, w_ssd_proj, w_out,
           w_router_group, b_router_group, w_router_expert, b_router_expert, w_exp_gate, w_exp_up, w_exp_down):
    batch, seq, d = x.shape
    ctx_len = ctx.shape[1]
    assert w_ada.shape[0] == 1, "single layer: the context stream is read, never updated"
    d_inner = d
    n_ssd_heads = d_inner // SSD_HEAD_DIM
    hpg = n_ssd_heads // SSD_GROUPS
    kv_dim = N_KV_HEADS * HEAD_DIM
    q_dim = N_Q_HEADS * HEAD_DIM
    bc_dim = SSD_GROUPS * SSD_STATE
    xbc_dim = d_inner + 2 * bc_dim
    p1_cols = 2 * kv_dim + xbc_dim
    dt_cols = 2 * n_ssd_heads
    p3_col0 = p1_cols + dt_cols
    p3_cols = q_dim + 2 * d + d_inner
    assert dt_cols == LANES
    mx, mc = batch * seq, batch * ctx_len

    x2 = x.reshape(mx, d)
    c2 = ctx.reshape(mc, d)
    w_in0 = w_in[0]

    cond8 = jnp.zeros((8, d), F32).at[0:batch].set(c).at[batch].set(c_ctx)
    assert batch == 2
    mod3 = _adaln(cond8, w_ada[0], b_ada[0]).reshape(8, N_MOD, d)

    h_all = _norm_mix(x2, c2, norm_mix_w[0], mod3, seq)
    tm_all = 1088 if (mx + mc) % 1088 == 0 else _tile(mx + mc, 512)
    p1 = _mm(h_all, w_in0, 0, p1_cols, mx + mc, tm_all, 1024, BF16, "in_proj_kvx")
    dt_raw = _mm(h_all, w_in0, p1_cols, dt_cols, mx + mc, _tile(mx + mc, 512), dt_cols, F32, "in_proj_dt")
    p3 = _mm(h_all, w_in0, p3_col0, p3_cols, mx, _tile(mx, 1024), 1024, BF16, "in_proj_qgz")

    tmq = _tile(min(seq, mc), 256)
    cos_t, sin_t = _rope_tables(seq, tmq)
    k = _qk_post(p1, 0, N_KV_HEADS, mx + mc, k_norm_w[0], cos_t, sin_t, seq, mx, 1.0, "k_post")
    attn = _attention(p3, q_norm_w[0] * (HEAD_DIM ** -0.5 * LOG2_E), cos_t, sin_t, k, p1, batch, seq, ctx_len)

    xbc = _conv_silu(p1, 2 * kv_dim, xbc_dim, conv_w[0], conv_b[0], seq, ctx_len, mx)
    bias2 = jnp.concatenate([dt_bias_f[0], dt_bias_b[0]]).reshape(1, dt_cols)
    alog2 = jnp.concatenate([a_log_f[0], a_log_b[0]]).reshape(1, dt_cols)
    acum_t, dt_t = _dt_prep(dt_raw, bias2, alog2, n_ssd_heads)
    h_zero = jnp.zeros((batch, SSD_GROUPS, 2, SSD_STATE, hpg * SSD_HEAD_DIM), F32)
    ssd_consts = _ssd_consts(hpg)
    _, _, h_ctx = _ssd(xbc, acum_t, dt_t, h_zero, ssd_consts, batch, ctx_len, mx // CHUNK, hpg, d_inner)
    yf, yb, _ = _ssd(xbc, acum_t, dt_t, h_ctx, ssd_consts, batch, seq, 0, hpg, d_inner)
    dskip_row = jnp.repeat(d_skip[0], SSD_HEAD_DIM).reshape(1, d_inner)
    ynorm = _ssd_norm(yf, yb, xbc, p3, (q_dim + 2 * d) // d_inner, dskip_row, ssd_norm_w[0])

    merged = _merge(attn, ynorm, w_attn_proj[0], w_ssd_proj[0], p3, q_dim, q_dim + d)
    x1 = _out_proj(merged, w_out[0], x2, mod3, seq, 2)

    wr = jnp.zeros((d, LANES), F32).at[:, :N_GROUPS].set(w_router_group[0])
    wr = wr.at[:, N_GROUPS:N_GROUPS + N_EXPERTS].set(w_router_expert[0])
    br = jnp.zeros((1, LANES), F32).at[0, :N_GROUPS].set(b_router_group[0])
    br = br.at[0, N_GROUPS:N_GROUPS + N_EXPERTS].set(b_router_expert[0])
    h2, route = _ffn_norm_route(x1, norm_ffn_w[0], mod3, wr, br, seq)
    d_exp = w_exp_gate.shape[-1]
    tn_up, tn_down = _tile(d_exp, 512), _tile(d, 4096)
    row_tok, n_valid, dest, items_up, items_down = _moe_plan(route, MOE_BLK, d_exp // tn_up, d // tn_down)
    xg = _gather_rows(row_tok, n_valid, h2, MOE_BLK)
    hid = _expert_mm(items_up, xg, (w_exp_gate[0], w_exp_up[0]), MOE_BLK, tn_up, _up_compute, BF16, "moe_up")
    yexp = _expert_mm(items_down, hid, (w_exp_down[0],), MOE_BLK, tn_down, _down_compute, jnp.uint32, "moe_down",
                      out_div=2)
    out = _combine(dest, yexp, x1, route, mod3, seq)
    return out.reshape(batch, seq, d)
```

```python
import functools

import numpy as np
import jax
import jax.numpy as jnp
from jax import lax
from jax.experimental import pallas as pl
from jax.experimental.pallas import tpu as pltpu

F32 = jnp.float32
BF16 = jnp.bfloat16

N_MOD = 6
EPS = 1e-6
GRID_W = 64
N_Q_HEADS = 32
N_KV_HEADS = 8
HEAD_DIM = 128
GQA_GROUP = N_Q_HEADS // N_KV_HEADS
ROPE_THETA = 10000.0
ROPE_AXIS_FREQS = HEAD_DIM // 4
SSD_HEAD_DIM = 64
SSD_GROUPS = 8
SSD_STATE = 128
CONV_W = 5
CHUNK = 128
N_GROUPS = 4
EXPERTS_PER_GROUP = 8
N_EXPERTS = N_GROUPS * EXPERTS_PER_GROUP
TOP_K = 2

LOG2_E = 1.4426950408889634
LANES = 128
HALO = 16
SSD_GROUPS_PER_STEP = 8
MOE_BLK = 256
ROW_GROUP = 8
DMA_PRIORITIES = 2
ATTN_TQ = 512
ATTN_ROW_SPLIT = 4
VMEM_LIMIT = 56 * 1024 * 1024


def _cparams(sem):
    return pltpu.CompilerParams(dimension_semantics=sem, vmem_limit_bytes=VMEM_LIMIT)


def _tile(n, pref):
    t = min(n, pref)
    while n % t:
        t //= 2
    return t


def _silu(v):
    return v * jax.nn.sigmoid(v)


def _pack_pairs(v):
    c = v.shape[1] // 2
    lo = lax.bitcast_convert_type(v[:, :c].astype(BF16).astype(F32), jnp.uint32)
    hi = lax.bitcast_convert_type(v[:, c:].astype(BF16).astype(F32), jnp.uint32)
    return (lo >> 16) | hi


def _unpack_pairs(w):
    lo = lax.bitcast_convert_type(w << 16, F32)
    hi = lax.bitcast_convert_type(w & jnp.uint32(0xFFFF0000), F32)
    return lo, hi


def _split3(v):
    hi = v.astype(BF16)
    r1 = v - hi.astype(F32)
    mid = r1.astype(BF16)
    lo = (r1 - mid.astype(F32)).astype(BF16)
    return hi, mid, lo


def _dot_exact_lhs(m_bf16, v, dims=(((1,), (0,)), ((), ()))):
    hi, mid, lo = _split3(v)
    out = lax.dot_general(m_bf16, hi, dims, preferred_element_type=F32)
    out = out + lax.dot_general(m_bf16, mid, dims, preferred_element_type=F32)
    return out + lax.dot_general(m_bf16, lo, dims, preferred_element_type=F32)


def _adaln_kernel(c_ref, w_ref, b_ref, o_ref):
    s = _silu(c_ref[...]).astype(BF16)
    o_ref[...] = jnp.dot(s, w_ref[...].astype(BF16), preferred_element_type=F32) + b_ref[...]


def _adaln(cond8, w_ada, b_ada):
    d, n = w_ada.shape
    tn = _tile(n, 512)
    return pl.pallas_call(
        _adaln_kernel,
        grid=(n // tn,),
        in_specs=[pl.BlockSpec((8, d), lambda j: (0, 0)),
                  pl.BlockSpec((d, tn), lambda j: (0, j)),
                  pl.BlockSpec((1, tn), lambda j: (0, j))],
        out_specs=pl.BlockSpec((8, tn), lambda j: (0, j)),
        out_shape=jax.ShapeDtypeStruct((8, n), F32),
        compiler_params=_cparams(("arbitrary",)),
        name="adaln",
    )(cond8, w_ada, b_ada.reshape(1, n))


def _norm_mod(xv, w, shift, scale):
    ms = jnp.mean(xv * xv, axis=-1, keepdims=True)
    h = xv * lax.rsqrt(ms + EPS) * w
    return h * (1.0 + scale) + shift


def _norm_mix_kernel(x_ref, c_ref, w_ref, m_ref, o_ref, *, n_x_tiles):
    i = pl.program_id(0)

    def emit(src):
        o_ref[...] = _norm_mod(src[...], w_ref[...], m_ref[0, 0:1, :], m_ref[0, 1:2, :]).astype(o_ref.dtype)

    @pl.when(i < n_x_tiles)
    def _():
        emit(x_ref)

    @pl.when(i >= n_x_tiles)
    def _():
        emit(c_ref)


def _norm_mix(x2, c2, w, mod3, seq):
    mx, d = x2.shape
    mc = c2.shape[0]
    tm = _tile(min(seq, mc), 256)
    nx, nc = mx // tm, mc // tm
    tpb = seq // tm
    return pl.pallas_call(
        functools.partial(_norm_mix_kernel, n_x_tiles=nx),
        grid=(nx + nc,),
        in_specs=[pl.BlockSpec((tm, d), lambda i: (jnp.minimum(i, nx - 1), 0)),
                  pl.BlockSpec((tm, d), lambda i: (jnp.maximum(i - nx, 0), 0)),
                  pl.BlockSpec((1, d), lambda i: (0, 0)),
                  pl.BlockSpec((1, N_MOD, d), lambda i: (jnp.where(i < nx, i // tpb, 2), 0, 0))],
        out_specs=pl.BlockSpec((tm, d), lambda i: (i, 0)),
        out_shape=jax.ShapeDtypeStruct((mx + mc, d), BF16),
        compiler_params=_cparams(("arbitrary",)),
        name="norm_mix",
    )(x2, c2, w.reshape(1, d), mod3)


def _mm_kernel(a_ref, w_hbm, *rest, epilogue, col0, tn, n_col_tiles):
    extra, o_ref, stage_ref, wbf_ref, sem = rest[:-4], rest[-4], rest[-3], rest[-2], rest[-1]
    j, i = pl.program_id(0), pl.program_id(1)

    def w_copy(jj):
        col = pl.multiple_of(col0 + jj * tn, LANES)
        return pltpu.make_async_copy(w_hbm.at[:, pl.ds(col, tn)], stage_ref, sem)

    @pl.when((i == 0) & (j == 0))
    def _():
        w_copy(0).start()

    @pl.when(i == 0)
    def _():
        w_copy(j).wait()
        wbf_ref[...] = stage_ref[...].astype(BF16)

        @pl.when(j + 1 < n_col_tiles)
        def _():
            w_copy(j + 1).start()

    t = jnp.dot(a_ref[...], wbf_ref[...], preferred_element_type=F32)
    o_ref[...] = epilogue(t, *extra).astype(o_ref.dtype)


def _ep_plain(t):
    return t


def _ep_gate(t, g_ref):
    return jax.nn.sigmoid(g_ref[...].astype(F32)) * t


def _ep_gate_add(t, g_ref, prev_ref):
    return prev_ref[...].astype(F32) + jax.nn.sigmoid(g_ref[...].astype(F32)) * t


def _ep_residual(t, x_ref, m_ref, *, gate_row):
    return x_ref[...] + m_ref[0, gate_row:gate_row + 1, :] * t


def _mm(a, w, col0, ncols, m_rows, tm, tn, out_dtype, name, epilogue=_ep_plain, extra=(), extra_specs=()):
    k = a.shape[1]
    assert ncols % tn == 0 and m_rows % tm == 0 and col0 % LANES == 0
    return pl.pallas_call(
        functools.partial(_mm_kernel, epilogue=epilogue, col0=col0, tn=tn, n_col_tiles=ncols // tn),
        grid=(ncols // tn, m_rows // tm),
        in_specs=[pl.BlockSpec((tm, k), lambda j, i: (i, 0)), pl.BlockSpec(memory_space=pl.ANY)] + list(extra_specs),
        out_specs=pl.BlockSpec((tm, tn), lambda j, i: (i, j)),
        out_shape=jax.ShapeDtypeStruct((m_rows, ncols), out_dtype),
        scratch_shapes=[pltpu.VMEM((k, tn), F32), pltpu.VMEM((k, tn), BF16), pltpu.SemaphoreType.DMA(())],
        compiler_params=_cparams(("arbitrary", "arbitrary")),
        name=name,
    )(a, w, *extra)


def _qk_post_kernel(x_ref, w_ref, cos_ref, sin_ref, o_ref, *, n_heads, scale):
    lane = lax.broadcasted_iota(jnp.int32, (x_ref.shape[0], HEAD_DIM), 1)
    first = (lane % (HEAD_DIM // 2)) < (HEAD_DIM // 4)
    for h in range(n_heads):
        sl = slice(h * HEAD_DIM, (h + 1) * HEAD_DIM)
        xh = x_ref[:, sl].astype(F32)
        ms = jnp.mean(xh * xh, axis=-1, keepdims=True)
        y = xh * lax.rsqrt(ms + EPS) * w_ref[...]
        partner = jnp.where(first, pltpu.roll(y, HEAD_DIM - HEAD_DIM // 4, 1), pltpu.roll(y, HEAD_DIM // 4, 1))
        y = y * cos_ref[...] + partner * sin_ref[...]
        o_ref[:, sl] = (y * scale).astype(o_ref.dtype)


def _qk_post(src, col_blk0, n_heads_total, rows, w, cos_t, sin_t, seq, n_x_rows, scale, name):
    tm = _tile(min(seq, rows), 256)
    hpb = 4
    tpb = seq // tm
    nx = n_x_rows // tm
    tab_map = lambda i, j: (jnp.where(i < nx, i % tpb, tpb), 0)
    return pl.pallas_call(
        functools.partial(_qk_post_kernel, n_heads=hpb, scale=scale),
        grid=(rows // tm, n_heads_total // hpb),
        in_specs=[pl.BlockSpec((tm, hpb * HEAD_DIM), lambda i, j: (i, col_blk0 + j)),
                  pl.BlockSpec((1, HEAD_DIM), lambda i, j: (0, 0)),
                  pl.BlockSpec((tm, HEAD_DIM), tab_map),
                  pl.BlockSpec((tm, HEAD_DIM), tab_map)],
        out_specs=pl.BlockSpec((tm, hpb * HEAD_DIM), lambda i, j: (i, j)),
        out_shape=jax.ShapeDtypeStruct((rows, n_heads_total * HEAD_DIM), BF16),
        compiler_params=_cparams(("arbitrary", "arbitrary")),
        name=name,
    )(src, w.reshape(1, HEAD_DIM), cos_t, sin_t)


def _rope_tables(seq, tm):
    rows = seq // GRID_W
    row_pos = jnp.repeat(jnp.arange(rows, dtype=F32), GRID_W)
    col_pos = (jnp.arange(seq) % GRID_W).astype(F32)
    inv_freq = ROPE_THETA ** (-jnp.arange(ROPE_AXIS_FREQS, dtype=F32) / ROPE_AXIS_FREQS)
    ar = row_pos[:, None] * inv_freq
    ac = col_pos[:, None] * inv_freq
    cos_t = jnp.concatenate([jnp.cos(ar), jnp.cos(ar), jnp.cos(ac), jnp.cos(ac)], axis=-1)
    sin_t = jnp.concatenate([-jnp.sin(ar), jnp.sin(ar), -jnp.sin(ac), jnp.sin(ac)], axis=-1)
    cos_t = jnp.concatenate([cos_t, jnp.ones((tm, HEAD_DIM), F32)], axis=0)
    sin_t = jnp.concatenate([sin_t, jnp.zeros((tm, HEAD_DIM), F32)], axis=0)
    return cos_t, sin_t


def _attn_kernel(q_ref, wq_ref, cos_ref, sin_ref, kx_ref, kc_ref, vx_ref, vc_ref, o_ref):
    nt = (((1,), (1,)), ((), ()))
    kx, kc = kx_ref[...], kc_ref[...]

    def with_ones(v):
        lane = lax.broadcasted_iota(jnp.int32, v.shape, 1)
        return jnp.concatenate([v, jnp.where(lane == 0, 1.0, 0.0).astype(v.dtype)], axis=1)

    vx, vc = with_ones(vx_ref[...]), with_ones(vc_ref[...])
    tq = q_ref.shape[0]
    rows = tq // ATTN_ROW_SPLIT
    units = [(g, r) for g in range(GQA_GROUP) for r in range(ATTN_ROW_SPLIT)]

    lane = lax.broadcasted_iota(jnp.int32, (rows, HEAD_DIM), 1)
    first = (lane % (HEAD_DIM // 2)) < (HEAD_DIM // 4)

    def scores(u):
        g, r = u
        rs = slice(r * rows, (r + 1) * rows)
        xh = q_ref[rs, g * HEAD_DIM:(g + 1) * HEAD_DIM].astype(F32)
        y = xh * lax.rsqrt(jnp.mean(xh * xh, axis=-1, keepdims=True) + EPS) * wq_ref[...]
        partner = jnp.where(first, pltpu.roll(y, HEAD_DIM - HEAD_DIM // 4, 1), pltpu.roll(y, HEAD_DIM // 4, 1))
        q = (y * cos_ref[rs, :] + partner * sin_ref[rs, :]).astype(BF16)
        return (lax.dot_general(q, kx, nt, preferred_element_type=F32),
                lax.dot_general(q, kc, nt, preferred_element_type=F32))

    nxt = scores(units[0])
    for n, (g, r) in enumerate(units):
        s1, s2 = nxt
        if n + 1 < len(units):
            nxt = scores(units[n + 1])
        m = jnp.maximum(jnp.max(s1, axis=-1, keepdims=True), jnp.max(s2, axis=-1, keepdims=True))
        p1 = jnp.exp2(s1 - m).astype(BF16)
        p2 = jnp.exp2(s2 - m).astype(BF16)
        o = jnp.dot(p1, vx, preferred_element_type=F32) + jnp.dot(p2, vc, preferred_element_type=F32)
        o_ref[r * rows:(r + 1) * rows, g * HEAD_DIM:(g + 1) * HEAD_DIM] = (
            o[:, :HEAD_DIM] / o[:, HEAD_DIM:HEAD_DIM + 1]).astype(o_ref.dtype)


def _attention(p3, wq, cos_t, sin_t, k, p1, batch, seq, ctx_len):
    tq = _tile(seq, ATTN_TQ)
    qpb = seq // tq
    gw = GQA_GROUP * HEAD_DIM
    cblk0 = batch * seq // ctx_len
    return pl.pallas_call(
        _attn_kernel,
        grid=(batch, N_KV_HEADS, qpb),
        in_specs=[pl.BlockSpec((tq, gw), lambda b, h, i: (b * qpb + i, h)),
                  pl.BlockSpec((1, HEAD_DIM), lambda b, h, i: (0, 0)),
                  pl.BlockSpec((tq, HEAD_DIM), lambda b, h, i: (i, 0)),
                  pl.BlockSpec((tq, HEAD_DIM), lambda b, h, i: (i, 0)),
                  pl.BlockSpec((seq, HEAD_DIM), lambda b, h, i: (b, h)),
                  pl.BlockSpec((ctx_len, HEAD_DIM), lambda b, h, i: (cblk0 + b, h)),
                  pl.BlockSpec((seq, HEAD_DIM), lambda b, h, i: (b, N_KV_HEADS + h)),
                  pl.BlockSpec((ctx_len, HEAD_DIM), lambda b, h, i: (cblk0 + b, N_KV_HEADS + h))],
        out_specs=pl.BlockSpec((tq, gw), lambda b, h, i: (b * qpb + i, h)),
        out_shape=jax.ShapeDtypeStruct((batch * seq, N_Q_HEADS * HEAD_DIM), BF16),
        compiler_params=_cparams(("arbitrary", "arbitrary", "arbitrary")),
        name="attention",
    )(p3, wq.reshape(1, HEAD_DIM), cos_t, sin_t, k, k, p1, p1)


def _conv_kernel(prev_ref, cur_ref, next_ref, w_ref, b_ref, o_ref, buf_ref, *, tl, x_tiles, x_tpb, c_tpb):
    i = pl.program_id(0)
    j = jnp.where(i < x_tiles, i % x_tpb, (i - x_tiles) % c_tpb)
    n = jnp.where(i < x_tiles, x_tpb, c_tpb)
    buf_ref[0:HALO, :] = jnp.where(j == 0, 0.0, prev_ref[...].astype(F32))
    buf_ref[HALO:HALO + tl, :] = cur_ref[...].astype(F32)
    buf_ref[HALO + tl:, :] = jnp.where(j == n - 1, 0.0, next_ref[...].astype(F32))
    pad = CONV_W // 2
    xall = buf_ref[...]
    n_rows = xall.shape[0]
    acc = b_ref[...] + w_ref[pad:pad + 1, :] * xall[HALO:HALO + tl]
    for t in range(CONV_W):
        if t != pad:
            acc = acc + w_ref[t:t + 1, :] * pltpu.roll(xall, (pad - t) % n_rows, 0)[HALO:HALO + tl]
    o_ref[...] = _silu(acc).astype(o_ref.dtype)


def _conv_silu(p1, col0, ncols, conv_w, conv_b, seq, ctx_len, n_x_rows):
    rows = p1.shape[0]
    tl = _tile(min(seq, ctx_len), 256)
    tc = _tile(ncols, 2048)
    assert col0 % tc == 0
    cb0 = col0 // tc
    hb = tl // HALO
    last_hb = rows // HALO - 1
    return pl.pallas_call(
        functools.partial(_conv_kernel, tl=tl, x_tiles=n_x_rows // tl, x_tpb=seq // tl, c_tpb=ctx_len // tl),
        grid=(rows // tl, ncols // tc),
        in_specs=[pl.BlockSpec((HALO, tc), lambda i, j: (jnp.maximum(i * hb - 1, 0), cb0 + j)),
                  pl.BlockSpec((tl, tc), lambda i, j: (i, cb0 + j)),
                  pl.BlockSpec((HALO, tc), lambda i, j: (jnp.minimum((i + 1) * hb, last_hb), cb0 + j)),
                  pl.BlockSpec((CONV_W, tc), lambda i, j: (0, j)),
                  pl.BlockSpec((1, tc), lambda i, j: (0, j))],
        out_specs=pl.BlockSpec((tl, tc), lambda i, j: (i, j)),
        out_shape=jax.ShapeDtypeStruct((rows, ncols), BF16),
        scratch_shapes=[pltpu.VMEM((tl + 2 * HALO, tc), F32)],
        compiler_params=_cparams(("arbitrary", "arbitrary")),
        name="conv_silu",
    )(p1, p1, p1, conv_w, conv_b.reshape(1, ncols))


def _dt_prep_kernel(raw_ref, bias_ref, alog_ref, acum_ref, dt_ref, *, n_heads):
    v = raw_ref[...] + bias_ref[...]
    dt = jnp.maximum(v, 0.0) + jnp.log(1.0 + jnp.exp(-jnp.abs(v)))
    dta = dt * (-jnp.exp(alog_ref[...]))
    r = lax.broadcasted_iota(jnp.int32, (CHUNK, CHUNK), 0)
    c = lax.broadcasted_iota(jnp.int32, (CHUNK, CHUNK), 1)
    tril = jnp.where(r >= c, 1.0, 0.0).astype(BF16)
    triu = jnp.where(r <= c, 1.0, 0.0).astype(BF16)
    lane = lax.broadcasted_iota(jnp.int32, dta.shape, 1)
    acum = jnp.where(lane < n_heads, _dot_exact_lhs(tril, dta), _dot_exact_lhs(triu, dta))
    acum_ref[0] = acum.T
    dt_ref[0] = dt.T


def _dt_prep(dt_raw, bias2, alog2, n_heads):
    rows, w = dt_raw.shape
    nch = rows // CHUNK
    out = jax.ShapeDtypeStruct((nch, w, CHUNK), F32)
    return pl.pallas_call(
        functools.partial(_dt_prep_kernel, n_heads=n_heads),
        grid=(nch,),
        in_specs=[pl.BlockSpec((CHUNK, w), lambda i: (i, 0)),
                  pl.BlockSpec((1, w), lambda i: (0, 0)),
                  pl.BlockSpec((1, w), lambda i: (0, 0))],
        out_specs=[pl.BlockSpec((1, w, CHUNK), lambda i: (i, 0, 0))] * 2,
        out_shape=[out, out],
        compiler_params=_cparams(("arbitrary",)),
        name="dt_prep",
    )(dt_raw, bias2, alog2)


def _split3_f32(v):
    hi = v.astype(BF16).astype(F32)
    r1 = v - hi
    mid = r1.astype(BF16).astype(F32)
    lo = (r1 - mid).astype(BF16).astype(F32)
    return [hi, mid, lo]


def _ssd_consts(hpg):
    p = SSD_HEAD_DIM
    gw = hpg * p
    k = np.arange(CHUNK)[:, None]

    def expand(base, width, per):
        col_head = (np.arange(width) // per)[None, :]
        kk = k - base
        return ((kk >= 0) & (kk < 3 * hpg) & (kk % hpg == col_head)).astype(np.float32)

    mats = [expand(0, gw, p), expand(3 * hpg, gw, p), expand(9 * hpg, gw, p), expand(6 * hpg, hpg * CHUNK, CHUNK)]
    return jnp.asarray(np.concatenate(mats, axis=1), dtype=BF16)


def _ssd_kernel(xf_ref, bf_ref, cf_ref, af_ref, df_ref, xb_ref, bb_ref, cb_ref, ab_ref, db_ref, h0_ref, k_ref,
                yf_ref, yb_ref, hfin_ref, h_scr, *, hpg, gps):
    k = pl.program_id(2)
    nck = pl.num_programs(2)
    p = SSD_HEAD_DIM
    gw = hpg * p

    @pl.when(k == 0)
    def _():
        h_scr[...] = h0_ref[0]

    ri = lax.broadcasted_iota(jnp.int32, (CHUNK, CHUNK), 0)
    ci = lax.broadcasted_iota(jnp.int32, (CHUNK, CHUNK), 1)
    low_half = (lax.broadcasted_iota(jnp.int32, (CHUNK, gw), 1) % (2 * p)) < p
    nt = (((1,), (1,)), ((), ()))
    e_exp_a, e_to_end, e_gain = k_ref[:, 0:gw], k_ref[:, gw:2 * gw], k_ref[:, 2 * gw:3 * gw]
    e_col_a = k_ref[:, 3 * gw:]
    pad_rows = jnp.zeros((CHUNK - 12 * hpg, CHUNK), F32)

    dirs = ((xf_ref, bf_ref, cf_ref, af_ref, df_ref, yf_ref), (xb_ref, bb_ref, cb_ref, ab_ref, db_ref, yb_ref))
    for gi, d in [(gi, d) for gi in range(gps) for d in range(2)]:
        x_ref, b_ref, c_ref, a_ref, dt_ref, y_ref = dirs[d]
        gcols = slice(gi * gw, (gi + 1) * gw)
        ncols = slice(gi * SSD_STATE, (gi + 1) * SSD_STATE)
        row_a = a_ref[0, gi * hpg:(gi + 1) * hpg, :]
        row_dt = dt_ref[0, gi * hpg:(gi + 1) * hpg, :]
        if d == 0:
            mask = ri >= ci
            tot = row_a[:, CHUNK - 1:CHUNK]
        else:
            mask = ri <= ci
            tot = row_a[:, 0:1]
        exp_a = jnp.exp(row_a)
        to_end = jnp.exp(tot - row_a) * row_dt
        gain = jnp.broadcast_to(jnp.exp(tot), row_a.shape)
        table = jnp.concatenate(_split3_f32(exp_a) + _split3_f32(to_end) + _split3_f32(row_a)
                                + _split3_f32(gain) + [pad_rows], axis=0)
        tab_t = table.T.astype(BF16)
        exp_a_full = jnp.dot(tab_t, e_exp_a, preferred_element_type=F32)
        to_end_full = jnp.dot(tab_t, e_to_end, preferred_element_type=F32)
        gain_full = jnp.dot(tab_t[0:16], e_gain, preferred_element_type=F32)[0:1]
        col_a = jnp.dot(tab_t, e_col_a, preferred_element_type=F32)
        bc = b_ref[:, ncols]
        cc = c_ref[:, ncols]
        cb = lax.dot_general(cc, bc, nt, preferred_element_type=F32)
        x32 = x_ref[:, gcols].astype(F32)
        x_lo = jnp.where(low_half, x32, 0.0).astype(BF16)
        x_hi = jnp.where(low_half, 0.0, x32).astype(BF16)
        y_parts = []
        for q in range(hpg // 2):
            ws = []
            for r in (2 * q, 2 * q + 1):
                seg = col_a[:, r * CHUNK:(r + 1) * CHUNK] - row_a[r:r + 1, :]
                ws.append((cb * jnp.exp(jnp.where(mask, seg, -jnp.inf)) * row_dt[r:r + 1, :]).astype(BF16))
            sl = slice(q * 2 * p, (q + 1) * 2 * p)
            y_parts.append(jnp.dot(jnp.concatenate(ws, axis=1), jnp.concatenate([x_lo[:, sl], x_hi[:, sl]], axis=0),
                                   preferred_element_type=F32))
        y_state = jnp.dot(cc, h_scr[gi, d].astype(BF16), preferred_element_type=F32)
        y_ref[:, gcols] = (jnp.concatenate(y_parts, axis=1) + y_state * exp_a_full).astype(y_ref.dtype)
        bct = bc.astype(F32).T.astype(BF16)
        upd = jnp.dot(bct, (x32 * to_end_full).astype(BF16), preferred_element_type=F32)
        h_scr[gi, d] = h_scr[gi, d] * gain_full + upd

    @pl.when(k == nck - 1)
    def _():
        hfin_ref[0] = h_scr[...]


def _ssd(xbc, acum_t, dt_t, h0, consts, batch, length, chunk0, hpg, d_inner):
    nck = length // CHUNK
    gw = hpg * SSD_HEAD_DIM
    assert hpg % 2 == 0 and 2 * SSD_HEAD_DIM == LANES and 12 * hpg <= CHUNK
    gps = SSD_GROUPS_PER_STEP
    n_gsteps = SSD_GROUPS // gps
    b_blk0 = d_inner // (gps * SSD_STATE)
    c_blk0 = b_blk0 + n_gsteps
    assert SSD_GROUPS % gps == 0 and d_inner % (gps * SSD_STATE) == 0
    fwd = lambda b, g, k: chunk0 + b * nck + k
    bwd = lambda b, g, k: chunk0 + b * nck + (nck - 1 - k)

    def dir_specs(ch, d):
        return [pl.BlockSpec((CHUNK, gps * gw), lambda b, g, k: (ch(b, g, k), g)),
                pl.BlockSpec((CHUNK, gps * SSD_STATE), lambda b, g, k: (ch(b, g, k), b_blk0 + g)),
                pl.BlockSpec((CHUNK, gps * SSD_STATE), lambda b, g, k: (ch(b, g, k), c_blk0 + g)),
                pl.BlockSpec((1, gps * hpg, CHUNK), lambda b, g, k: (ch(b, g, k), d * n_gsteps + g, 0)),
                pl.BlockSpec((1, gps * hpg, CHUNK), lambda b, g, k: (ch(b, g, k), d * n_gsteps + g, 0))]

    st_spec = pl.BlockSpec((1, gps, 2, SSD_STATE, gw), lambda b, g, k: (b, g, 0, 0, 0))
    y_shape = jax.ShapeDtypeStruct((batch * length, d_inner), BF16)
    return pl.pallas_call(
        functools.partial(_ssd_kernel, hpg=hpg, gps=gps),
        grid=(batch, n_gsteps, nck),
        in_specs=dir_specs(fwd, 0) + dir_specs(bwd, 1) + [st_spec, pl.BlockSpec(consts.shape, lambda b, g, k: (0, 0))],
        out_specs=[pl.BlockSpec((CHUNK, gps * gw), lambda b, g, k: (b * nck + k, g)),
                   pl.BlockSpec((CHUNK, gps * gw), lambda b, g, k: (b * nck + (nck - 1 - k), g)),
                   st_spec],
        out_shape=[y_shape, y_shape, jax.ShapeDtypeStruct(h0.shape, F32)],
        scratch_shapes=[pltpu.VMEM((gps, 2, SSD_STATE, gw), F32)],
        compiler_params=_cparams(("arbitrary", "arbitrary", "arbitrary")),
        name="ssd",
    )(xbc, xbc, xbc, acum_t, dt_t, xbc, xbc, xbc, acum_t, dt_t, h0, consts)


def _ssd_norm_kernel(yf_ref, yb_ref, xs_ref, z_ref, ds_ref, w_ref, o_ref):
    y = yf_ref[...].astype(F32) + yb_ref[...].astype(F32) + ds_ref[...] * xs_ref[...].astype(F32)
    y = y * _silu(z_ref[...].astype(F32))
    ms = jnp.mean(y * y, axis=-1, keepdims=True)
    o_ref[...] = (y * lax.rsqrt(ms + EPS) * w_ref[...]).astype(o_ref.dtype)


def _ssd_norm(yf, yb, xbc, p3, z_blk, dskip_row, w):
    m, d = yf.shape
    tm = _tile(m, 256)
    row = lambda i: (i, 0)
    return pl.pallas_call(
        _ssd_norm_kernel,
        grid=(m // tm,),
        in_specs=[pl.BlockSpec((tm, d), row), pl.BlockSpec((tm, d), row), pl.BlockSpec((tm, d), row),
                  pl.BlockSpec((tm, d), lambda i: (i, z_blk)),
                  pl.BlockSpec((1, d), lambda i: (0, 0)), pl.BlockSpec((1, d), lambda i: (0, 0))],
        out_specs=pl.BlockSpec((tm, d), row),
        out_shape=jax.ShapeDtypeStruct((m, d), BF16),
        compiler_params=_cparams(("arbitrary",)),
        name="ssd_norm",
    )(yf, yb, xbc, p3, dskip_row, w.reshape(1, d))


def _merge(attn, ynorm, w1, w2, p3, g1_col0, g2_col0):
    m, _ = attn.shape
    n = w1.shape[1]
    tm, tn = _tile(m, 512), _tile(n, 1024)
    blk = lambda c0: pl.BlockSpec((tm, tn), lambda j, i: (i, c0 // tn + j))
    part = _mm(attn, w1, 0, n, m, tm, tn, BF16, "merge_attn", _ep_gate, (p3,), (blk(g1_col0),))
    return _mm(ynorm, w2, 0, n, m, tm, tn, BF16, "merge_ssd", _ep_gate_add, (p3, part), (blk(g2_col0), blk(0)))


def _out_proj(merged, w, x2, mod3, seq, gate_row):
    m, _ = merged.shape
    n = w.shape[1]
    tm, tn = _tile(seq, 512), _tile(n, 1024)
    tpb = seq // tm
    return _mm(merged, w, 0, n, m, tm, tn, F32, "out_proj", functools.partial(_ep_residual, gate_row=gate_row),
               (x2, mod3), (pl.BlockSpec((tm, tn), lambda j, i: (i, j)),
                            pl.BlockSpec((1, N_MOD, tn), lambda j, i: (i // tpb, 0, j))))


def _ffn_norm_route_kernel(x_ref, w_ref, m_ref, wr_ref, br_ref, h_ref, route_ref):
    h = _norm_mod(x_ref[...], w_ref[...], m_ref[0, 3:4, :], m_ref[0, 4:5, :])
    h_ref[...] = _pack_pairs(h)
    h_hi = h.astype(BF16)
    h_lo = (h - h_hi.astype(F32)).astype(BF16)
    wr = wr_ref[...]
    w_hi = wr.astype(BF16)
    w_lo = (wr - w_hi.astype(F32)).astype(BF16)
    logits = (jnp.dot(h_hi, w_hi, preferred_element_type=F32) + jnp.dot(h_hi, w_lo, preferred_element_type=F32)
              + jnp.dot(h_lo, w_hi, preferred_element_type=F32)) + br_ref[...]
    lane = lax.broadcasted_iota(jnp.int32, logits.shape, 1).astype(F32)
    big = float(LANES)
    neg = -jnp.inf
    gl = jnp.where(lane < N_GROUPS, logits, neg)
    gmax = jnp.max(gl, axis=-1, keepdims=True)
    gidx = jnp.min(jnp.where(gl == gmax, lane, big), axis=-1, keepdims=True)
    g_prob = 1.0 / jnp.sum(jnp.exp(gl - gmax), axis=-1, keepdims=True)
    lo = N_GROUPS + gidx * EXPERTS_PER_GROUP
    el = jnp.where((lane >= lo) & (lane < lo + EXPERTS_PER_GROUP), logits, neg)
    m1 = jnp.max(el, axis=-1, keepdims=True)
    i1 = jnp.min(jnp.where(el == m1, lane, big), axis=-1, keepdims=True)
    el2 = jnp.where(lane == i1, neg, el)
    m2 = jnp.max(el2, axis=-1, keepdims=True)
    i2 = jnp.min(jnp.where(el2 == m2, lane, big), axis=-1, keepdims=True)
    z = jnp.sum(jnp.exp(el - m1), axis=-1, keepdims=True)
    p1 = 1.0 / z
    p2 = jnp.exp(m2 - m1) / z
    w1 = g_prob * p1 / (p1 + p2)
    w2 = g_prob * p2 / (p1 + p2)
    route = jnp.where(lane == 0, i1 - N_GROUPS, jnp.where(lane == 1, i2 - N_GROUPS,
                      jnp.where(lane == 2, w1, jnp.where(lane == 3, w2, 0.0))))
    route_ref[...] = route


def _ffn_norm_route(x1, w, mod3, wr, br, seq):
    m, d = x1.shape
    tm = _tile(seq, 256)
    tpb = seq // tm
    return pl.pallas_call(
        _ffn_norm_route_kernel,
        grid=(m // tm,),
        in_specs=[pl.BlockSpec((tm, d), lambda i: (i, 0)),
                  pl.BlockSpec((1, d), lambda i: (0, 0)),
                  pl.BlockSpec((1, N_MOD, d), lambda i: (i // tpb, 0, 0)),
                  pl.BlockSpec((d, LANES), lambda i: (0, 0)),
                  pl.BlockSpec((1, LANES), lambda i: (0, 0))],
        out_specs=[pl.BlockSpec((tm, d // 2), lambda i: (i, 0)), pl.BlockSpec((tm, LANES), lambda i: (i, 0))],
        out_shape=[jax.ShapeDtypeStruct((m, d // 2), jnp.uint32), jax.ShapeDtypeStruct((m, LANES), F32)],
        compiler_params=_cparams(("arbitrary",)),
        name="ffn_norm_route",
    )(x1, w.reshape(1, d), mod3, wr, br)


def _gather_kernel(tok_ref, nv_ref, src_ref, o_ref, buf_ref, sem, *, blk):
    i = pl.program_id(0)
    n = pl.num_programs(0)
    groups = lambda b: (nv_ref[b] + ROW_GROUP - 1) // ROW_GROUP

    def request(b, slot):
        def issue(g, c):
            for u in range(ROW_GROUP):
                r = g * ROW_GROUP + u
                pltpu.make_async_copy(src_ref.at[pl.ds(tok_ref[b * blk + r], 1)], buf_ref.at[slot, pl.ds(r, 1)],
                                      sem.at[slot]).start(priority=u % DMA_PRIORITIES)
            return c

        lax.fori_loop(0, groups(b), issue, 0)

    @pl.when(i == 0)
    def _():
        request(0, 0)

    @pl.when(i + 1 < n)
    def _():
        request(i + 1, (i + 1) % 2)

    slot = i % 2
    n_groups = groups(i)

    def zero(g, c):
        r0 = pl.multiple_of(g * ROW_GROUP, ROW_GROUP)
        buf_ref[slot, pl.ds(r0, ROW_GROUP), :] = jnp.zeros((ROW_GROUP, buf_ref.shape[2]), buf_ref.dtype)
        return c

    lax.fori_loop(n_groups, blk // ROW_GROUP, zero, 0)

    def drain(g, c):
        pltpu.make_async_copy(src_ref.at[pl.ds(0, ROW_GROUP)], buf_ref.at[slot, pl.ds(0, ROW_GROUP)],
                              sem.at[slot]).wait()
        return c

    lax.fori_loop(0, n_groups, drain, 0)
    lo, hi = _unpack_pairs(buf_ref[slot])
    half = lo.shape[1]
    o_ref[:, :half] = lo.astype(o_ref.dtype)
    o_ref[:, half:] = hi.astype(o_ref.dtype)


def _gather_rows(row_tok, n_valid, src, blk):
    n_rows = row_tok.shape[0]
    dp = src.shape[1]
    d = 2 * dp
    return pl.pallas_call(
        functools.partial(_gather_kernel, blk=blk),
        grid_spec=pltpu.PrefetchScalarGridSpec(
            num_scalar_prefetch=2,
            grid=(n_rows // blk,),
            in_specs=[pl.BlockSpec(memory_space=pl.ANY)],
            out_specs=pl.BlockSpec((blk, d), lambda i, tok, nv: (i, 0)),
            scratch_shapes=[pltpu.VMEM((2, blk, dp), src.dtype), pltpu.SemaphoreType.DMA((2,))]),
        out_shape=jax.ShapeDtypeStruct((n_rows, d), BF16),
        compiler_params=_cparams(("arbitrary",)),
        name="moe_gather",
    )(row_tok, n_valid, src)


def _expert_mm_kernel(ie_ref, it_ref, ib_ref, if_ref, iv_ref, ne_ref, nt_ref, hn_ref, a_ref, *rest, n_w, tn, compute):
    w_hbm = rest[:n_w]
    o_ref = rest[n_w]
    stage = rest[n_w + 1:2 * n_w + 1]
    w_bf = rest[2 * n_w + 1:3 * n_w + 1]
    sem = rest[3 * n_w + 1]
    i = pl.program_id(0)

    def copies(e, t):
        col = pl.multiple_of(t * tn, LANES)
        return [pltpu.make_async_copy(w_hbm[n].at[e, :, pl.ds(col, tn)], stage[n], sem.at[n]) for n in range(n_w)]

    @pl.when(i == 0)
    def _():
        for cp in copies(ie_ref[0], it_ref[0]):
            cp.start()

    @pl.when(if_ref[i] == 1)
    def _():
        for cp in copies(ie_ref[i], it_ref[i]):
            cp.wait()
        for n in range(n_w):
            w_bf[n][...] = stage[n][...].astype(BF16)

        @pl.when(hn_ref[i] == 1)
        def _():
            for cp in copies(ne_ref[i], nt_ref[i]):
                cp.start()

    @pl.when(iv_ref[i] == 1)
    def _():
        o_ref[...] = compute(a_ref[...], [w[...] for w in w_bf]).astype(o_ref.dtype)

    @pl.when(iv_ref[i] == 0)
    def _():
        o_ref[...] = jnp.zeros(o_ref.shape, o_ref.dtype)


def _expert_mm(items, a, weights, blk, tn, compute, out_dtype, name, out_div=1):
    n_items = items[0].shape[0]
    n_rows, k = a.shape
    n = weights[0].shape[2]
    n_w = len(weights)
    assert out_div == 1 or tn == n
    imap_a = lambda i, ie, it, ib, *_: (ib[i], 0)
    imap_o = lambda i, ie, it, ib, *_: (ib[i], it[i])
    return pl.pallas_call(
        functools.partial(_expert_mm_kernel, n_w=n_w, tn=tn, compute=compute),
        grid_spec=pltpu.PrefetchScalarGridSpec(
            num_scalar_prefetch=len(items),
            grid=(n_items,),
            in_specs=[pl.BlockSpec((blk, k), imap_a)] + [pl.BlockSpec(memory_space=pl.ANY)] * n_w,
            out_specs=pl.BlockSpec((blk, tn // out_div), imap_o),
            scratch_shapes=([pltpu.VMEM((k, tn), F32)] * n_w + [pltpu.VMEM((k, tn), BF16)] * n_w
                            + [pltpu.SemaphoreType.DMA((n_w,))])),
        out_shape=jax.ShapeDtypeStruct((n_rows, n // out_div), out_dtype),
        compiler_params=_cparams(("arbitrary",)),
        name=name,
    )(*items, a, *weights)


def _up_compute(xb, ws):
    a = jnp.dot(xb, ws[0], preferred_element_type=F32)
    u = jnp.dot(xb, ws[1], preferred_element_type=F32)
    return _silu(a) * u


def _down_compute(hb, ws):
    return _pack_pairs(jnp.dot(hb, ws[0], preferred_element_type=F32))


def _combine_kernel(pos_ref, y_ref, x_ref, rt_ref, m_ref, o_ref, buf_ref, sem, *, tm):
    i = pl.program_id(0)
    n = pl.num_programs(0)

    def request(t, slot):
        def issue(g, c):
            for u in range(ROW_GROUP):
                r = g * ROW_GROUP + u
                for kk in range(TOP_K):
                    pltpu.make_async_copy(y_ref.at[pl.ds(pos_ref[TOP_K * (t * tm + r) + kk], 1)],
                                          buf_ref.at[slot, kk, pl.ds(r, 1)], sem.at[slot]).start(priority=kk % DMA_PRIORITIES)
            return c

        lax.fori_loop(0, tm // ROW_GROUP, issue, 0)

    @pl.when(i == 0)
    def _():
        request(0, 0)

    @pl.when(i + 1 < n)
    def _():
        request(i + 1, (i + 1) % 2)

    slot = i % 2
    for kk in range(TOP_K):
        pltpu.make_async_copy(y_ref.at[pl.ds(0, tm)], buf_ref.at[slot, kk], sem.at[slot]).wait()
    rt = rt_ref[...]
    lo0, hi0 = _unpack_pairs(buf_ref[slot, 0])
    lo1, hi1 = _unpack_pairs(buf_ref[slot, 1])
    half = lo0.shape[1]
    w0, w1 = rt[:, 2:3], rt[:, 3:4]
    o_ref[:, :half] = x_ref[:, :half] + m_ref[0, 5:6, :half] * (w0 * lo0 + w1 * lo1)
    o_ref[:, half:] = x_ref[:, half:] + m_ref[0, 5:6, half:] * (w0 * hi0 + w1 * hi1)


def _combine(pos, yexp, x1, route, mod3, seq):
    m, d = x1.shape
    tm = _tile(seq, 128)
    tpb = seq // tm
    return pl.pallas_call(
        functools.partial(_combine_kernel, tm=tm),
        grid_spec=pltpu.PrefetchScalarGridSpec(
            num_scalar_prefetch=1,
            grid=(m // tm,),
            in_specs=[pl.BlockSpec(memory_space=pl.ANY),
                      pl.BlockSpec((tm, d), lambda i, pos: (i, 0)),
                      pl.BlockSpec((tm, LANES), lambda i, pos: (i, 0)),
                      pl.BlockSpec((1, N_MOD, d), lambda i, pos: (i // tpb, 0, 0))],
            out_specs=pl.BlockSpec((tm, d), lambda i, pos: (i, 0)),
            scratch_shapes=[pltpu.VMEM((2, TOP_K, tm, d // 2), jnp.uint32), pltpu.SemaphoreType.DMA((2,))]),
        out_shape=jax.ShapeDtypeStruct((m, d), F32),
        compiler_params=_cparams(("arbitrary",)),
        name="moe_combine",
    )(pos, yexp, x1, route, mod3)


def _moe_plan(route, blk, n_tiles_up, n_tiles_down):
    t = route.shape[0]
    n_assign = t * TOP_K
    nb_max = n_assign // blk + N_EXPERTS
    flat_e = route[:, 0:TOP_K].astype(jnp.int32).reshape(-1)
    onehot = (flat_e[:, None] == jnp.arange(N_EXPERTS, dtype=jnp.int32)[None, :]).astype(jnp.int32)
    cum = jnp.cumsum(onehot, axis=0)
    rank = jnp.take_along_axis(cum, flat_e[:, None], axis=1)[:, 0] - 1
    counts = cum[-1]
    nblk = (counts + blk - 1) // blk
    pend = jnp.cumsum(nblk)
    pstart = pend - nblk
    dest = pstart[flat_e] * blk + rank
    row_tok = jnp.zeros((nb_max * blk,), jnp.int32).at[dest].set(jnp.arange(n_assign, dtype=jnp.int32) // TOP_K)
    total = pend[-1]
    unused = jnp.maximum(nb_max - total, 1)

    def items(n_tiles):
        j = jnp.arange(n_tiles * nb_max, dtype=jnp.int32)
        valid = j < n_tiles * total
        jj = jnp.minimum(j, n_tiles * total - 1)
        e = jnp.minimum(jnp.sum((jj[:, None] >= n_tiles * pend[None, :]).astype(jnp.int32), axis=1), N_EXPERTS - 1)
        local = jj - n_tiles * pstart[e]
        nb_e = jnp.maximum(nblk[e], 1)
        u = j - n_tiles * total
        tile = jnp.where(valid, local // nb_e, u // unused)
        b = jnp.where(valid, pstart[e] + local % nb_e, total + u % unused)
        first = valid & (local % nb_e == 0)
        nxt = j + nb_e
        has_next = first & (nxt < n_tiles * total)
        nxt = jnp.minimum(nxt, n_tiles * nb_max - 1)
        i32 = lambda v: v.astype(jnp.int32)
        return (e, i32(tile), i32(b), i32(first), i32(valid), e[nxt], i32(tile[nxt]), i32(has_next))

    blk_ids = jnp.arange(nb_max, dtype=jnp.int32)
    blk_e = jnp.minimum(jnp.sum((blk_ids[:, None] >= pend[None, :]).astype(jnp.int32), axis=1), N_EXPERTS - 1)
    n_valid = jnp.clip(counts[blk_e] - (blk_ids - pstart[blk_e]) * blk, 0, blk)
    n_valid = jnp.where(blk_ids < total, n_valid, 0).astype(jnp.int32)
    return row_tok, n_valid, dest, items(n_tiles_up), items(n_tiles_down)


def kernel(x, c, ctx, c_ctx, w_ada, b_ada, norm_mix_w, norm_ffn_w, w_in, q_norm_w, k_norm_w, conv_w, conv_b,
           a_log_f, a_log_b, dt_bias_f, dt_bias_b, d_skip, ssd_norm_w, w_attn_proj, w_ssd_proj, w_out,
           w_router_group, b_router_group, w_router_expert, b_router_expert, w_exp_gate, w_exp_up, w_exp_down):
    batch, seq, d = x.shape
    ctx_len = ctx.shape[1]
    assert w_ada.shape[0] == 1, "single layer: the context stream is read, never updated"
    d_inner = d
    n_ssd_heads = d_inner // SSD_HEAD_DIM
    hpg = n_ssd_heads // SSD_GROUPS
    kv_dim = N_KV_HEADS * HEAD_DIM
    q_dim = N_Q_HEADS * HEAD_DIM
    bc_dim = SSD_GROUPS * SSD_STATE
    xbc_dim = d_inner + 2 * bc_dim
    p1_cols = 2 * kv_dim + xbc_dim
    dt_cols = 2 * n_ssd_heads
    p3_col0 = p1_cols + dt_cols
    p3_cols = q_dim + 2 * d + d_inner
    assert dt_cols == LANES
    mx, mc = batch * seq, batch * ctx_len

    x2 = x.reshape(mx, d)
    c2 = ctx.reshape(mc, d)
    w_in0 = w_in[0]

    cond8 = jnp.zeros((8, d), F32).at[0:batch].set(c).at[batch].set(c_ctx)
    assert batch == 2
    mod3 = _adaln(cond8, w_ada[0], b_ada[0]).reshape(8, N_MOD, d)

    h_all = _norm_mix(x2, c2, norm_mix_w[0], mod3, seq)
    tm_all = 1088 if (mx + mc) % 1088 == 0 else _tile(mx + mc, 512)
    p1 = _mm(h_all, w_in0, 0, p1_cols, mx + mc, tm_all, 1024, BF16, "in_proj_kvx")
    dt_raw = _mm(h_all, w_in0, p1_cols, dt_cols, mx + mc, _tile(mx + mc, 512), dt_cols, F32, "in_proj_dt")
    p3 = _mm(h_all, w_in0, p3_col0, p3_cols, mx, _tile(mx, 1024), 1024, BF16, "in_proj_qgz")

    tmq = _tile(min(seq, mc), 256)
    cos_t, sin_t = _rope_tables(seq, tmq)
    k = _qk_post(p1, 0, N_KV_HEADS, mx + mc, k_norm_w[0], cos_t, sin_t, seq, mx, 1.0, "k_post")
    attn = _attention(p3, q_norm_w[0] * (HEAD_DIM ** -0.5 * LOG2_E), cos_t, sin_t, k, p1, batch, seq, ctx_len)

    xbc = _conv_silu(p1, 2 * kv_dim, xbc_dim, conv_w[0], conv_b[0], seq, ctx_len, mx)
    bias2 = jnp.concatenate([dt_bias_f[0], dt_bias_b[0]]).reshape(1, dt_cols)
    alog2 = jnp.concatenate([a_log_f[0], a_log_b[0]]).reshape(1, dt_cols)
    acum_t, dt_t = _dt_prep(dt_raw, bias2, alog2, n_ssd_heads)
    h_zero = jnp.zeros((batch, SSD_GROUPS, 2, SSD_STATE, hpg * SSD_HEAD_DIM), F32)
    ssd_consts = _ssd_consts(hpg)
    _, _, h_ctx = _ssd(xbc, acum_t, dt_t, h_zero, ssd_consts, batch, ctx_len, mx // CHUNK, hpg, d_inner)
    yf, yb, _ = _ssd(xbc, acum_t, dt_t, h_ctx, ssd_consts, batch, seq, 0, hpg, d_inner)
    dskip_row = jnp.repeat(d_skip[0], SSD_HEAD_DIM).reshape(1, d_inner)
    ynorm = _ssd_norm(yf, yb, xbc, p3, (q_dim + 2 * d) // d_inner, dskip_row, ssd_norm_w[0])

    merged = _merge(attn, ynorm, w_attn_proj[0], w_ssd_proj[0], p3, q_dim, q_dim + d)
    x1 = _out_proj(merged, w_out[0], x2, mod3, seq, 2)

    wr = jnp.zeros((d, LANES), F32).at[:, :N_GROUPS].set(w_router_group[0])
    wr = wr.at[:, N_GROUPS:N_GROUPS + N_EXPERTS].set(w_router_expert[0])
    br = jnp.zeros((1, LANES), F32).at[0, :N_GROUPS].set(b_router_group[0])
    br = br.at[0, N_GROUPS:N_GROUPS + N_EXPERTS].set(b_router_expert[0])
    h2, route = _ffn_norm_route(x1, norm_ffn_w[0], mod3, wr, br, seq)
    d_exp = w_exp_gate.shape[-1]
    tn_up, tn_down = _tile(d_exp, 512), _tile(d, 4096)
    row_tok, n_valid, dest, items_up, items_down = _moe_plan(route, MOE_BLK, d_exp // tn_up, d // tn_down)
    xg = _gather_rows(row_tok, n_valid, h2, MOE_BLK)
    hid = _expert_mm(items_up, xg, (w_exp_gate[0], w_exp_up[0]), MOE_BLK, tn_up, _up_compute, BF16, "moe_up")
    yexp = _expert_mm(items_down, hid, (w_exp_down[0],), MOE_BLK, tn_down, _down_compute, jnp.uint32, "moe_down",
                      out_div=2)
    out = _combine(dest, yexp, x1, route, mod3, seq)
    return out.reshape(batch, seq, d)
```

```python
import functools

import numpy as np
import jax
import jax.numpy as jnp
from jax import lax
from jax.experimental import pallas as pl
from jax.experimental.pallas import tpu as pltpu

F32 = jnp.float32
BF16 = jnp.bfloat16

N_MOD = 6
EPS = 1e-6
GRID_W = 64
N_Q_HEADS = 32
N_KV_HEADS = 8
HEAD_DIM = 128
GQA_GROUP = N_Q_HEADS // N_KV_HEADS
ROPE_THETA = 10000.0
ROPE_AXIS_FREQS = HEAD_DIM // 4
SSD_HEAD_DIM = 64
SSD_GROUPS = 8
SSD_STATE = 128
CONV_W = 5
CHUNK = 128
N_GROUPS = 4
EXPERTS_PER_GROUP = 8
N_EXPERTS = N_GROUPS * EXPERTS_PER_GROUP
TOP_K = 2

LOG2_E = 1.4426950408889634
LANES = 128
HALO = 16
SSD_GROUPS_PER_STEP = 8
MOE_BLK = 256
ROW_GROUP = 8
ATTN_TQ = 512
ATTN_ROW_SPLIT = 4
VMEM_LIMIT = 56 * 1024 * 1024


def _cparams(sem):
    return pltpu.CompilerParams(dimension_semantics=sem, vmem_limit_bytes=VMEM_LIMIT)


def _tile(n, pref):
    t = min(n, pref)
    while n % t:
        t //= 2
    return t


def _silu(v):
    return v * jax.nn.sigmoid(v)


def _split3(v):
    hi = v.astype(BF16)
    r1 = v - hi.astype(F32)
    mid = r1.astype(BF16)
    lo = (r1 - mid.astype(F32)).astype(BF16)
    return hi, mid, lo


def _dot_exact_lhs(m_bf16, v, dims=(((1,), (0,)), ((), ()))):
    hi, mid, lo = _split3(v)
    out = lax.dot_general(m_bf16, hi, dims, preferred_element_type=F32)
    out = out + lax.dot_general(m_bf16, mid, dims, preferred_element_type=F32)
    return out + lax.dot_general(m_bf16, lo, dims, preferred_element_type=F32)


def _adaln_kernel(c_ref, w_ref, b_ref, o_ref):
    s = _silu(c_ref[...]).astype(BF16)
    o_ref[...] = jnp.dot(s, w_ref[...].astype(BF16), preferred_element_type=F32) + b_ref[...]


def _adaln(cond8, w_ada, b_ada):
    d, n = w_ada.shape
    tn = _tile(n, 512)
    return pl.pallas_call(
        _adaln_kernel,
        grid=(n // tn,),
        in_specs=[pl.BlockSpec((8, d), lambda j: (0, 0)),
                  pl.BlockSpec((d, tn), lambda j: (0, j)),
                  pl.BlockSpec((1, tn), lambda j: (0, j))],
        out_specs=pl.BlockSpec((8, tn), lambda j: (0, j)),
        out_shape=jax.ShapeDtypeStruct((8, n), F32),
        compiler_params=_cparams(("arbitrary",)),
        name="adaln",
    )(cond8, w_ada, b_ada.reshape(1, n))


def _norm_mod(xv, w, shift, scale):
    ms = jnp.mean(xv * xv, axis=-1, keepdims=True)
    h = xv * lax.rsqrt(ms + EPS) * w
    return h * (1.0 + scale) + shift


def _norm_mix_kernel(x_ref, c_ref, w_ref, m_ref, o_ref, *, n_x_tiles):
    i = pl.program_id(0)

    def emit(src):
        o_ref[...] = _norm_mod(src[...], w_ref[...], m_ref[0, 0:1, :], m_ref[0, 1:2, :]).astype(o_ref.dtype)

    @pl.when(i < n_x_tiles)
    def _():
        emit(x_ref)

    @pl.when(i >= n_x_tiles)
    def _():
        emit(c_ref)


def _norm_mix(x2, c2, w, mod3, seq):
    mx, d = x2.shape
    mc = c2.shape[0]
    tm = _tile(min(seq, mc), 256)
    nx, nc = mx // tm, mc // tm
    tpb = seq // tm
    return pl.pallas_call(
        functools.partial(_norm_mix_kernel, n_x_tiles=nx),
        grid=(nx + nc,),
        in_specs=[pl.BlockSpec((tm, d), lambda i: (jnp.minimum(i, nx - 1), 0)),
                  pl.BlockSpec((tm, d), lambda i: (jnp.maximum(i - nx, 0), 0)),
                  pl.BlockSpec((1, d), lambda i: (0, 0)),
                  pl.BlockSpec((1, N_MOD, d), lambda i: (jnp.where(i < nx, i // tpb, 2), 0, 0))],
        out_specs=pl.BlockSpec((tm, d), lambda i: (i, 0)),
        out_shape=jax.ShapeDtypeStruct((mx + mc, d), BF16),
        compiler_params=_cparams(("arbitrary",)),
        name="norm_mix",
    )(x2, c2, w.reshape(1, d), mod3)


def _mm_kernel(a_ref, w_hbm, *rest, epilogue, col0, tn, n_col_tiles):
    extra, o_ref, stage_ref, wbf_ref, sem = rest[:-4], rest[-4], rest[-3], rest[-2], rest[-1]
    j, i = pl.program_id(0), pl.program_id(1)

    def w_copy(jj):
        col = pl.multiple_of(col0 + jj * tn, LANES)
        return pltpu.make_async_copy(w_hbm.at[:, pl.ds(col, tn)], stage_ref, sem)

    @pl.when((i == 0) & (j == 0))
    def _():
        w_copy(0).start()

    @pl.when(i == 0)
    def _():
        w_copy(j).wait()
        wbf_ref[...] = stage_ref[...].astype(BF16)

        @pl.when(j + 1 < n_col_tiles)
        def _():
            w_copy(j + 1).start()

    t = jnp.dot(a_ref[...], wbf_ref[...], preferred_element_type=F32)
    o_ref[...] = epilogue(t, *extra).astype(o_ref.dtype)


def _ep_plain(t):
    return t


def _ep_gate(t, g_ref):
    return jax.nn.sigmoid(g_ref[...].astype(F32)) * t


def _ep_gate_add(t, g_ref, prev_ref):
    return prev_ref[...].astype(F32) + jax.nn.sigmoid(g_ref[...].astype(F32)) * t


def _ep_residual(t, x_ref, m_ref, *, gate_row):
    return x_ref[...] + m_ref[0, gate_row:gate_row + 1, :] * t


def _mm(a, w, col0, ncols, m_rows, tm, tn, out_dtype, name, epilogue=_ep_plain, extra=(), extra_specs=()):
    k = a.shape[1]
    assert ncols % tn == 0 and m_rows % tm == 0 and col0 % LANES == 0
    return pl.pallas_call(
        functools.partial(_mm_kernel, epilogue=epilogue, col0=col0, tn=tn, n_col_tiles=ncols // tn),
        grid=(ncols // tn, m_rows // tm),
        in_specs=[pl.BlockSpec((tm, k), lambda j, i: (i, 0)), pl.BlockSpec(memory_space=pl.ANY)] + list(extra_specs),
        out_specs=pl.BlockSpec((tm, tn), lambda j, i: (i, j)),
        out_shape=jax.ShapeDtypeStruct((m_rows, ncols), out_dtype),
        scratch_shapes=[pltpu.VMEM((k, tn), F32), pltpu.VMEM((k, tn), BF16), pltpu.SemaphoreType.DMA(())],
        compiler_params=_cparams(("arbitrary", "arbitrary")),
        name=name,
    )(a, w, *extra)


def _qk_post_kernel(x_ref, w_ref, cos_ref, sin_ref, o_ref, *, n_heads, scale):
    lane = lax.broadcasted_iota(jnp.int32, (x_ref.shape[0], HEAD_DIM), 1)
    first = (lane % (HEAD_DIM // 2)) < (HEAD_DIM // 4)
    for h in range(n_heads):
        sl = slice(h * HEAD_DIM, (h + 1) * HEAD_DIM)
        xh = x_ref[:, sl].astype(F32)
        ms = jnp.mean(xh * xh, axis=-1, keepdims=True)
        y = xh * lax.rsqrt(ms + EPS) * w_ref[...]
        partner = jnp.where(first, pltpu.roll(y, HEAD_DIM - HEAD_DIM // 4, 1), pltpu.roll(y, HEAD_DIM // 4, 1))
        y = y * cos_ref[...] + partner * sin_ref[...]
        o_ref[:, sl] = (y * scale).astype(o_ref.dtype)


def _qk_post(src, col_blk0, n_heads_total, rows, w, cos_t, sin_t, seq, n_x_rows, scale, name):
    tm = _tile(min(seq, rows), 256)
    hpb = 4
    tpb = seq // tm
    nx = n_x_rows // tm
    tab_map = lambda i, j: (jnp.where(i < nx, i % tpb, tpb), 0)
    return pl.pallas_call(
        functools.partial(_qk_post_kernel, n_heads=hpb, scale=scale),
        grid=(rows // tm, n_heads_total // hpb),
        in_specs=[pl.BlockSpec((tm, hpb * HEAD_DIM), lambda i, j: (i, col_blk0 + j)),
                  pl.BlockSpec((1, HEAD_DIM), lambda i, j: (0, 0)),
                  pl.BlockSpec((tm, HEAD_DIM), tab_map),
                  pl.BlockSpec((tm, HEAD_DIM), tab_map)],
        out_specs=pl.BlockSpec((tm, hpb * HEAD_DIM), lambda i, j: (i, j)),
        out_shape=jax.ShapeDtypeStruct((rows, n_heads_total * HEAD_DIM), BF16),
        compiler_params=_cparams(("arbitrary", "arbitrary")),
        name=name,
    )(src, w.reshape(1, HEAD_DIM), cos_t, sin_t)


def _rope_tables(seq, tm):
    rows = seq // GRID_W
    row_pos = jnp.repeat(jnp.arange(rows, dtype=F32), GRID_W)
    col_pos = (jnp.arange(seq) % GRID_W).astype(F32)
    inv_freq = ROPE_THETA ** (-jnp.arange(ROPE_AXIS_FREQS, dtype=F32) / ROPE_AXIS_FREQS)
    ar = row_pos[:, None] * inv_freq
    ac = col_pos[:, None] * inv_freq
    cos_t = jnp.concatenate([jnp.cos(ar), jnp.cos(ar), jnp.cos(ac), jnp.cos(ac)], axis=-1)
    sin_t = jnp.concatenate([-jnp.sin(ar), jnp.sin(ar), -jnp.sin(ac), jnp.sin(ac)], axis=-1)
    cos_t = jnp.concatenate([cos_t, jnp.ones((tm, HEAD_DIM), F32)], axis=0)
    sin_t = jnp.concatenate([sin_t, jnp.zeros((tm, HEAD_DIM), F32)], axis=0)
    return cos_t, sin_t


def _attn_kernel(q_ref, wq_ref, cos_ref, sin_ref, kx_ref, kc_ref, vx_ref, vc_ref, o_ref):
    nt = (((1,), (1,)), ((), ()))
    kx, kc = kx_ref[...], kc_ref[...]

    def with_ones(v):
        lane = lax.broadcasted_iota(jnp.int32, v.shape, 1)
        return jnp.concatenate([v, jnp.where(lane == 0, 1.0, 0.0).astype(v.dtype)], axis=1)

    vx, vc = with_ones(vx_ref[...]), with_ones(vc_ref[...])
    tq = q_ref.shape[0]
    rows = tq // ATTN_ROW_SPLIT
    units = [(g, r) for g in range(GQA_GROUP) for r in range(ATTN_ROW_SPLIT)]

    lane = lax.broadcasted_iota(jnp.int32, (rows, HEAD_DIM), 1)
    first = (lane % (HEAD_DIM // 2)) < (HEAD_DIM // 4)

    def scores(u):
        g, r = u
        rs = slice(r * rows, (r + 1) * rows)
        xh = q_ref[rs, g * HEAD_DIM:(g + 1) * HEAD_DIM].astype(F32)
        y = xh * lax.rsqrt(jnp.mean(xh * xh, axis=-1, keepdims=True) + EPS) * wq_ref[...]
        partner = jnp.where(first, pltpu.roll(y, HEAD_DIM - HEAD_DIM // 4, 1), pltpu.roll(y, HEAD_DIM // 4, 1))
        q = (y * cos_ref[rs, :] + partner * sin_ref[rs, :]).astype(BF16)
        return (lax.dot_general(q, kx, nt, preferred_element_type=F32),
                lax.dot_general(q, kc, nt, preferred_element_type=F32))

    nxt = scores(units[0])
    for n, (g, r) in enumerate(units):
        s1, s2 = nxt
        if n + 1 < len(units):
            nxt = scores(units[n + 1])
        m = jnp.maximum(jnp.max(s1, axis=-1, keepdims=True), jnp.max(s2, axis=-1, keepdims=True))
        p1 = jnp.exp2(s1 - m).astype(BF16)
        p2 = jnp.exp2(s2 - m).astype(BF16)
        o = jnp.dot(p1, vx, preferred_element_type=F32) + jnp.dot(p2, vc, preferred_element_type=F32)
        o_ref[r * rows:(r + 1) * rows, g * HEAD_DIM:(g + 1) * HEAD_DIM] = (
            o[:, :HEAD_DIM] / o[:, HEAD_DIM:HEAD_DIM + 1]).astype(o_ref.dtype)


def _attention(p3, wq, cos_t, sin_t, k, p1, batch, seq, ctx_len):
    tq = _tile(seq, ATTN_TQ)
    qpb = seq // tq
    gw = GQA_GROUP * HEAD_DIM
    cblk0 = batch * seq // ctx_len
    return pl.pallas_call(
        _attn_kernel,
        grid=(batch, N_KV_HEADS, qpb),
        in_specs=[pl.BlockSpec((tq, gw), lambda b, h, i: (b * qpb + i, h)),
                  pl.BlockSpec((1, HEAD_DIM), lambda b, h, i: (0, 0)),
                  pl.BlockSpec((tq, HEAD_DIM), lambda b, h, i: (i, 0)),
                  pl.BlockSpec((tq, HEAD_DIM), lambda b, h, i: (i, 0)),
                  pl.BlockSpec((seq, HEAD_DIM), lambda b, h, i: (b, h)),
                  pl.BlockSpec((ctx_len, HEAD_DIM), lambda b, h, i: (cblk0 + b, h)),
                  pl.BlockSpec((seq, HEAD_DIM), lambda b, h, i: (b, N_KV_HEADS + h)),
                  pl.BlockSpec((ctx_len, HEAD_DIM), lambda b, h, i: (cblk0 + b, N_KV_HEADS + h))],
        out_specs=pl.BlockSpec((tq, gw), lambda b, h, i: (b * qpb + i, h)),
        out_shape=jax.ShapeDtypeStruct((batch * seq, N_Q_HEADS * HEAD_DIM), BF16),
        compiler_params=_cparams(("arbitrary", "arbitrary", "arbitrary")),
        name="attention",
    )(p3, wq.reshape(1, HEAD_DIM), cos_t, sin_t, k, k, p1, p1)


def _conv_kernel(prev_ref, cur_ref, next_ref, w_ref, b_ref, o_ref, buf_ref, *, tl, x_tiles, x_tpb, c_tpb):
    i = pl.program_id(0)
    j = jnp.where(i < x_tiles, i % x_tpb, (i - x_tiles) % c_tpb)
    n = jnp.where(i < x_tiles, x_tpb, c_tpb)
    buf_ref[0:HALO, :] = jnp.where(j == 0, 0.0, prev_ref[...].astype(F32))
    buf_ref[HALO:HALO + tl, :] = cur_ref[...].astype(F32)
    buf_ref[HALO + tl:, :] = jnp.where(j == n - 1, 0.0, next_ref[...].astype(F32))
    pad = CONV_W // 2
    xall = buf_ref[...]
    n_rows = xall.shape[0]
    acc = b_ref[...] + w_ref[pad:pad + 1, :] * xall[HALO:HALO + tl]
    for t in range(CONV_W):
        if t != pad:
            acc = acc + w_ref[t:t + 1, :] * pltpu.roll(xall, (pad - t) % n_rows, 0)[HALO:HALO + tl]
    o_ref[...] = _silu(acc).astype(o_ref.dtype)


def _conv_silu(p1, col0, ncols, conv_w, conv_b, seq, ctx_len, n_x_rows):
    rows = p1.shape[0]
    tl = _tile(min(seq, ctx_len), 256)
    tc = _tile(ncols, 2048)
    assert col0 % tc == 0
    cb0 = col0 // tc
    hb = tl // HALO
    last_hb = rows // HALO - 1
    return pl.pallas_call(
        functools.partial(_conv_kernel, tl=tl, x_tiles=n_x_rows // tl, x_tpb=seq // tl, c_tpb=ctx_len // tl),
        grid=(rows // tl, ncols // tc),
        in_specs=[pl.BlockSpec((HALO, tc), lambda i, j: (jnp.maximum(i * hb - 1, 0), cb0 + j)),
                  pl.BlockSpec((tl, tc), lambda i, j: (i, cb0 + j)),
                  pl.BlockSpec((HALO, tc), lambda i, j: (jnp.minimum((i + 1) * hb, last_hb), cb0 + j)),
                  pl.BlockSpec((CONV_W, tc), lambda i, j: (0, j)),
                  pl.BlockSpec((1, tc), lambda i, j: (0, j))],
        out_specs=pl.BlockSpec((tl, tc), lambda i, j: (i, j)),
        out_shape=jax.ShapeDtypeStruct((rows, ncols), BF16),
        scratch_shapes=[pltpu.VMEM((tl + 2 * HALO, tc), F32)],
        compiler_params=_cparams(("arbitrary", "arbitrary")),
        name="conv_silu",
    )(p1, p1, p1, conv_w, conv_b.reshape(1, ncols))


def _dt_prep_kernel(raw_ref, bias_ref, alog_ref, acum_ref, dt_ref, *, n_heads):
    v = raw_ref[...] + bias_ref[...]
    dt = jnp.maximum(v, 0.0) + jnp.log(1.0 + jnp.exp(-jnp.abs(v)))
    dta = dt * (-jnp.exp(alog_ref[...]))
    r = lax.broadcasted_iota(jnp.int32, (CHUNK, CHUNK), 0)
    c = lax.broadcasted_iota(jnp.int32, (CHUNK, CHUNK), 1)
    tril = jnp.where(r >= c, 1.0, 0.0).astype(BF16)
    triu = jnp.where(r <= c, 1.0, 0.0).astype(BF16)
    lane = lax.broadcasted_iota(jnp.int32, dta.shape, 1)
    acum = jnp.where(lane < n_heads, _dot_exact_lhs(tril, dta), _dot_exact_lhs(triu, dta))
    acum_ref[0] = acum.T
    dt_ref[0] = dt.T


def _dt_prep(dt_raw, bias2, alog2, n_heads):
    rows, w = dt_raw.shape
    nch = rows // CHUNK
    out = jax.ShapeDtypeStruct((nch, w, CHUNK), F32)
    return pl.pallas_call(
        functools.partial(_dt_prep_kernel, n_heads=n_heads),
        grid=(nch,),
        in_specs=[pl.BlockSpec((CHUNK, w), lambda i: (i, 0)),
                  pl.BlockSpec((1, w), lambda i: (0, 0)),
                  pl.BlockSpec((1, w), lambda i: (0, 0))],
        out_specs=[pl.BlockSpec((1, w, CHUNK), lambda i: (i, 0, 0))] * 2,
        out_shape=[out, out],
        compiler_params=_cparams(("arbitrary",)),
        name="dt_prep",
    )(dt_raw, bias2, alog2)


def _split3_f32(v):
    hi = v.astype(BF16).astype(F32)
    r1 = v - hi
    mid = r1.astype(BF16).astype(F32)
    lo = (r1 - mid).astype(BF16).astype(F32)
    return [hi, mid, lo]


def _ssd_consts(hpg):
    p = SSD_HEAD_DIM
    gw = hpg * p
    k = np.arange(CHUNK)[:, None]

    def expand(base, width, per):
        col_head = (np.arange(width) // per)[None, :]
        kk = k - base
        return ((kk >= 0) & (kk < 3 * hpg) & (kk % hpg == col_head)).astype(np.float32)

    mats = [expand(0, gw, p), expand(3 * hpg, gw, p), expand(9 * hpg, gw, p), expand(6 * hpg, hpg * CHUNK, CHUNK)]
    return jnp.asarray(np.concatenate(mats, axis=1), dtype=BF16)


def _ssd_kernel(xf_ref, bf_ref, cf_ref, af_ref, df_ref, xb_ref, bb_ref, cb_ref, ab_ref, db_ref, h0_ref, k_ref,
                yf_ref, yb_ref, hfin_ref, h_scr, *, hpg, gps):
    k = pl.program_id(2)
    nck = pl.num_programs(2)
    p = SSD_HEAD_DIM
    gw = hpg * p

    @pl.when(k == 0)
    def _():
        h_scr[...] = h0_ref[0]

    ri = lax.broadcasted_iota(jnp.int32, (CHUNK, CHUNK), 0)
    ci = lax.broadcasted_iota(jnp.int32, (CHUNK, CHUNK), 1)
    low_half = (lax.broadcasted_iota(jnp.int32, (CHUNK, gw), 1) % (2 * p)) < p
    nt = (((1,), (1,)), ((), ()))
    e_exp_a, e_to_end, e_gain = k_ref[:, 0:gw], k_ref[:, gw:2 * gw], k_ref[:, 2 * gw:3 * gw]
    e_col_a = k_ref[:, 3 * gw:]
    pad_rows = jnp.zeros((CHUNK - 12 * hpg, CHUNK), F32)

    dirs = ((xf_ref, bf_ref, cf_ref, af_ref, df_ref, yf_ref), (xb_ref, bb_ref, cb_ref, ab_ref, db_ref, yb_ref))
    for gi, d in [(gi, d) for gi in range(gps) for d in range(2)]:
        x_ref, b_ref, c_ref, a_ref, dt_ref, y_ref = dirs[d]
        gcols = slice(gi * gw, (gi + 1) * gw)
        ncols = slice(gi * SSD_STATE, (gi + 1) * SSD_STATE)
        row_a = a_ref[0, gi * hpg:(gi + 1) * hpg, :]
        row_dt = dt_ref[0, gi * hpg:(gi + 1) * hpg, :]
        if d == 0:
            mask = ri >= ci
            tot = row_a[:, CHUNK - 1:CHUNK]
        else:
            mask = ri <= ci
            tot = row_a[:, 0:1]
        exp_a = jnp.exp(row_a)
        to_end = jnp.exp(tot - row_a) * row_dt
        gain = jnp.broadcast_to(jnp.exp(tot), row_a.shape)
        table = jnp.concatenate(_split3_f32(exp_a) + _split3_f32(to_end) + _split3_f32(row_a)
                                + _split3_f32(gain) + [pad_rows], axis=0)
        tab_t = table.T.astype(BF16)
        exp_a_full = jnp.dot(tab_t, e_exp_a, preferred_element_type=F32)
        to_end_full = jnp.dot(tab_t, e_to_end, preferred_element_type=F32)
        gain_full = jnp.dot(tab_t[0:16], e_gain, preferred_element_type=F32)[0:1]
        col_a = jnp.dot(tab_t, e_col_a, preferred_element_type=F32)
        bc = b_ref[:, ncols]
        cc = c_ref[:, ncols]
        cb = lax.dot_general(cc, bc, nt, preferred_element_type=F32)
        x32 = x_ref[:, gcols].astype(F32)
        x_lo = jnp.where(low_half, x32, 0.0).astype(BF16)
        x_hi = jnp.where(low_half, 0.0, x32).astype(BF16)
        y_parts = []
        for q in range(hpg // 2):
            ws = []
            for r in (2 * q, 2 * q + 1):
                seg = col_a[:, r * CHUNK:(r + 1) * CHUNK] - row_a[r:r + 1, :]
                ws.append((cb * jnp.exp(jnp.where(mask, seg, -jnp.inf)) * row_dt[r:r + 1, :]).astype(BF16))
            sl = slice(q * 2 * p, (q + 1) * 2 * p)
            y_parts.append(jnp.dot(jnp.concatenate(ws, axis=1), jnp.concatenate([x_lo[:, sl], x_hi[:, sl]], axis=0),
                                   preferred_element_type=F32))
        y_state = jnp.dot(cc, h_scr[gi, d].astype(BF16), preferred_element_type=F32)
        y_ref[:, gcols] = (jnp.concatenate(y_parts, axis=1) + y_state * exp_a_full).astype(y_ref.dtype)
        bct = bc.astype(F32).T.astype(BF16)
        upd = jnp.dot(bct, (x32 * to_end_full).astype(BF16), preferred_element_type=F32)
        h_scr[gi, d] = h_scr[gi, d] * gain_full + upd

    @pl.when(k == nck - 1)
    def _():
        hfin_ref[0] = h_scr[...]


def _ssd(xbc, acum_t, dt_t, h0, consts, batch, length, chunk0, hpg, d_inner):
    nck = length // CHUNK
    gw = hpg * SSD_HEAD_DIM
    assert hpg % 2 == 0 and 2 * SSD_HEAD_DIM == LANES and 12 * hpg <= CHUNK
    gps = SSD_GROUPS_PER_STEP
    n_gsteps = SSD_GROUPS // gps
    b_blk0 = d_inner // (gps * SSD_STATE)
    c_blk0 = b_blk0 + n_gsteps
    assert SSD_GROUPS % gps == 0 and d_inner % (gps * SSD_STATE) == 0
    fwd = lambda b, g, k: chunk0 + b * nck + k
    bwd = lambda b, g, k: chunk0 + b * nck + (nck - 1 - k)

    def dir_specs(ch, d):
        return [pl.BlockSpec((CHUNK, gps * gw), lambda b, g, k: (ch(b, g, k), g)),
                pl.BlockSpec((CHUNK, gps * SSD_STATE), lambda b, g, k: (ch(b, g, k), b_blk0 + g)),
                pl.BlockSpec((CHUNK, gps * SSD_STATE), lambda b, g, k: (ch(b, g, k), c_blk0 + g)),
                pl.BlockSpec((1, gps * hpg, CHUNK), lambda b, g, k: (ch(b, g, k), d * n_gsteps + g, 0)),
                pl.BlockSpec((1, gps * hpg, CHUNK), lambda b, g, k: (ch(b, g, k), d * n_gsteps + g, 0))]

    st_spec = pl.BlockSpec((1, gps, 2, SSD_STATE, gw), lambda b, g, k: (b, g, 0, 0, 0))
    y_shape = jax.ShapeDtypeStruct((batch * length, d_inner), BF16)
    return pl.pallas_call(
        functools.partial(_ssd_kernel, hpg=hpg, gps=gps),
        grid=(batch, n_gsteps, nck),
        in_specs=dir_specs(fwd, 0) + dir_specs(bwd, 1) + [st_spec, pl.BlockSpec(consts.shape, lambda b, g, k: (0, 0))],
        out_specs=[pl.BlockSpec((CHUNK, gps * gw), lambda b, g, k: (b * nck + k, g)),
                   pl.BlockSpec((CHUNK, gps * gw), lambda b, g, k: (b * nck + (nck - 1 - k), g)),
                   st_spec],
        out_shape=[y_shape, y_shape, jax.ShapeDtypeStruct(h0.shape, F32)],
        scratch_shapes=[pltpu.VMEM((gps, 2, SSD_STATE, gw), F32)],
        compiler_params=_cparams(("arbitrary", "arbitrary", "arbitrary")),
        name="ssd",
    )(xbc, xbc, xbc, acum_t, dt_t, xbc, xbc, xbc, acum_t, dt_t, h0, consts)


def _ssd_norm_kernel(yf_ref, yb_ref, xs_ref, z_ref, ds_ref, w_ref, o_ref):
    y = yf_ref[...].astype(F32) + yb_ref[...].astype(F32) + ds_ref[...] * xs_ref[...].astype(F32)
    y = y * _silu(z_ref[...].astype(F32))
    ms = jnp.mean(y * y, axis=-1, keepdims=True)
    o_ref[...] = (y * lax.rsqrt(ms + EPS) * w_ref[...]).astype(o_ref.dtype)


def _ssd_norm(yf, yb, xbc, p3, z_blk, dskip_row, w):
    m, d = yf.shape
    tm = _tile(m, 256)
    row = lambda i: (i, 0)
    return pl.pallas_call(
        _ssd_norm_kernel,
        grid=(m // tm,),
        in_specs=[pl.BlockSpec((tm, d), row), pl.BlockSpec((tm, d), row), pl.BlockSpec((tm, d), row),
                  pl.BlockSpec((tm, d), lambda i: (i, z_blk)),
                  pl.BlockSpec((1, d), lambda i: (0, 0)), pl.BlockSpec((1, d), lambda i: (0, 0))],
        out_specs=pl.BlockSpec((tm, d), row),
        out_shape=jax.ShapeDtypeStruct((m, d), BF16),
        compiler_params=_cparams(("arbitrary",)),
        name="ssd_norm",
    )(yf, yb, xbc, p3, dskip_row, w.reshape(1, d))


def _merge(attn, ynorm, w1, w2, p3, g1_col0, g2_col0):
    m, _ = attn.shape
    n = w1.shape[1]
    tm, tn = _tile(m, 512), _tile(n, 1024)
    blk = lambda c0: pl.BlockSpec((tm, tn), lambda j, i: (i, c0 // tn + j))
    part = _mm(attn, w1, 0, n, m, tm, tn, BF16, "merge_attn", _ep_gate, (p3,), (blk(g1_col0),))
    return _mm(ynorm, w2, 0, n, m, tm, tn, BF16, "merge_ssd", _ep_gate_add, (p3, part), (blk(g2_col0), blk(0)))


def _out_proj(merged, w, x2, mod3, seq, gate_row):
    m, _ = merged.shape
    n = w.shape[1]
    tm, tn = _tile(seq, 512), _tile(n, 1024)
    tpb = seq // tm
    return _mm(merged, w, 0, n, m, tm, tn, F32, "out_proj", functools.partial(_ep_residual, gate_row=gate_row),
               (x2, mod3), (pl.BlockSpec((tm, tn), lambda j, i: (i, j)),
                            pl.BlockSpec((1, N_MOD, tn), lambda j, i: (i // tpb, 0, j))))


def _ffn_norm_route_kernel(x_ref, w_ref, m_ref, wr_ref, br_ref, h_ref, route_ref):
    h = _norm_mod(x_ref[...], w_ref[...], m_ref[0, 3:4, :], m_ref[0, 4:5, :])
    h_ref[...] = h
    h_hi = h.astype(BF16)
    h_lo = (h - h_hi.astype(F32)).astype(BF16)
    wr = wr_ref[...]
    w_hi = wr.astype(BF16)
    w_lo = (wr - w_hi.astype(F32)).astype(BF16)
    logits = (jnp.dot(h_hi, w_hi, preferred_element_type=F32) + jnp.dot(h_hi, w_lo, preferred_element_type=F32)
              + jnp.dot(h_lo, w_hi, preferred_element_type=F32)) + br_ref[...]
    lane = lax.broadcasted_iota(jnp.int32, logits.shape, 1).astype(F32)
    big = float(LANES)
    neg = -jnp.inf
    gl = jnp.where(lane < N_GROUPS, logits, neg)
    gmax = jnp.max(gl, axis=-1, keepdims=True)
    gidx = jnp.min(jnp.where(gl == gmax, lane, big), axis=-1, keepdims=True)
    g_prob = 1.0 / jnp.sum(jnp.exp(gl - gmax), axis=-1, keepdims=True)
    lo = N_GROUPS + gidx * EXPERTS_PER_GROUP
    el = jnp.where((lane >= lo) & (lane < lo + EXPERTS_PER_GROUP), logits, neg)
    m1 = jnp.max(el, axis=-1, keepdims=True)
    i1 = jnp.min(jnp.where(el == m1, lane, big), axis=-1, keepdims=True)
    el2 = jnp.where(lane == i1, neg, el)
    m2 = jnp.max(el2, axis=-1, keepdims=True)
    i2 = jnp.min(jnp.where(el2 == m2, lane, big), axis=-1, keepdims=True)
    z = jnp.sum(jnp.exp(el - m1), axis=-1, keepdims=True)
    p1 = 1.0 / z
    p2 = jnp.exp(m2 - m1) / z
    w1 = g_prob * p1 / (p1 + p2)
    w2 = g_prob * p2 / (p1 + p2)
    route = jnp.where(lane == 0, i1 - N_GROUPS, jnp.where(lane == 1, i2 - N_GROUPS,
                      jnp.where(lane == 2, w1, jnp.where(lane == 3, w2, 0.0))))
    route_ref[...] = route


def _ffn_norm_route(x1, w, mod3, wr, br, seq):
    m, d = x1.shape
    tm = _tile(seq, 256)
    tpb = seq // tm
    return pl.pallas_call(
        _ffn_norm_route_kernel,
        grid=(m // tm,),
        in_specs=[pl.BlockSpec((tm, d), lambda i: (i, 0)),
                  pl.BlockSpec((1, d), lambda i: (0, 0)),
                  pl.BlockSpec((1, N_MOD, d), lambda i: (i // tpb, 0, 0)),
                  pl.BlockSpec((d, LANES), lambda i: (0, 0)),
                  pl.BlockSpec((1, LANES), lambda i: (0, 0))],
        out_specs=[pl.BlockSpec((tm, d), lambda i: (i, 0)), pl.BlockSpec((tm, LANES), lambda i: (i, 0))],
        out_shape=[jax.ShapeDtypeStruct((m, d), F32), jax.ShapeDtypeStruct((m, LANES), F32)],
        compiler_params=_cparams(("arbitrary",)),
        name="ffn_norm_route",
    )(x1, w.reshape(1, d), mod3, wr, br)


def _gather_kernel(tok_ref, nv_ref, src_ref, o_ref, buf_ref, sem, *, blk):
    i = pl.program_id(0)
    n = pl.num_programs(0)
    groups = lambda b: (nv_ref[b] + ROW_GROUP - 1) // ROW_GROUP

    def request(b, slot):
        def issue(g, c):
            for u in range(ROW_GROUP):
                r = g * ROW_GROUP + u
                pltpu.make_async_copy(src_ref.at[pl.ds(tok_ref[b * blk + r], 1)], buf_ref.at[slot, pl.ds(r, 1)],
                                      sem.at[slot]).start()
            return c

        lax.fori_loop(0, groups(b), issue, 0)

    @pl.when(i == 0)
    def _():
        request(0, 0)

    @pl.when(i + 1 < n)
    def _():
        request(i + 1, (i + 1) % 2)

    slot = i % 2
    n_groups = groups(i)

    def zero(g, c):
        r0 = pl.multiple_of(g * ROW_GROUP, ROW_GROUP)
        buf_ref[slot, pl.ds(r0, ROW_GROUP), :] = jnp.zeros((ROW_GROUP, buf_ref.shape[2]), buf_ref.dtype)
        return c

    lax.fori_loop(n_groups, blk // ROW_GROUP, zero, 0)

    def drain(g, c):
        pltpu.make_async_copy(src_ref.at[pl.ds(0, ROW_GROUP)], buf_ref.at[slot, pl.ds(0, ROW_GROUP)],
                              sem.at[slot]).wait()
        return c

    lax.fori_loop(0, n_groups, drain, 0)
    o_ref[...] = buf_ref[slot].astype(o_ref.dtype)


def _gather_rows(row_tok, n_valid, src, blk):
    n_rows = row_tok.shape[0]
    dp = d = src.shape[1]
    return pl.pallas_call(
        functools.partial(_gather_kernel, blk=blk),
        grid_spec=pltpu.PrefetchScalarGridSpec(
            num_scalar_prefetch=2,
            grid=(n_rows // blk,),
            in_specs=[pl.BlockSpec(memory_space=pl.ANY)],
            out_specs=pl.BlockSpec((blk, d), lambda i, tok, nv: (i, 0)),
            scratch_shapes=[pltpu.VMEM((2, blk, dp), src.dtype), pltpu.SemaphoreType.DMA((2,))]),
        out_shape=jax.ShapeDtypeStruct((n_rows, d), BF16),
        compiler_params=_cparams(("arbitrary",)),
        name="moe_gather",
    )(row_tok, n_valid, src)


def _expert_mm_kernel(ie_ref, it_ref, ib_ref, if_ref, iv_ref, ne_ref, nt_ref, hn_ref, a_ref, *rest, n_w, tn, compute):
    w_hbm = rest[:n_w]
    o_ref = rest[n_w]
    stage = rest[n_w + 1:2 * n_w + 1]
    w_bf = rest[2 * n_w + 1:3 * n_w + 1]
    sem = rest[3 * n_w + 1]
    i = pl.program_id(0)

    def copies(e, t):
        col = pl.multiple_of(t * tn, LANES)
        return [pltpu.make_async_copy(w_hbm[n].at[e, :, pl.ds(col, tn)], stage[n], sem.at[n]) for n in range(n_w)]

    @pl.when(i == 0)
    def _():
        for cp in copies(ie_ref[0], it_ref[0]):
            cp.start()

    @pl.when(if_ref[i] == 1)
    def _():
        for cp in copies(ie_ref[i], it_ref[i]):
            cp.wait()
        for n in range(n_w):
            w_bf[n][...] = stage[n][...].astype(BF16)

        @pl.when(hn_ref[i] == 1)
        def _():
            for cp in copies(ne_ref[i], nt_ref[i]):
                cp.start()

    @pl.when(iv_ref[i] == 1)
    def _():
        o_ref[...] = compute(a_ref[...], [w[...] for w in w_bf]).astype(o_ref.dtype)

    @pl.when(iv_ref[i] == 0)
    def _():
        o_ref[...] = jnp.zeros(o_ref.shape, o_ref.dtype)


def _expert_mm(items, a, weights, blk, tn, compute, out_dtype, name):
    n_items = items[0].shape[0]
    n_rows, k = a.shape
    n = weights[0].shape[2]
    n_w = len(weights)
    imap_a = lambda i, ie, it, ib, *_: (ib[i], 0)
    imap_o = lambda i, ie, it, ib, *_: (ib[i], it[i])
    return pl.pallas_call(
        functools.partial(_expert_mm_kernel, n_w=n_w, tn=tn, compute=compute),
        grid_spec=pltpu.PrefetchScalarGridSpec(
            num_scalar_prefetch=len(items),
            grid=(n_items,),
            in_specs=[pl.BlockSpec((blk, k), imap_a)] + [pl.BlockSpec(memory_space=pl.ANY)] * n_w,
            out_specs=pl.BlockSpec((blk, tn), imap_o),
            scratch_shapes=([pltpu.VMEM((k, tn), F32)] * n_w + [pltpu.VMEM((k, tn), BF16)] * n_w
                            + [pltpu.SemaphoreType.DMA((n_w,))])),
        out_shape=jax.ShapeDtypeStruct((n_rows, n), out_dtype),
        compiler_params=_cparams(("arbitrary",)),
        name=name,
    )(*items, a, *weights)


def _up_compute(xb, ws):
    a = jnp.dot(xb, ws[0], preferred_element_type=F32)
    u = jnp.dot(xb, ws[1], preferred_element_type=F32)
    return _silu(a) * u


def _down_compute(hb, ws):
    return jnp.dot(hb, ws[0], preferred_element_type=F32)


def _combine_kernel(pos_ref, y_ref, x_ref, rt_ref, m_ref, o_ref, buf_ref, sem, *, tm):
    i = pl.program_id(0)
    n = pl.num_programs(0)

    def request(t, slot):
        def issue(g, c):
            for u in range(ROW_GROUP):
                r = g * ROW_GROUP + u
                for kk in range(TOP_K):
                    pltpu.make_async_copy(y_ref.at[pl.ds(pos_ref[TOP_K * (t * tm + r) + kk], 1)],
                                          buf_ref.at[slot, kk, pl.ds(r, 1)], sem.at[slot]).start()
            return c

        lax.fori_loop(0, tm // ROW_GROUP, issue, 0)

    @pl.when(i == 0)
    def _():
        request(0, 0)

    @pl.when(i + 1 < n)
    def _():
        request(i + 1, (i + 1) % 2)

    slot = i % 2
    for kk in range(TOP_K):
        pltpu.make_async_copy(y_ref.at[pl.ds(0, tm)], buf_ref.at[slot, kk], sem.at[slot]).wait()
    rt = rt_ref[...]
    moe = rt[:, 2:3] * buf_ref[slot, 0] + rt[:, 3:4] * buf_ref[slot, 1]
    o_ref[...] = x_ref[...] + m_ref[0, 5:6, :] * moe


def _combine(pos, yexp, x1, route, mod3, seq):
    m, d = x1.shape
    tm = _tile(seq, 128)
    tpb = seq // tm
    return pl.pallas_call(
        functools.partial(_combine_kernel, tm=tm),
        grid_spec=pltpu.PrefetchScalarGridSpec(
            num_scalar_prefetch=1,
            grid=(m // tm,),
            in_specs=[pl.BlockSpec(memory_space=pl.ANY),
                      pl.BlockSpec((tm, d), lambda i, pos: (i, 0)),
                      pl.BlockSpec((tm, LANES), lambda i, pos: (i, 0)),
                      pl.BlockSpec((1, N_MOD, d), lambda i, pos: (i // tpb, 0, 0))],
            out_specs=pl.BlockSpec((tm, d), lambda i, pos: (i, 0)),
            scratch_shapes=[pltpu.VMEM((2, TOP_K, tm, d), F32), pltpu.SemaphoreType.DMA((2,))]),
        out_shape=jax.ShapeDtypeStruct((m, d), F32),
        compiler_params=_cparams(("arbitrary",)),
        name="moe_combine",
    )(pos, yexp, x1, route, mod3)


def _moe_plan(route, blk, n_tiles_up, n_tiles_down):
    t = route.shape[0]
    n_assign = t * TOP_K
    nb_max = n_assign // blk + N_EXPERTS
    flat_e = route[:, 0:TOP_K].astype(jnp.int32).reshape(-1)
    onehot = (flat_e[:, None] == jnp.arange(N_EXPERTS, dtype=jnp.int32)[None, :]).astype(jnp.int32)
    cum = jnp.cumsum(onehot, axis=0)
    rank = jnp.take_along_axis(cum, flat_e[:, None], axis=1)[:, 0] - 1
    counts = cum[-1]
    nblk = (counts + blk - 1) // blk
    pend = jnp.cumsum(nblk)
    pstart = pend - nblk
    dest = pstart[flat_e] * blk + rank
    row_tok = jnp.zeros((nb_max * blk,), jnp.int32).at[dest].set(jnp.arange(n_assign, dtype=jnp.int32) // TOP_K)
    total = pend[-1]
    unused = jnp.maximum(nb_max - total, 1)

    def items(n_tiles):
        j = jnp.arange(n_tiles * nb_max, dtype=jnp.int32)
        valid = j < n_tiles * total
        jj = jnp.minimum(j, n_tiles * total - 1)
        e = jnp.minimum(jnp.sum((jj[:, None] >= n_tiles * pend[None, :]).astype(jnp.int32), axis=1), N_EXPERTS - 1)
        local = jj - n_tiles * pstart[e]
        nb_e = jnp.maximum(nblk[e], 1)
        u = j - n_tiles * total
        tile = jnp.where(valid, local // nb_e, u // unused)
        b = jnp.where(valid, pstart[e] + local % nb_e, total + u % unused)
        first = valid & (local % nb_e == 0)
        nxt = j + nb_e
        has_next = first & (nxt < n_tiles * total)
        nxt = jnp.minimum(nxt, n_tiles * nb_max - 1)
        i32 = lambda v: v.astype(jnp.int32)
        return (e, i32(tile), i32(b), i32(first), i32(valid), e[nxt], i32(tile[nxt]), i32(has_next))

    blk_ids = jnp.arange(nb_max, dtype=jnp.int32)
    blk_e = jnp.minimum(jnp.sum((blk_ids[:, None] >= pend[None, :]).astype(jnp.int32), axis=1), N_EXPERTS - 1)
    n_valid = jnp.clip(counts[blk_e] - (blk_ids - pstart[blk_e]) * blk, 0, blk)
    n_valid = jnp.where(blk_ids < total, n_valid, 0).astype(jnp.int32)
    return row_tok, n_valid, dest, items(n_tiles_up), items(n_tiles_down)


def kernel(x, c, ctx, c_ctx, w_ada, b_ada, norm_mix_w, norm_ffn_w, w_in, q_norm_w, k_norm_w, conv_w, conv_b,
           a_log_f, a_log_b, dt_bias_f, dt_bias_b, d_skip, ssd_norm_w, w_attn_proj, w_ssd_proj, w_out,
           w_router_group, b_router_group, w_router_expert, b_router_expert, w_exp_gate, w_exp_up, w_exp_down):
    batch, seq, d = x.shape
    ctx_len = ctx.shape[1]
    assert w_ada.shape[0] == 1, "single layer: the context stream is read, never updated"
    d_inner = d
    n_ssd_heads = d_inner // SSD_HEAD_DIM
    hpg = n_ssd_heads // SSD_GROUPS
    kv_dim = N_KV_HEADS * HEAD_DIM
    q_dim = N_Q_HEADS * HEAD_DIM
    bc_dim = SSD_GROUPS * SSD_STATE
    xbc_dim = d_inner + 2 * bc_dim
    p1_cols = 2 * kv_dim + xbc_dim
    dt_cols = 2 * n_ssd_heads
    p3_col0 = p1_cols + dt_cols
    p3_cols = q_dim + 2 * d + d_inner
    assert dt_cols == LANES
    mx, mc = batch * seq, batch * ctx_len

    x2 = x.reshape(mx, d)
    c2 = ctx.reshape(mc, d)
    w_in0 = w_in[0]

    cond8 = jnp.zeros((8, d), F32).at[0:batch].set(c).at[batch].set(c_ctx)
    assert batch == 2
    mod3 = _adaln(cond8, w_ada[0], b_ada[0]).reshape(8, N_MOD, d)

    h_all = _norm_mix(x2, c2, norm_mix_w[0], mod3, seq)
    tm_all = 1088 if (mx + mc) % 1088 == 0 else _tile(mx + mc, 512)
    p1 = _mm(h_all, w_in0, 0, p1_cols, mx + mc, tm_all, 1024, BF16, "in_proj_kvx")
    dt_raw = _mm(h_all, w_in0, p1_cols, dt_cols, mx + mc, _tile(mx + mc, 512), dt_cols, F32, "in_proj_dt")
    p3 = _mm(h_all, w_in0, p3_col0, p3_cols, mx, _tile(mx, 1024), 1024, BF16, "in_proj_qgz")

    tmq = _tile(min(seq, mc), 256)
    cos_t, sin_t = _rope_tables(seq, tmq)
    k = _qk_post(p1, 0, N_KV_HEADS, mx + mc, k_norm_w[0], cos_t, sin_t, seq, mx, 1.0, "k_post")
    attn = _attention(p3, q_norm_w[0] * (HEAD_DIM ** -0.5 * LOG2_E), cos_t, sin_t, k, p1, batch, seq, ctx_len)

    xbc = _conv_silu(p1, 2 * kv_dim, xbc_dim, conv_w[0], conv_b[0], seq, ctx_len, mx)
    bias2 = jnp.concatenate([dt_bias_f[0], dt_bias_b[0]]).reshape(1, dt_cols)
    alog2 = jnp.concatenate([a_log_f[0], a_log_b[0]]).reshape(1, dt_cols)
    acum_t, dt_t = _dt_prep(dt_raw, bias2, alog2, n_ssd_heads)
    h_zero = jnp.zeros((batch, SSD_GROUPS, 2, SSD_STATE, hpg * SSD_HEAD_DIM), F32)
    ssd_consts = _ssd_consts(hpg)
    _, _, h_ctx = _ssd(xbc, acum_t, dt_t, h_zero, ssd_consts, batch, ctx_len, mx // CHUNK, hpg, d_inner)
    yf, yb, _ = _ssd(xbc, acum_t, dt_t, h_ctx, ssd_consts, batch, seq, 0, hpg, d_inner)
    dskip_row = jnp.repeat(d_skip[0], SSD_HEAD_DIM).reshape(1, d_inner)
    ynorm = _ssd_norm(yf, yb, xbc, p3, (q_dim + 2 * d) // d_inner, dskip_row, ssd_norm_w[0])

    merged = _merge(attn, ynorm, w_attn_proj[0], w_ssd_proj[0], p3, q_dim, q_dim + d)
    x1 = _out_proj(merged, w_out[0], x2, mod3, seq, 2)

    wr = jnp.zeros((d, LANES), F32).at[:, :N_GROUPS].set(w_router_group[0])
    wr = wr.at[:, N_GROUPS:N_GROUPS + N_EXPERTS].set(w_router_expert[0])
    br = jnp.zeros((1, LANES), F32).at[0, :N_GROUPS].set(b_router_group[0])
    br = br.at[0, N_GROUPS:N_GROUPS + N_EXPERTS].set(b_router_expert[0])
    h2, route = _ffn_norm_route(x1, norm_ffn_w[0], mod3, wr, br, seq)
    d_exp = w_exp_gate.shape[-1]
    tn_up, tn_down = _tile(d_exp, 512), _tile(d, 4096)
    row_tok, n_valid, dest, items_up, items_down = _moe_plan(route, MOE_BLK, d_exp // tn_up, d // tn_down)
    xg = _gather_rows(row_tok, n_valid, h2, MOE_BLK)
    hid = _expert_mm(items_up, xg, (w_exp_gate[0], w_exp_up[0]), MOE_BLK, tn_up, _up_compute, BF16, "moe_up")
    yexp = _expert_mm(items_down, hid, (w_exp_down[0],), MOE_BLK, tn_down, _down_compute, F32, "moe_down")
    out = _combine(dest, yexp, x1, route, mod3, seq)
    return out.reshape(batch, seq, d)
```
